```python
import jax, jax.numpy as jnp
from jax import lax
import numpy as np

D_MODEL = 2048
BATCH = 2
SEQ = 4096
DEPTH = 1
DEC_BATCH = 8
DEC_SEQ = 4
PAST_LEN = 16384
PAGE_SIZE = 128

HEAD_DIM = 128
GDN_HEADS = D_MODEL // HEAD_DIM
GDN_WIDTH = GDN_HEADS * HEAD_DIM
CONV_WIDTH = 4
GDN_CHUNK = 64
NSA_HEADS = D_MODEL // HEAD_DIM
NSA_KV_HEADS = NSA_HEADS // 4
NSA_GROUP = NSA_HEADS // NSA_KV_HEADS
NSA_WIDTH = NSA_HEADS * HEAD_DIM
NSA_KV_WIDTH = NSA_KV_HEADS * HEAD_DIM
CMP_BLOCK = 32
SEL_BLOCK = 64
SEL_TOPK = 16
WINDOW = 512
Q_BLOCK = 64
PLE_DIM = 256
FFN_HIDDEN = -(-8 * D_MODEL // (3 * 256)) * 256
PROJ_WIDTH = 4 * GDN_WIDTH + 2 * GDN_HEADS + NSA_WIDTH + 6 * NSA_KV_WIDTH + 3 * NSA_HEADS + 2 * D_MODEL
RMS_EPS = 1e-6
NEG_INF = -1e30
FORCE_BONUS = float(NSA_GROUP + 1)

kernel_name = "hybrid_gdn_nsa_decode_step"


def rmsnorm(x, g):
    xf = x.astype(jnp.float32)
    r = lax.rsqrt(jnp.mean(xf * xf, axis=-1, keepdims=True) + RMS_EPS)
    return (xf * r).astype(x.dtype) * g


def l2norm(x):
    return x * lax.rsqrt(jnp.sum(x * x, axis=-1, keepdims=True) + RMS_EPS)


def masked_softmax(s, valid):
    p = jax.nn.softmax(jnp.where(valid, s, NEG_INF), axis=-1)
    return jnp.where(valid, p, 0.0)


def alibi_slopes():
    h = jnp.arange(1, NSA_HEADS + 1, dtype=jnp.float32)
    return jnp.exp2(-8.0 * h / NSA_HEADS).reshape(NSA_KV_HEADS, NSA_GROUP)


def split_projection(z):
    sizes = (3 * GDN_WIDTH, GDN_WIDTH, GDN_HEADS, GDN_HEADS, NSA_WIDTH,
             2 * NSA_KV_WIDTH, 2 * NSA_KV_WIDTH, 2 * NSA_KV_WIDTH, 3 * NSA_HEADS, 2 * D_MODEL)
    offsets, acc = [], 0
    for s in sizes[:-1]:
        acc += s
        offsets.append(acc)
    return jnp.split(z, offsets, axis=-1)


def causal_conv_silu(u, buf, w):
    T = u.shape[1]
    full = jnp.concatenate([buf.astype(u.dtype), u], axis=1)
    out = full[:, 0:T] * w[0]
    for j in range(1, CONV_WIDTH):
        out = out + full[:, j:j + T] * w[j]
    return jax.nn.silu(out), full[:, T:]


def gdn_chunked(q, k, v, g, beta, S0):
    B, T, H, dk = q.shape
    dv = v.shape[-1]
    C = min(GDN_CHUNK, T)
    N = T // C

    def chunks(a):
        a = a.reshape((B, N, C, H) + a.shape[3:])
        return jnp.moveaxis(jnp.moveaxis(a, 1, 0), 3, 2)

    qc, kc, vc, gc, bc = chunks(q), chunks(k), chunks(v), chunks(g), chunks(beta)
    gc = jnp.cumsum(gc, axis=-1)
    incl = jnp.tril(jnp.ones((C, C), dtype=bool))
    strict = jnp.tril(jnp.ones((C, C), dtype=bool), -1)
    eye = jnp.eye(C, dtype=jnp.float32)

    def step(S, inp):
        qi, ki, vi, gi, bi = inp
        decay = jnp.exp(jnp.where(incl, gi[..., :, None] - gi[..., None, :], NEG_INF))
        kk = jnp.einsum('bhid,bhjd->bhij', ki, ki)
        L = jnp.where(strict, kk * decay, 0.0) * bi[..., :, None]
        rhs = jnp.concatenate([vi * bi[..., None], ki * (bi * jnp.exp(gi))[..., None]], axis=-1)
        sol = lax.linalg.triangular_solve(eye + L, rhs, left_side=True, lower=True, unit_diagonal=True)
        u, w = sol[..., :dv], sol[..., dv:]
        v_new = u - jnp.einsum('bhck,bhkv->bhcv', w, S)
        qk = jnp.einsum('bhid,bhjd->bhij', qi, ki) * decay
        o = (jnp.einsum('bhck,bhkv->bhcv', qi * jnp.exp(gi)[..., None], S)
             + jnp.einsum('bhij,bhjv->bhiv', qk, v_new))
        g_last = gi[..., -1:]
        S = (S * jnp.exp(g_last)[..., None]
             + jnp.einsum('bhck,bhcv->bhkv', ki * jnp.exp(g_last - gi)[..., None], v_new))
        return S, o

    S, o = lax.scan(step, S0, (qc, kc, vc, gc, bc))
    o = jnp.moveaxis(jnp.moveaxis(o, 3, 2), 0, 1).reshape(B, T, H, dv)
    return o, S


def nsa_block(q, gates, t_pos, kvc, c_end, sel, kw, w_pos, slopes):
    B, Tq, G, R, d = q.shape
    f32 = jnp.float32
    sl = slopes[None, None, :, :, None]
    kc, vc = kvc[:, :, 0], kvc[:, :, 1]
    dist_c = t_pos[:, None] - c_end[None, :]
    s = jnp.einsum('btgrd,bngd->btgrn', q, kc).astype(f32) - sl * dist_c.astype(f32)[None, :, None, None, :]
    p_c = masked_softmax(s, (dist_c >= 0)[None, :, None, None, :])
    o_c = jnp.einsum('btgrn,bngd->btgrd', p_c.astype(vc.dtype), vc)
    NS = sel.shape[2]
    ratio = SEL_BLOCK // CMP_BLOCK
    NC = p_c.shape[-1]
    imp = jnp.pad(p_c.sum(axis=3), ((0, 0), (0, 0), (0, 0), (0, NS * ratio - NC)))
    imp = imp.reshape(B, Tq, G, NS, ratio).sum(-1)
    blk = jnp.arange(NS)
    cur = t_pos // SEL_BLOCK
    forced = (blk[None, :] == 0) | (blk[None, :] == cur[:, None]) | (blk[None, :] == cur[:, None] - 1)
    avail = blk[None, :] * SEL_BLOCK <= t_pos[:, None]
    score = jnp.where(forced[None, :, None, :], imp + FORCE_BONUS,
                      jnp.where(avail[None, :, None, :], imp, -1.0))
    k_sel = min(SEL_TOPK, NS)
    _, idx = lax.top_k(score, k_sel)
    idx_t = idx.transpose(0, 2, 1, 3).reshape(B, G, Tq * k_sel)
    kv = sel[jnp.arange(B)[:, None, None], jnp.arange(G)[None, :, None], idx_t]
    kv = kv.reshape(B, G, Tq, k_sel * SEL_BLOCK, 2, d)
    kpos = (idx[..., None] * SEL_BLOCK + jnp.arange(SEL_BLOCK)).reshape(B, Tq, G, k_sel * SEL_BLOCK)
    dist_s = t_pos[None, :, None, None] - kpos
    s = jnp.einsum('btgrd,bgtkd->btgrk', q, kv[..., 0, :]).astype(f32) - sl * dist_s.astype(f32)[:, :, :, None, :]
    p_s = masked_softmax(s, (dist_s >= 0)[:, :, :, None, :])
    o_s = jnp.einsum('btgrk,bgtkd->btgrd', p_s.astype(kv.dtype), kv[..., 1, :])
    dist_w = t_pos[:, None] - w_pos[None, :]
    valid_w = (dist_w >= 0) & (dist_w < WINDOW) & (w_pos[None, :] >= 0)
    s = jnp.einsum('btgrd,bsgd->btgrs', q, kw[:, :, 0]).astype(f32) - sl * dist_w.astype(f32)[None, :, None, None, :]
    p_w = masked_softmax(s, valid_w[None, :, None, None, :])
    o_w = jnp.einsum('btgrs,bsgd->btgrd', p_w.astype(kw.dtype), kw[:, :, 1])
    return gates[..., 0:1] * o_c + gates[..., 1:2] * o_s + gates[..., 2:3] * o_w


def nsa_attention(q_n, gate_n, cmp_all, sel_all, kw_pad, past_len, w_cmp):
    B, T, _ = q_n.shape
    Tk = cmp_all.shape[1]
    G, R, d = NSA_KV_HEADS, NSA_GROUP, HEAD_DIM
    q = q_n.reshape(B, T, G, R, d) * (HEAD_DIM ** -0.5)
    gates = jax.nn.sigmoid(gate_n.reshape(B, T, G, R, 3))
    NC = Tk // CMP_BLOCK
    blocks = cmp_all[:, :NC * CMP_BLOCK].reshape(B, NC, CMP_BLOCK, 2, G, d)
    kvc = jnp.einsum('bncxgd,cxgd->bnxgd', blocks, w_cmp)
    c_end = jnp.arange(NC) * CMP_BLOCK + (CMP_BLOCK - 1)
    NS = -(-Tk // SEL_BLOCK)
    sel = jnp.pad(sel_all, ((0, 0), (0, NS * SEL_BLOCK - Tk), (0, 0), (0, 0), (0, 0)))
    sel = sel.reshape(B, NS, SEL_BLOCK, 2, G, d).transpose(0, 4, 1, 2, 3, 5)
    slopes = alibi_slopes()
    qb = min(Q_BLOCK, T)
    nb = T // qb

    def one_block(i):
        s0 = i * qb
        qi = lax.dynamic_slice_in_dim(q, s0, qb, axis=1)
        gi = lax.dynamic_slice_in_dim(gates, s0, qb, axis=1)
        kwi = lax.dynamic_slice_in_dim(kw_pad, s0, WINDOW + qb, axis=1)
        t_pos = past_len + s0 + jnp.arange(qb)
        w_pos = past_len - WINDOW + s0 + jnp.arange(WINDOW + qb)
        return nsa_block(qi, gi, t_pos, kvc, c_end, sel, kwi, w_pos, slopes)

    o = lax.map(one_block, jnp.arange(nb))
    return o.transpose(1, 0, 2, 3, 4, 5).reshape(B, T, NSA_WIDTH)


def trunk_layer(x, ple, past_len, win_buf, conv_buf, S0, cmp_past, sel_past, win_past,
                g_pre1, w_in, conv_w, A_log, dt_bias, gdn_norm_w, w_cmp, w_o, g_post1,
                g_pre2, w_ffn_in, w_ffn_out, g_post2, w_ple, w_ple_gate):
    B, T, _ = x.shape
    f32 = jnp.float32
    h = rmsnorm(x, g_pre1)
    (qkv_a, z_a, a_a, b_a, q_n, cmp_new, sel_new, win_new, gate_n, gate_m) = split_projection(h @ w_in)
    qkv_c, conv_new = causal_conv_silu(qkv_a, conv_buf, conv_w)
    qa, ka, va = [a.reshape(B, T, GDN_HEADS, HEAD_DIM) for a in jnp.split(qkv_c.astype(f32), 3, axis=-1)]
    qa = l2norm(qa) * (HEAD_DIM ** -0.5)
    ka = l2norm(ka)
    beta = jax.nn.sigmoid(b_a.astype(f32))
    g = -jnp.exp(A_log.astype(f32)) * jax.nn.softplus(a_a.astype(f32) + dt_bias.astype(f32))
    o_a, S_new = gdn_chunked(qa, ka, va, g, beta, S0.astype(f32))
    o_a = rmsnorm(o_a, gdn_norm_w.astype(f32)) * jax.nn.silu(z_a.astype(f32).reshape(B, T, GDN_HEADS, HEAD_DIM))
    o_a = o_a.reshape(B, T, GDN_WIDTH).astype(x.dtype)
    kv_shape = (B, T, 2, NSA_KV_HEADS, HEAD_DIM)
    cmp_new = cmp_new.reshape(kv_shape)
    sel_new = sel_new.reshape(kv_shape)
    win_new = win_new.reshape(kv_shape)
    cmp_all = jnp.concatenate([cmp_past.astype(x.dtype), cmp_new], axis=1)
    sel_all = jnp.concatenate([sel_past.astype(x.dtype), sel_new], axis=1)
    win_all = jnp.concatenate([win_past.astype(x.dtype), win_new], axis=1)
    kw_pad = jnp.pad(win_all, ((0, 0), (WINDOW - win_past.shape[1], 0), (0, 0), (0, 0), (0, 0)))
    o_n = nsa_attention(q_n, gate_n, cmp_all, sel_all, kw_pad, past_len, w_cmp)
    win_state = kw_pad[:, kw_pad.shape[1] - win_buf:]
    gm_a, gm_n = jnp.split(jax.nn.sigmoid(gate_m), 2, axis=-1)
    x = x + rmsnorm((gm_a * o_a + gm_n * o_n) @ w_o, g_post1)
    gt, up = jnp.split(rmsnorm(x, g_pre2) @ w_ffn_in, 2, axis=-1)
    x = x + rmsnorm((jax.nn.silu(gt) * up) @ w_ffn_out, g_post2)
    x = x + jax.nn.sigmoid(x @ w_ple_gate) * (ple @ w_ple)
    return x, cmp_new, sel_new, win_state, S_new, conv_new


def setup_inputs(seed: int = 0) -> dict:
    key = jax.random.key(seed)
    ks = jax.random.split(key, 26)
    f32 = jnp.float32
    n_pages = PAST_LEN // PAGE_SIZE
    used = DEC_BATCH * n_pages
    n_pool = used + max(1, used // 4)
    win_buf = min(WINDOW, PAST_LEN)
    kvh = (2, NSA_KV_HEADS, HEAD_DIM)
    nrm = lambda k, shape, s=1.0: jax.random.normal(k, shape, f32) * s
    gain = lambda k, shape: 1.0 + 0.05 * jax.random.normal(k, shape, f32)
    page_table = jax.random.permutation(ks[3], n_pool)[:used].reshape(DEC_BATCH, n_pages).astype(jnp.int32)
    dt = jnp.exp(jax.random.uniform(ks[14], (DEPTH, GDN_HEADS), f32, np.log(0.001), np.log(0.1)))
    return {
        "x_prompt": nrm(ks[0], (BATCH, SEQ, D_MODEL)),
        "x_sample": nrm(ks[1], (DEC_BATCH, DEC_SEQ, D_MODEL)),
        "cache_cmp_kv": nrm(ks[2], (DEPTH, n_pool, PAGE_SIZE) + kvh),
        "cache_sel_kv": nrm(ks[4], (DEPTH, n_pool, PAGE_SIZE) + kvh),
        "page_table": page_table,
        "state_win_kv": nrm(ks[5], (DEPTH, DEC_BATCH, win_buf) + kvh),
        "state_gdn": nrm(ks[6], (DEPTH, DEC_BATCH, GDN_HEADS, HEAD_DIM, HEAD_DIM), 0.1),
        "state_conv": nrm(ks[7], (DEPTH, DEC_BATCH, CONV_WIDTH - 1, 3 * GDN_WIDTH)),
        "p_prompt": nrm(ks[8], (DEPTH, BATCH, SEQ, PLE_DIM)),
        "p_sample": nrm(ks[9], (DEPTH, DEC_BATCH, DEC_SEQ, PLE_DIM)),
        "g_pre1": gain(ks[10], (DEPTH, D_MODEL)),
        "w_in": nrm(ks[11], (DEPTH, D_MODEL, PROJ_WIDTH), D_MODEL ** -0.5),
        "conv_w": nrm(ks[12], (DEPTH, CONV_WIDTH, 3 * GDN_WIDTH), CONV_WIDTH ** -0.5),
        "A_log": jnp.log(jax.random.uniform(ks[13], (DEPTH, GDN_HEADS), f32, 1.0, 16.0)),
        "dt_bias": dt + jnp.log(-jnp.expm1(-dt)),
        "gdn_norm_w": gain(ks[15], (DEPTH, HEAD_DIM)),
        "w_cmp": (1.0 + 0.5 * jax.random.normal(ks[16], (DEPTH, CMP_BLOCK) + kvh, f32)) / CMP_BLOCK,
        "w_o": nrm(ks[17], (DEPTH, D_MODEL, D_MODEL), D_MODEL ** -0.5),
        "g_post1": gain(ks[18], (DEPTH, D_MODEL)),
        "g_pre2": gain(ks[19], (DEPTH, D_MODEL)),
        "w_ffn_in": nrm(ks[20], (DEPTH, D_MODEL, 2 * FFN_HIDDEN), D_MODEL ** -0.5),
        "w_ffn_out": nrm(ks[21], (DEPTH, FFN_HIDDEN, D_MODEL), FFN_HIDDEN ** -0.5),
        "g_post2": gain(ks[22], (DEPTH, D_MODEL)),
        "w_ple": nrm(ks[23], (DEPTH, PLE_DIM, D_MODEL), PLE_DIM ** -0.5),
        "w_ple_gate": nrm(ks[24], (DEPTH, D_MODEL, D_MODEL), D_MODEL ** -0.5),
    }


def reference(x_prompt, x_sample, cache_cmp_kv, cache_sel_kv, page_table, state_win_kv, state_gdn,
              state_conv, p_prompt, p_sample, g_pre1, w_in, conv_w, A_log, dt_bias, gdn_norm_w, w_cmp,
              w_o, g_post1, g_pre2, w_ffn_in, w_ffn_out, g_post2, w_ple, w_ple_gate):
    B = x_prompt.shape[0]
    Bs = x_sample.shape[0]
    n_pages = page_table.shape[1]
    past_len = n_pages * PAGE_SIZE
    win_buf = state_win_kv.shape[2]
    kvh = (2, NSA_KV_HEADS, HEAD_DIM)
    xp, xs = x_prompt, x_sample
    outs_p, outs_s = [], []
    for i in range(DEPTH):
        w = (g_pre1[i], w_in[i], conv_w[i], A_log[i], dt_bias[i], gdn_norm_w[i], w_cmp[i], w_o[i],
             g_post1[i], g_pre2[i], w_ffn_in[i], w_ffn_out[i], g_post2[i], w_ple[i], w_ple_gate[i])
        empty = jnp.zeros((B, 0) + kvh, xp.dtype)
        conv0 = jnp.zeros((B, CONV_WIDTH - 1, 3 * GDN_WIDTH), xp.dtype)
        S0 = jnp.zeros((B, GDN_HEADS, HEAD_DIM, HEAD_DIM), jnp.float32)
        xp, *sp = trunk_layer(xp, p_prompt[i], 0, win_buf, conv0, S0, empty, empty, empty, *w)
        cmp_past = cache_cmp_kv[i][page_table].reshape((Bs, past_len) + kvh)
        sel_past = cache_sel_kv[i][page_table].reshape((Bs, past_len) + kvh)
        xs, *ss = trunk_layer(xs, p_sample[i], past_len, win_buf, state_conv[i], state_gdn[i],
                              cmp_past, sel_past, state_win_kv[i], *w)
        outs_p.append(sp)
        outs_s.append(ss)
    new_cmp_p = jnp.stack([o[0] for o in outs_p])
    new_sel_p = jnp.stack([o[1] for o in outs_p])
    new_win_p = jnp.stack([o[2] for o in outs_p])
    new_gdn_p = jnp.stack([o[3] for o in outs_p])
    new_conv_p = jnp.stack([o[4] for o in outs_p])
    new_cmp_s = jnp.stack([o[0] for o in outs_s])
    new_sel_s = jnp.stack([o[1] for o in outs_s])
    new_win_s = jnp.stack([o[2] for o in outs_s])
    new_gdn_s = jnp.stack([o[3] for o in outs_s])
    new_conv_s = jnp.stack([o[4] for o in outs_s])
    return (xp, xs, new_cmp_p, new_sel_p, new_win_p, new_gdn_p, new_conv_p,
            new_cmp_s, new_sel_s, new_win_s, new_gdn_s, new_conv_s)
```

```python
import functools

import jax
import jax.numpy as jnp
from jax import lax
from jax.experimental import pallas as pl
from jax.experimental.pallas import tpu as pltpu

F32 = jnp.float32
BF16 = jnp.bfloat16
HI = lax.Precision.HIGHEST

D_MODEL = 2048
HEAD_DIM = 128
GDN_HEADS = 16
GDN_WIDTH = 2048
CONV_WIDTH = 4
GDN_CHUNK = 64
NSA_HEADS = 16
NSA_KV_HEADS = 4
NSA_GROUP = 4
NSA_KV_WIDTH = 512
CMP_BLOCK = 32
SEL_BLOCK = 64
SEL_TOPK = 16
WINDOW = 512
PLE_DIM = 256
FFN_HIDDEN = 5632
PAGE_SIZE = 128
RMS_EPS = 1e-6
NEG_INF = -1e30
FORCE_BONUS = float(NSA_GROUP + 1)
QK_SCALE = HEAD_DIM ** -0.5

COL_QKV = 0
COL_ZA = 6144
COL_GMA = 8192
COL_GMN = 10240
COL_QN = 12288
COL_CMP = 14336
COL_SEL = 15360
COL_WIN = 16384
MAIN_WIDTH = 17408
SMALL_WIDTH = 128
SMALL_A, SMALL_B, SMALL_GN = 0, 16, 32

VMEM_LIMIT = 56 * 1024 * 1024


def _cparams(sem):
    return pltpu.CompilerParams(dimension_semantics=sem, vmem_limit_bytes=VMEM_LIMIT)


def _bdot(a, b):
    return jnp.dot(a.astype(BF16), b.astype(BF16), preferred_element_type=F32)


def _bdot_nt(a, b):
    return lax.dot_general(a.astype(BF16), b.astype(BF16), (((1,), (1,)), ((), ())),
                           preferred_element_type=F32)


def _hdot(a, b):
    return jnp.dot(a, b, precision=HI, preferred_element_type=F32)


def _hdot_nt(a, b):
    return lax.dot_general(a, b, (((1,), (1,)), ((), ())), precision=HI, preferred_element_type=F32)


def _sigmoid(x):
    return 1.0 / (1.0 + jnp.exp(-x))


def _silu(x):
    return x * _sigmoid(x)


def _rms_rows(x):
    return lax.rsqrt(jnp.mean(x * x, axis=-1, keepdims=True) + RMS_EPS)


def _inproj_kernel(x_ref, g_ref, w_ref, ws_ref, z_ref, zs_ref, h_scr):
    @pl.when(pl.program_id(1) == 0)
    def _():
        x = x_ref[...]
        h = ((x * _rms_rows(x)) * g_ref[...]).astype(BF16)
        h_scr[...] = h
        zs_ref[...] = jnp.dot(h, ws_ref[...], preferred_element_type=F32)

    z_ref[...] = jnp.dot(h_scr[...], w_ref[...], preferred_element_type=F32)


def _inproj(x2, g_pre1, w_main, w_small, tm, tn=1024):
    n = x2.shape[0]
    return pl.pallas_call(
        _inproj_kernel,
        grid=(n // tm, MAIN_WIDTH // tn),
        in_specs=[
            pl.BlockSpec((tm, D_MODEL), lambda i, j: (i, 0)),
            pl.BlockSpec((1, D_MODEL), lambda i, j: (0, 0)),
            pl.BlockSpec((D_MODEL, tn), lambda i, j: (0, j)),
            pl.BlockSpec((D_MODEL, SMALL_WIDTH), lambda i, j: (0, 0)),
        ],
        out_specs=[
            pl.BlockSpec((tm, tn), lambda i, j: (i, j)),
            pl.BlockSpec((tm, SMALL_WIDTH), lambda i, j: (i, 0)),
        ],
        out_shape=[jax.ShapeDtypeStruct((n, MAIN_WIDTH), F32),
                   jax.ShapeDtypeStruct((n, SMALL_WIDTH), F32)],
        scratch_shapes=[pltpu.VMEM((tm, D_MODEL), BF16)],
        compiler_params=_cparams(("parallel", "arbitrary")),
        name="inproj",
    )(x2, g_pre1.reshape(1, D_MODEL), w_main, w_small)


def _unit_lower_inverse(L, ii, jj, C):
    eye = (ii == jj).astype(F32)
    s = 1
    X = eye - jnp.where((ii // s) % 2 == 1, jnp.where(jj // s == ii // s - 1, L, 0.0), 0.0)
    s = 2
    while s < C:
        cs = jnp.where((ii // s) % 2 == 1, jnp.where(jj // s == ii // s - 1, L, 0.0), 0.0)
        X = X - _hdot(X, _hdot(cs, X))
        s *= 2
    return X


def _gdn_kernel(tb, t_valid,
                q_ref, k_ref, v_ref, qh_ref, kh_ref, vh_ref, cpq_ref, cpk_ref, cpv_ref,
                cwq_ref, cwk_ref, cwv_ref, z_ref, zs_ref, hp_ref, nw_ref, s0_ref,
                o_ref, sn_ref, ext_scr, s_scr):
    C = GDN_CHUNK
    h = pl.program_id(1)
    t = pl.program_id(2)
    nt = pl.num_programs(2)

    @pl.when(t == 0)
    def _():
        s_scr[...] = s0_ref[...]

    def conv_silu(u_ref, halo_ref, cp_ref, cw_ref):
        prev = jnp.where(t == 0, cp_ref[...], halo_ref[...])
        ext_scr[0:8, :] = prev
        ext_scr[8:8 + tb, :] = u_ref[...]
        w = cw_ref[...]
        acc = ext_scr[5:5 + tb, :] * w[0:1, :]
        for j in range(1, CONV_WIDTH):
            acc = acc + ext_scr[5 + j:5 + j + tb, :] * w[j:j + 1, :]
        return _silu(acc)

    q = conv_silu(q_ref, qh_ref, cpq_ref, cwq_ref)
    k = conv_silu(k_ref, kh_ref, cpk_ref, cwk_ref)
    v = conv_silu(v_ref, vh_ref, cpv_ref, cwv_ref)
    q = (q * lax.rsqrt(jnp.sum(q * q, axis=-1, keepdims=True) + RMS_EPS)) * QK_SCALE
    k = k * lax.rsqrt(jnp.sum(k * k, axis=-1, keepdims=True) + RMS_EPS)

    zs = zs_ref[...]
    lane = lax.broadcasted_iota(jnp.int32, (tb, SMALL_WIDTH), 1)
    xa = zs + hp_ref[1:2, :]
    softplus = jnp.maximum(xa, 0.0) + jnp.log1p(jnp.exp(-jnp.abs(xa)))
    g_all = -jnp.exp(hp_ref[0:1, :]) * softplus
    beta_all = _sigmoid(zs)
    g_tok = jnp.sum(jnp.where(lane == SMALL_A + h, g_all, 0.0), axis=1, keepdims=True)
    b_tok = jnp.sum(jnp.where(lane == SMALL_B + h, beta_all, 0.0), axis=1, keepdims=True)
    if t_valid < tb:
        row = lax.broadcasted_iota(jnp.int32, (tb, 1), 0)
        g_tok = jnp.where(row < t_valid, g_tok, 0.0)
        b_tok = jnp.where(row < t_valid, b_tok, 0.0)

    ii = lax.broadcasted_iota(jnp.int32, (C, C), 0)
    jj = lax.broadcasted_iota(jnp.int32, (C, C), 1)
    tril = (ii >= jj).astype(F32)
    nw = nw_ref[...]
    S = s_scr[...]
    for c in range(tb // C):
        sl = slice(c * C, (c + 1) * C)
        qc, kc, vc = q[sl], k[sl], v[sl]
        bcol = jnp.broadcast_to(b_tok[sl], (C, HEAD_DIM))
        gcol = _hdot(tril, jnp.broadcast_to(g_tok[sl], (C, HEAD_DIM)))
        grow = jnp.transpose(gcol)[0:C, :]
        gsq = gcol[:, 0:C]
        decay = jnp.exp(jnp.where(ii >= jj, gsq - grow, NEG_INF))
        kk = _bdot_nt(kc, kc)
        L = jnp.where(ii > jj, kk * decay, 0.0) * bcol[:, 0:C]
        eg = jnp.exp(gcol)
        rhs = jnp.concatenate([vc * bcol, kc * (bcol * eg)], axis=1)
        X = _unit_lower_inverse(L, ii, jj, C)
        sol = _hdot(X, rhs)
        u, w = sol[:, 0:HEAD_DIM], sol[:, HEAD_DIM:]
        v_new = u - _bdot(w, S)
        qk = _bdot_nt(qc, kc) * decay
        o = _bdot(qc * eg, S) + _bdot(qk, v_new)
        g_last = gcol[C - 1:C, :]
        kd = kc * jnp.exp(g_last - gcol)
        S = S * jnp.exp(g_last) + _bdot(jnp.transpose(kd), v_new)
        on = (o * _rms_rows(o)) * nw
        o_ref[sl, :] = on * _silu(z_ref[sl, :])
    s_scr[...] = S

    @pl.when(t == nt - 1)
    def _():
        sn_ref[...] = S


def _gdn(z3, zs3, conv_prev, conv_w, hp, norm_w, s0, tb, t_valid):
    B, T, _ = z3.shape
    nt = T // tb
    hb = tb // 8

    def main(col0):
        return pl.BlockSpec((None, tb, HEAD_DIM), lambda b, h, t: (b, t, col0 + h))

    def halo(col0):
        return pl.BlockSpec((None, 8, HEAD_DIM), lambda b, h, t: (b, jnp.maximum(t * hb - 1, 0), col0 + h))

    def cprev(col0):
        return pl.BlockSpec((None, 8, HEAD_DIM), lambda b, h, t: (b, 0, col0 + h))

    def cw(col0):
        return pl.BlockSpec((CONV_WIDTH, HEAD_DIM), lambda b, h, t: (0, col0 + h))

    cq, ck, cv = COL_QKV // 128, COL_QKV // 128 + 16, COL_QKV // 128 + 32
    return pl.pallas_call(
        functools.partial(_gdn_kernel, tb, t_valid),
        grid=(B, GDN_HEADS, nt),
        in_specs=[
            main(cq), main(ck), main(cv), halo(cq), halo(ck), halo(cv),
            cprev(0), cprev(16), cprev(32), cw(0), cw(16), cw(32),
            main(COL_ZA // 128),
            pl.BlockSpec((None, tb, SMALL_WIDTH), lambda b, h, t: (b, t, 0)),
            pl.BlockSpec((8, SMALL_WIDTH), lambda b, h, t: (0, 0)),
            pl.BlockSpec((1, HEAD_DIM), lambda b, h, t: (0, 0)),
            pl.BlockSpec((None, None, HEAD_DIM, HEAD_DIM), lambda b, h, t: (b, h, 0, 0)),
        ],
        out_specs=[
            pl.BlockSpec((None, tb, HEAD_DIM), lambda b, h, t: (b, t, h)),
            pl.BlockSpec((None, None, HEAD_DIM, HEAD_DIM), lambda b, h, t: (b, h, 0, 0)),
        ],
        out_shape=[jax.ShapeDtypeStruct((B, T, GDN_WIDTH), F32),
                   jax.ShapeDtypeStruct((B, GDN_HEADS, HEAD_DIM, HEAD_DIM), F32)],
        scratch_shapes=[pltpu.VMEM((tb + 8, HEAD_DIM), F32), pltpu.VMEM((HEAD_DIM, HEAD_DIM), F32)],
        compiler_params=_cparams(("parallel", "parallel", "arbitrary")),
        name="gdn",
    )(z3, z3, z3, z3, z3, z3, conv_prev, conv_prev, conv_prev, conv_w, conv_w, conv_w,
      z3, zs3, hp, norm_w.reshape(1, HEAD_DIM), s0)


def _compress_kernel(x_ref, w_ref, o_ref):
    rows = x_ref.shape[0]
    x = x_ref[...].reshape(rows // CMP_BLOCK, CMP_BLOCK, 2 * NSA_KV_WIDTH)
    o_ref[...] = jnp.sum(x * w_ref[...][None], axis=1)


def _compress_prompt(z3, w_cmp2, tc=256):
    B, T, _ = z3.shape
    nc = T // CMP_BLOCK
    return pl.pallas_call(
        _compress_kernel,
        grid=(B, T // tc),
        in_specs=[pl.BlockSpec((None, tc, 2 * NSA_KV_WIDTH), lambda b, i: (b, i, COL_CMP // 1024)),
                  pl.BlockSpec((CMP_BLOCK, 2 * NSA_KV_WIDTH), lambda b, i: (0, 0))],
        out_specs=pl.BlockSpec((None, tc // CMP_BLOCK, 2 * NSA_KV_WIDTH), lambda b, i: (b, i, 0)),
        out_shape=jax.ShapeDtypeStruct((B, nc, 2 * NSA_KV_WIDTH), F32),
        compiler_params=_cparams(("parallel", "parallel")),
        name="compress_prompt",
    )(z3, w_cmp2)


def _topk_mask(sc, k_sel):
    n = sc.shape[1]
    lanef = lax.broadcasted_iota(jnp.int32, sc.shape, 1).astype(F32)
    sel = jnp.zeros(sc.shape, F32)
    for _ in range(k_sel):
        m = jnp.max(sc, axis=1, keepdims=True)
        idx = jnp.min(jnp.where(sc == m, lanef, float(n)), axis=1, keepdims=True)
        hit = lanef == idx
        sel = jnp.where(hit, 1.0, sel)
        sc = jnp.where(hit, -3.0, sc)
    return sel


def _cmp_attn_kernel(ta, nc, ns, ns_real, t_base, slopes_ref, q_ref, kvc_ref, zs_ref, oc_ref, sel_ref, sc_scr):
    t0 = t_base + pl.program_id(1) * ta
    tpos = t0 + lax.broadcasted_iota(jnp.int32, (ta, nc), 0)
    cend = lax.broadcasted_iota(jnp.int32, (ta, nc), 1) * CMP_BLOCK + (CMP_BLOCK - 1)
    dist = tpos - cend
    valid = dist >= 0
    distf = dist.astype(F32)
    gates = _sigmoid(zs_ref[...])
    pc = lax.broadcasted_iota(jnp.int32, (nc, ns), 0)
    ps = lax.broadcasted_iota(jnp.int32, (nc, ns), 1)
    pool = (pc // (SEL_BLOCK // CMP_BLOCK) == ps).astype(F32)
    tq = t0 + lax.broadcasted_iota(jnp.int32, (ta, ns), 0)
    blk = lax.broadcasted_iota(jnp.int32, (ta, ns), 1)
    cur = tq // SEL_BLOCK
    forced = (blk == 0) | (blk == cur) | (blk == cur - 1)
    avail = blk * SEL_BLOCK <= tq
    for g in range(NSA_KV_HEADS):
        kc = kvc_ref[:, g * HEAD_DIM:(g + 1) * HEAD_DIM]
        vc = kvc_ref[:, NSA_KV_WIDTH + g * HEAD_DIM:NSA_KV_WIDTH + (g + 1) * HEAD_DIM]
        imp = jnp.zeros((ta, nc), F32)
        for r in range(NSA_GROUP):
            hd = g * NSA_GROUP + r
            qh = q_ref[:, hd * HEAD_DIM:(hd + 1) * HEAD_DIM] * QK_SCALE
            s = _hdot_nt(qh, kc) - slopes_ref[hd] * distf
            s = jnp.where(valid, s, NEG_INF)
            e = jnp.exp(s - jnp.max(s, axis=1, keepdims=True))
            p = jnp.where(valid, e / jnp.sum(e, axis=1, keepdims=True), 0.0)
            imp = imp + p
            gi = SMALL_GN + hd * 3
            oc_ref[:, hd * HEAD_DIM:(hd + 1) * HEAD_DIM] = gates[:, gi:gi + 1] * _bdot(p, vc)
        imps = _hdot(imp, pool)
        score = jnp.where(forced, imps + FORCE_BONUS, jnp.where(avail, imps, -1.0))
        if ns_real < ns:
            score = jnp.where(blk < ns_real, score, -2.0)
        sc_scr[g * ta:(g + 1) * ta, :] = score
    sel = _topk_mask(sc_scr[...], min(SEL_TOPK, ns_real))
    for g in range(NSA_KV_HEADS):
        sel_ref[g] = sel[g * ta:(g + 1) * ta, :]


def _cmp_attn(z3, zs3, kvc, slopes, ta, n_tok, ns, ns_real, t_base, name):
    B = z3.shape[0]
    nc = kvc.shape[1]
    return pl.pallas_call(
        functools.partial(_cmp_attn_kernel, ta, nc, ns, ns_real, t_base),
        grid_spec=pltpu.PrefetchScalarGridSpec(
            num_scalar_prefetch=1,
            grid=(B, n_tok // ta),
            in_specs=[
                pl.BlockSpec((None, ta, NSA_HEADS * HEAD_DIM), lambda b, i, sl: (b, i, COL_QN // 2048)),
                pl.BlockSpec((None, nc, 2 * NSA_KV_WIDTH), lambda b, i, sl: (b, 0, 0)),
                pl.BlockSpec((None, ta, SMALL_WIDTH), lambda b, i, sl: (b, i, 0)),
            ],
            out_specs=[
                pl.BlockSpec((None, ta, NSA_HEADS * HEAD_DIM), lambda b, i, sl: (b, i, 0)),
                pl.BlockSpec((None, NSA_KV_HEADS, ta, ns), lambda b, i, sl: (b, 0, i, 0)),
            ],
            scratch_shapes=[pltpu.VMEM((NSA_KV_HEADS * ta, ns), F32)],
        ),
        out_shape=[jax.ShapeDtypeStruct((B, n_tok, NSA_HEADS * HEAD_DIM), F32),
                   jax.ShapeDtypeStruct((B, NSA_KV_HEADS, n_tok, ns), F32)],
        compiler_params=_cparams(("parallel", "parallel")),
        name=name,
    )(slopes, z3, kvc, zs3)


def _sel_win_kernel(T, ns, tkv, wspan, slopes_ref, q_ref, ks_ref, vs_ref, kw_ref, vw_ref, sel_ref,
                    zs_ref, oc_ref, o_ref):
    QB = SEL_BLOCK
    R = NSA_GROUP
    rows = R * QB
    g = pl.program_id(1)
    qb = pl.program_id(2)
    t0 = qb * QB
    q = jnp.concatenate([q_ref[:, r * HEAD_DIM:(r + 1) * HEAD_DIM] for r in range(R)], axis=0)
    q = (q * QK_SCALE).astype(BF16)
    rr = lax.broadcasted_iota(jnp.int32, (rows, 1), 0) // QB
    slope = jnp.zeros((rows, 1), F32)
    for r in range(R):
        slope = jnp.where(rr == r, slopes_ref[g * R + r], slope)
    sel4 = jnp.concatenate([sel_ref[...]] * R, axis=0).astype(BF16)

    ti = lax.broadcasted_iota(jnp.int32, (rows, tkv), 0) % QB
    kj = lax.broadcasted_iota(jnp.int32, (rows, tkv), 1)
    d0 = (ti - kj).astype(F32)
    eb = lax.broadcasted_iota(jnp.int32, (ns, tkv), 0)
    ek = lax.broadcasted_iota(jnp.int32, (ns, tkv), 1) // SEL_BLOCK

    def body(j, carry):
        m, l, acc = carry
        k0 = pl.multiple_of(j * tkv, tkv)
        kt = ks_ref[pl.ds(k0, tkv), :]
        vt = vs_ref[pl.ds(k0, tkv), :]
        expand = (eb == ek + j * (tkv // SEL_BLOCK)).astype(BF16)
        keymask = jnp.dot(sel4, expand, preferred_element_type=F32)
        distf = d0 + (t0 - k0).astype(F32)
        valid = (distf >= 0.0) & (keymask > 0.5)
        s = _bdot_nt(q, kt) - slope * distf
        s = jnp.where(valid, s, NEG_INF)
        m_new = jnp.maximum(m, jnp.max(s, axis=1, keepdims=True))
        p = jnp.where(valid, jnp.exp(s - m_new), 0.0)
        alpha = jnp.exp(m - m_new)
        l = alpha * l + jnp.sum(p, axis=1, keepdims=True)
        acc = alpha * acc + _bdot(p, vt)
        return m_new, l, acc

    n_tiles = (t0 + QB + tkv - 1) // tkv
    init = (jnp.full((rows, 1), NEG_INF, F32), jnp.zeros((rows, 1), F32), jnp.zeros((rows, HEAD_DIM), F32))
    _, l_s, acc_s = lax.fori_loop(0, n_tiles, body, init)
    o_s = acc_s / l_s

    kstart = pl.multiple_of(jnp.clip(t0 - WINDOW, 0, T - wspan), SEL_BLOCK)
    kw = kw_ref[pl.ds(kstart, wspan), :]
    vw = vw_ref[pl.ds(kstart, wspan), :]
    tw = lax.broadcasted_iota(jnp.int32, (rows, wspan), 0) % QB
    kwj = lax.broadcasted_iota(jnp.int32, (rows, wspan), 1)
    dist = (tw - kwj) + (t0 - kstart)
    valid = (dist >= 0) & (dist < WINDOW)
    s = _bdot_nt(q, kw) - slope * dist.astype(F32)
    s = jnp.where(valid, s, NEG_INF)
    e = jnp.exp(s - jnp.max(s, axis=1, keepdims=True))
    p = jnp.where(valid, e / jnp.sum(e, axis=1, keepdims=True), 0.0)
    o_w = _bdot(p, vw)

    gates = _sigmoid(zs_ref[...])
    lane = lax.broadcasted_iota(jnp.int32, (QB, SMALL_WIDTH), 1)
    for r in range(R):
        gi = SMALL_GN + (g * R + r) * 3
        g_s = jnp.sum(jnp.where(lane == gi + 1, gates, 0.0), axis=1, keepdims=True)
        g_w = jnp.sum(jnp.where(lane == gi + 2, gates, 0.0), axis=1, keepdims=True)
        rs = slice(r * QB, (r + 1) * QB)
        cs = slice(r * HEAD_DIM, (r + 1) * HEAD_DIM)
        o_ref[:, cs] = oc_ref[:, cs] + g_s * o_s[rs] + g_w * o_w[rs]


def _sel_win_prompt(z3, zs3, selmask, oc, slopes):
    B, T, _ = z3.shape
    ns = T // SEL_BLOCK
    tkv = min(512, T)
    wspan = min(WINDOW + 2 * SEL_BLOCK, T)
    gw = NSA_GROUP * HEAD_DIM

    def kv(col0):
        return pl.BlockSpec((None, T, HEAD_DIM), lambda b, g, i, sl: (b, 0, col0 // 128 + g))

    return pl.pallas_call(
        functools.partial(_sel_win_kernel, T, ns, tkv, wspan),
        grid_spec=pltpu.PrefetchScalarGridSpec(
            num_scalar_prefetch=1,
            grid=(B, NSA_KV_HEADS, T // SEL_BLOCK),
            in_specs=[
                pl.BlockSpec((None, SEL_BLOCK, gw), lambda b, g, i, sl: (b, i, COL_QN // gw + g)),
                kv(COL_SEL), kv(COL_SEL + NSA_KV_WIDTH), kv(COL_WIN), kv(COL_WIN + NSA_KV_WIDTH),
                pl.BlockSpec((None, None, SEL_BLOCK, ns), lambda b, g, i, sl: (b, g, i, 0)),
                pl.BlockSpec((None, SEL_BLOCK, SMALL_WIDTH), lambda b, g, i, sl: (b, i, 0)),
                pl.BlockSpec((None, SEL_BLOCK, gw), lambda b, g, i, sl: (b, i, g)),
            ],
            out_specs=pl.BlockSpec((None, SEL_BLOCK, gw), lambda b, g, i, sl: (b, i, g)),
        ),
        out_shape=jax.ShapeDtypeStruct((B, T, NSA_HEADS * HEAD_DIM), F32),
        compiler_params=_cparams(("parallel", "parallel", "arbitrary")),
        name="sel_win_prompt",
    )(slopes, z3, z3, z3, z3, z3, selmask, zs3, oc)


def _merge_kernel(gma_ref, gmn_ref, oa_ref, on_ref, x_ref, w_ref, g_ref, o_ref):
    mixed = _sigmoid(gma_ref[...]) * oa_ref[...] + _sigmoid(gmn_ref[...]) * on_ref[...]
    y = jnp.dot(mixed.astype(BF16), w_ref[...], preferred_element_type=F32)
    o_ref[...] = x_ref[...] + (y * _rms_rows(y)) * g_ref[...]


def _merge_out(z2, o_a, o_n, x2, w_o, g_post1, tm):
    n = x2.shape[0]
    row = lambda c: pl.BlockSpec((tm, D_MODEL), lambda i: (i, c))
    return pl.pallas_call(
        _merge_kernel,
        grid=(n // tm,),
        in_specs=[row(COL_GMA // D_MODEL), row(COL_GMN // D_MODEL), row(0), row(0), row(0),
                  pl.BlockSpec((D_MODEL, D_MODEL), lambda i: (0, 0)),
                  pl.BlockSpec((1, D_MODEL), lambda i: (0, 0))],
        out_specs=row(0),
        out_shape=jax.ShapeDtypeStruct((n, D_MODEL), F32),
        compiler_params=_cparams(("parallel",)),
        name="merge_out",
    )(z2, z2, o_a, o_n, x2, w_o, g_post1.reshape(1, D_MODEL))


def _ffn_kernel(x_ref, g2_ref, wg_ref, wu_ref, wo_ref, gp_ref, o_ref, h_scr, acc_scr):
    j = pl.program_id(1)

    @pl.when(j == 0)
    def _():
        x = x_ref[...]
        h_scr[...] = ((x * _rms_rows(x)) * g2_ref[...]).astype(BF16)
        acc_scr[...] = jnp.zeros_like(acc_scr)

    h = h_scr[...]
    gt = jnp.dot(h, wg_ref[...], preferred_element_type=F32)
    up = jnp.dot(h, wu_ref[...], preferred_element_type=F32)
    acc_scr[...] += jnp.dot((_silu(gt) * up).astype(BF16), wo_ref[...], preferred_element_type=F32)

    @pl.when(j == pl.num_programs(1) - 1)
    def _():
        y = acc_scr[...]
        o_ref[...] = x_ref[...] + (y * _rms_rows(y)) * gp_ref[...]


def _ffn(x2, g_pre2, w_ffn_in, w_ffn_out, g_post2, tm, th=512):
    n = x2.shape[0]
    nh = FFN_HIDDEN // th
    return pl.pallas_call(
        _ffn_kernel,
        grid=(n // tm, nh),
        in_specs=[
            pl.BlockSpec((tm, D_MODEL), lambda i, j: (i, 0)),
            pl.BlockSpec((1, D_MODEL), lambda i, j: (0, 0)),
            pl.BlockSpec((D_MODEL, th), lambda i, j: (0, j)),
            pl.BlockSpec((D_MODEL, th), lambda i, j: (0, nh + j)),
            pl.BlockSpec((th, D_MODEL), lambda i, j: (j, 0)),
            pl.BlockSpec((1, D_MODEL), lambda i, j: (0, 0)),
        ],
        out_specs=pl.BlockSpec((tm, D_MODEL), lambda i, j: (i, 0)),
        out_shape=jax.ShapeDtypeStruct((n, D_MODEL), F32),
        scratch_shapes=[pltpu.VMEM((tm, D_MODEL), BF16), pltpu.VMEM((tm, D_MODEL), F32)],
        compiler_params=_cparams(("parallel", "arbitrary")),
        name="ffn",
    )(x2, g_pre2.reshape(1, D_MODEL), w_ffn_in, w_ffn_in, w_ffn_out, g_post2.reshape(1, D_MODEL))


def _ple_kernel(x_ref, p_ref, wg_ref, wp_ref, o_ref):
    x = x_ref[...]
    gate = _sigmoid(jnp.dot(x.astype(BF16), wg_ref[...], preferred_element_type=F32))
    o_ref[...] = x + gate * jnp.dot(p_ref[...].astype(BF16), wp_ref[...], preferred_element_type=F32)


def _ple(x2, ple2, w_ple_gate, w_ple, tm):
    n = x2.shape[0]
    return pl.pallas_call(
        _ple_kernel,
        grid=(n // tm,),
        in_specs=[pl.BlockSpec((tm, D_MODEL), lambda i: (i, 0)),
                  pl.BlockSpec((tm, PLE_DIM), lambda i: (i, 0)),
                  pl.BlockSpec((D_MODEL, D_MODEL), lambda i: (0, 0)),
                  pl.BlockSpec((PLE_DIM, D_MODEL), lambda i: (0, 0))],
        out_specs=pl.BlockSpec((tm, D_MODEL), lambda i: (i, 0)),
        out_shape=jax.ShapeDtypeStruct((n, D_MODEL), F32),
        compiler_params=_cparams(("parallel",)),
        name="ple",
    )(x2, ple2, w_ple_gate, w_ple)


def _compress_pages_kernel(npg, pt_ref, *refs):
    w = refs[npg][...]
    o_ref = refs[npg + 1]
    outs = []
    for p in range(npg):
        x = refs[p][...].reshape(PAGE_SIZE // CMP_BLOCK, CMP_BLOCK, 2 * NSA_KV_WIDTH)
        outs.append(jnp.sum(x * w[None], axis=1))
    o_ref[...] = jnp.concatenate(outs, axis=0)


def _compress_sample(cache3, page_table, w_cmp2, npg=4):
    Bs, n_pages = page_table.shape
    per = PAGE_SIZE // CMP_BLOCK

    def page(p):
        return pl.BlockSpec((None, PAGE_SIZE, 2 * NSA_KV_WIDTH), lambda b, i, pt: (pt[b, i * npg + p], 0, 0))

    return pl.pallas_call(
        functools.partial(_compress_pages_kernel, npg),
        grid_spec=pltpu.PrefetchScalarGridSpec(
            num_scalar_prefetch=1,
            grid=(Bs, n_pages // npg),
            in_specs=[page(p) for p in range(npg)]
            + [pl.BlockSpec((CMP_BLOCK, 2 * NSA_KV_WIDTH), lambda b, i, pt: (0, 0))],
            out_specs=pl.BlockSpec((None, npg * per, 2 * NSA_KV_WIDTH), lambda b, i, pt: (b, i, 0)),
        ),
        out_shape=jax.ShapeDtypeStruct((Bs, n_pages * per, 2 * NSA_KV_WIDTH), F32),
        compiler_params=_cparams(("parallel", "parallel")),
        name="compress_sample",
    )(page_table, *([cache3] * npg), w_cmp2)


SAMPLE_ROWS = 8
KV_ROW = 2 * NSA_KV_HEADS


def _sel_pages_kernel(npg, past_len, pt_ref, slopes_ref, q_ref, sel_ref, *refs):
    pages = refs[:npg]
    m_ref, l_ref, acc_ref, q_scr, selrows_scr, slope_scr = refs[npg:]
    i = pl.program_id(1)
    nrow = NSA_HEADS * SAMPLE_ROWS
    ncol = PAGE_SIZE * KV_ROW
    ns_pad = sel_ref.shape[-1]

    @pl.when(i == 0)
    def _():
        for hd in range(NSA_HEADS):
            rs = slice(hd * SAMPLE_ROWS, (hd + 1) * SAMPLE_ROWS)
            q_scr[rs, :] = q_ref[:, hd * HEAD_DIM:(hd + 1) * HEAD_DIM] * QK_SCALE
            selrows_scr[rs, :] = sel_ref[hd // NSA_GROUP]
            slope_scr[rs, :] = jnp.full((SAMPLE_ROWS, HEAD_DIM), slopes_ref[hd], F32)
        m_ref[...] = jnp.full(m_ref.shape, NEG_INF, F32)
        l_ref[...] = jnp.zeros(l_ref.shape, F32)
        acc_ref[...] = jnp.zeros(acc_ref.shape, F32)

    qb = q_scr[...].astype(BF16)
    selb = selrows_scr[...].astype(BF16)
    slope = slope_scr[:, 0:1]
    row = lax.broadcasted_iota(jnp.int32, (nrow, ncol), 0)
    col = lax.broadcasted_iota(jnp.int32, (nrow, ncol), 1)
    cpos = col // KV_ROW
    own_key = (col % KV_ROW) == row // (NSA_GROUP * SAMPLE_ROWS)
    tpos = past_len + row % SAMPLE_ROWS
    ob = lax.broadcasted_iota(jnp.int32, (ns_pad, HEAD_DIM), 0)
    ol = lax.broadcasted_iota(jnp.int32, (ns_pad, HEAD_DIM), 1) // SEL_BLOCK
    for p in range(npg):
        pg = i * npg + p
        page = pages[p][...].astype(BF16)
        flags = jnp.dot(selb, (ob == ol + pg * (PAGE_SIZE // SEL_BLOCK)).astype(BF16),
                        preferred_element_type=F32)
        picked = jnp.where(cpos < SEL_BLOCK, flags[:, 0:1], flags[:, SEL_BLOCK:SEL_BLOCK + 1]) > 0.5
        dist = tpos - (pg * PAGE_SIZE + cpos)
        valid = own_key & picked & (dist >= 0)
        s = _bdot_nt(qb, page) - slope * dist.astype(F32)
        s = jnp.where(valid, s, NEG_INF)
        m_old = m_ref[:, 0:1]
        m_new = jnp.maximum(m_old, jnp.max(s, axis=1, keepdims=True))
        pr = jnp.where(valid, jnp.exp(s - m_new), 0.0)
        alpha = jnp.exp(m_old - m_new)
        l_ref[...] = alpha * l_ref[...] + jnp.sum(pr, axis=1, keepdims=True)
        pv = pltpu.roll(pr, NSA_KV_HEADS, axis=1)
        acc_ref[...] = alpha * acc_ref[...] + jnp.dot(pv.astype(BF16), page, preferred_element_type=F32)
        m_ref[...] = jnp.broadcast_to(m_new, m_ref.shape)


def _sel_sample(z3s, selmask, cache_rows, page_table, slopes, past_len, npg=4):
    Bs, n_pages = page_table.shape
    nrow = NSA_HEADS * SAMPLE_ROWS
    ns_pad = selmask.shape[-1]

    def page(p):
        return pl.BlockSpec((None, PAGE_SIZE * KV_ROW, HEAD_DIM), lambda b, i, pt, sl: (pt[b, i * npg + p], 0, 0))

    part = pl.BlockSpec((None, nrow, HEAD_DIM), lambda b, i, pt, sl: (b, 0, 0))
    return pl.pallas_call(
        functools.partial(_sel_pages_kernel, npg, past_len),
        grid_spec=pltpu.PrefetchScalarGridSpec(
            num_scalar_prefetch=2,
            grid=(Bs, n_pages // npg),
            in_specs=[
                pl.BlockSpec((None, SAMPLE_ROWS, NSA_HEADS * HEAD_DIM), lambda b, i, pt, sl: (b, 0, COL_QN // 2048)),
                pl.BlockSpec((None, NSA_KV_HEADS, SAMPLE_ROWS, ns_pad), lambda b, i, pt, sl: (b, 0, 0, 0)),
            ] + [page(p) for p in range(npg)],
            out_specs=[part, part, part],
            scratch_shapes=[pltpu.VMEM((nrow, HEAD_DIM), F32), pltpu.VMEM((nrow, ns_pad), F32),
                            pltpu.VMEM((nrow, HEAD_DIM), F32)],
        ),
        out_shape=[jax.ShapeDtypeStruct((Bs, nrow, HEAD_DIM), F32)] * 3,
        compiler_params=_cparams(("parallel", "arbitrary")),
        name="sel_sample",
    )(page_table, slopes, z3s, selmask, *([cache_rows] * npg))


def _finish_sample_kernel(past_len, t_real, nnew, slopes_ref, q_ref, kst_ref, snew_ref, wnew_ref, sel_ref,
                          m_ref, l_ref, acc_ref, oc_ref, zs_ref, o_ref):
    nst = kst_ref.shape[0]
    cur = past_len // SEL_BLOCK
    gates = _sigmoid(zs_ref[...])
    t_new = lax.broadcasted_iota(jnp.int32, (SAMPLE_ROWS, nnew), 0)
    j_new = lax.broadcasted_iota(jnp.int32, (SAMPLE_ROWS, nnew), 1)
    dist_new = t_new - j_new
    ok_new = (dist_new >= 0) & (j_new < t_real)
    t_st = lax.broadcasted_iota(jnp.int32, (SAMPLE_ROWS, nst), 0)
    i_st = lax.broadcasted_iota(jnp.int32, (SAMPLE_ROWS, nst), 1)
    dist_st = t_st + nst - i_st
    ok_st = dist_st < WINDOW
    for hd in range(NSA_HEADS):
        g = hd // NSA_GROUP
        rs = slice(hd * SAMPLE_ROWS, (hd + 1) * SAMPLE_ROWS)
        kc = slice(g * HEAD_DIM, (g + 1) * HEAD_DIM)
        vc = slice(NSA_KV_WIDTH + g * HEAD_DIM, NSA_KV_WIDTH + (g + 1) * HEAD_DIM)
        sl = slopes_ref[hd]
        qh = (q_ref[:, hd * HEAD_DIM:(hd + 1) * HEAD_DIM] * QK_SCALE).astype(BF16)
        valid = ok_new & (sel_ref[g][:, cur:cur + 1] > 0.5)
        s = jnp.where(valid, _bdot_nt(qh, snew_ref[:, kc]) - sl * dist_new.astype(F32), NEG_INF)
        m_old = m_ref[rs, 0:1]
        m_new = jnp.maximum(m_old, jnp.max(s, axis=1, keepdims=True))
        pr = jnp.where(valid, jnp.exp(s - m_new), 0.0)
        alpha = jnp.exp(m_old - m_new)
        l = alpha * l_ref[rs, 0:1] + jnp.sum(pr, axis=1, keepdims=True)
        o_s = (alpha * acc_ref[rs, :] + _bdot(pr, snew_ref[:, vc])) / l
        s1 = jnp.where(ok_st, _bdot_nt(qh, kst_ref[:, kc]) - sl * dist_st.astype(F32), NEG_INF)
        s2 = jnp.where(ok_new, _bdot_nt(qh, wnew_ref[:, kc]) - sl * dist_new.astype(F32), NEG_INF)
        mw = jnp.maximum(jnp.max(s1, axis=1, keepdims=True), jnp.max(s2, axis=1, keepdims=True))
        e1 = jnp.where(ok_st, jnp.exp(s1 - mw), 0.0)
        e2 = jnp.where(ok_new, jnp.exp(s2 - mw), 0.0)
        den = jnp.sum(e1, axis=1, keepdims=True) + jnp.sum(e2, axis=1, keepdims=True)
        o_w = _bdot(e1 / den, kst_ref[:, vc]) + _bdot(e2 / den, wnew_ref[:, vc])
        gi = SMALL_GN + hd * 3
        cs = slice(hd * HEAD_DIM, (hd + 1) * HEAD_DIM)
        o_ref[:, cs] = oc_ref[:, cs] + gates[:, gi + 1:gi + 2] * o_s + gates[:, gi + 2:gi + 3] * o_w


def _finish_sample(z3s, zs3s, state_win2, selmask, m, l, acc, oc, slopes, past_len, t_real):
    Bs, tz, _ = z3s.shape
    nst = state_win2.shape[1]
    nrow = NSA_HEADS * SAMPLE_ROWS
    ns_pad = selmask.shape[-1]
    kvw = 2 * NSA_KV_WIDTH
    part = pl.BlockSpec((None, nrow, HEAD_DIM), lambda b, sl: (b, 0, 0))
    wide = pl.BlockSpec((None, SAMPLE_ROWS, NSA_HEADS * HEAD_DIM), lambda b, sl: (b, 0, 0))
    return pl.pallas_call(
        functools.partial(_finish_sample_kernel, past_len, t_real, tz),
        grid_spec=pltpu.PrefetchScalarGridSpec(
            num_scalar_prefetch=1,
            grid=(Bs,),
            in_specs=[
                pl.BlockSpec((None, SAMPLE_ROWS, NSA_HEADS * HEAD_DIM), lambda b, sl: (b, 0, COL_QN // 2048)),
                pl.BlockSpec((None, nst, kvw), lambda b, sl: (b, 0, 0)),
                pl.BlockSpec((None, tz, kvw), lambda b, sl: (b, 0, COL_SEL // kvw)),
                pl.BlockSpec((None, tz, kvw), lambda b, sl: (b, 0, COL_WIN // kvw)),
                pl.BlockSpec((None, NSA_KV_HEADS, SAMPLE_ROWS, ns_pad), lambda b, sl: (b, 0, 0, 0)),
                part, part, part, wide,
                pl.BlockSpec((None, SAMPLE_ROWS, SMALL_WIDTH), lambda b, sl: (b, 0, 0)),
            ],
            out_specs=wide,
        ),
        out_shape=jax.ShapeDtypeStruct((Bs, SAMPLE_ROWS, NSA_HEADS * HEAD_DIM), F32),
        compiler_params=_cparams(("parallel",)),
        name="finish_sample",
    )(slopes, z3s, state_win2, z3s, z3s, selmask, m, l, acc, oc, zs3s)


def _mix_and_ffn(z3, o_a, o_n, x3, ple3, wts, tm):
    B, T, _ = x3.shape
    n = B * T
    x1 = _merge_out(z3.reshape(n, MAIN_WIDTH), o_a.reshape(n, GDN_WIDTH), o_n.reshape(n, D_MODEL),
                    x3.reshape(n, D_MODEL), wts["w_o"], wts["g_post1"], min(tm, 256))
    x2 = _ffn(x1, wts["g_pre2"], wts["w_ffn_in"], wts["w_ffn_out"], wts["g_post2"], tm)
    x3o = _ple(x2, ple3.reshape(n, PLE_DIM), wts["w_ple_gate"], wts["w_ple"], min(tm, 256))
    return x3o.reshape(B, T, D_MODEL)


def kernel(x_prompt, x_sample, cache_cmp_kv, cache_sel_kv, page_table, state_win_kv, state_gdn, state_conv, p_prompt, p_sample, g_pre1, w_in, conv_w, A_log, dt_bias, gdn_norm_w, w_cmp, w_o, g_post1, g_pre2, w_ffn_in, w_ffn_out, g_post2, w_ple, w_ple_gate):
    B, T, _ = x_prompt.shape
    Bs, Ts, _ = x_sample.shape
    n_pages = page_table.shape[1]
    past_len = n_pages * PAGE_SIZE
    win_buf = state_win_kv.shape[2]
    kvh = (2, NSA_KV_HEADS, HEAD_DIM)
    qkv_w = 3 * GDN_WIDTH

    wi = w_in[0]
    w_main = jnp.concatenate([wi[:, 0:8192], wi[:, 13392:17488], wi[:, 8224:13344]], axis=1).astype(BF16)
    w_small = jnp.concatenate([wi[:, 8192:8224], wi[:, 13344:13392],
                               jnp.zeros((D_MODEL, SMALL_WIDTH - 80), F32)], axis=1).astype(BF16)
    wts = dict(w_o=w_o[0].astype(BF16), g_post1=g_post1[0], g_pre2=g_pre2[0],
               w_ffn_in=w_ffn_in[0].astype(BF16), w_ffn_out=w_ffn_out[0].astype(BF16), g_post2=g_post2[0],
               w_ple=w_ple[0].astype(BF16), w_ple_gate=w_ple_gate[0].astype(BF16))
    hp = jnp.zeros((8, SMALL_WIDTH), F32).at[0, 0:GDN_HEADS].set(A_log[0]).at[1, 0:GDN_HEADS].set(dt_bias[0])
    w_cmp2 = w_cmp[0].reshape(CMP_BLOCK, 2 * NSA_KV_WIDTH)
    heads = jnp.arange(1, NSA_HEADS + 1, dtype=F32)
    slopes = jnp.exp2(-8.0 * heads / NSA_HEADS)

    z2, zs2 = _inproj(x_prompt.reshape(B * T, D_MODEL), g_pre1[0], w_main, w_small, 512)
    z3, zs3 = z2.reshape(B, T, MAIN_WIDTH), zs2.reshape(B, T, SMALL_WIDTH)
    o_a, s_new_p = _gdn(z3, zs3, jnp.zeros((B, 8, qkv_w), F32), conv_w[0], hp, gdn_norm_w[0],
                        jnp.zeros((B, GDN_HEADS, HEAD_DIM, HEAD_DIM), F32), 256, 256)
    kvc = _compress_prompt(z3, w_cmp2)
    ns = T // SEL_BLOCK
    oc, selmask = _cmp_attn(z3, zs3, kvc, slopes, 512, T, ns, ns, 0, "cmp_attn_prompt")
    o_n = _sel_win_prompt(z3, zs3, selmask, oc, slopes)
    y_prompt = _mix_and_ffn(z3, o_a, o_n, x_prompt, p_prompt[0], wts, 512)

    tz = GDN_CHUNK
    xs = jnp.pad(x_sample, ((0, 0), (0, tz - Ts), (0, 0)))
    zs2_, zss2 = _inproj(xs.reshape(Bs * tz, D_MODEL), g_pre1[0], w_main, w_small, Bs * tz)
    z3s, zs3s = zs2_.reshape(Bs, tz, MAIN_WIDTH), zss2.reshape(Bs, tz, SMALL_WIDTH)
    conv_prev = jnp.pad(state_conv[0], ((0, 0), (8 - (CONV_WIDTH - 1), 0), (0, 0)))
    o_a_s, s_new_s = _gdn(z3s, zs3s, conv_prev, conv_w[0], hp, gdn_norm_w[0], state_gdn[0], tz, Ts)
    n_pool = cache_cmp_kv.shape[1]
    kvc_s = _compress_sample(cache_cmp_kv[0].reshape(n_pool, PAGE_SIZE, 2 * NSA_KV_WIDTH), page_table, w_cmp2)
    ns_real = -(-(past_len + Ts) // SEL_BLOCK)
    ns_pad = -(-ns_real // 128) * 128
    oc_s, selmask_s = _cmp_attn(z3s, zs3s, kvc_s, slopes, SAMPLE_ROWS, SAMPLE_ROWS, ns_pad, ns_real, past_len,
                                "cmp_attn_sample")
    m_s, l_s, acc_s = _sel_sample(z3s, selmask_s, cache_sel_kv[0].reshape(n_pool, PAGE_SIZE * KV_ROW, HEAD_DIM),
                                  page_table, slopes, past_len)
    state_win2 = state_win_kv[0].reshape(Bs, win_buf, 2 * NSA_KV_WIDTH)
    o_n_s = _finish_sample(z3s, zs3s, state_win2, selmask_s, m_s, l_s, acc_s, oc_s, slopes, past_len, Ts)
    o_n_s = jnp.pad(o_n_s, ((0, 0), (0, tz - SAMPLE_ROWS), (0, 0)))
    ps = jnp.pad(p_sample[0], ((0, 0), (0, tz - Ts), (0, 0)))
    y_sample = _mix_and_ffn(z3s, o_a_s, o_n_s, xs, ps, wts, Bs * tz)[:, :Ts]

    def kv_rows(z, col, lo, hi):
        return z[:, lo:hi, col:col + 2 * NSA_KV_WIDTH].reshape((z.shape[0], hi - lo) + kvh)

    new_win_s = jnp.concatenate([state_win_kv[0][:, Ts:], kv_rows(z3s, COL_WIN, 0, Ts)], axis=1)
    return (y_prompt, y_sample,
            kv_rows(z3, COL_CMP, 0, T)[None], kv_rows(z3, COL_SEL, 0, T)[None],
            kv_rows(z3, COL_WIN, T - win_buf, T)[None], s_new_p[None],
            z3[:, T - (CONV_WIDTH - 1):, 0:qkv_w][None],
            kv_rows(z3s, COL_CMP, 0, Ts)[None], kv_rows(z3s, COL_SEL, 0, Ts)[None],
            new_win_s[None], s_new_s[None],
            z3s[:, Ts - (CONV_WIDTH - 1):Ts, 0:qkv_w][None])
```

```python
import functools

import jax
import jax.numpy as jnp
from jax import lax
from jax.experimental import pallas as pl
from jax.experimental.pallas import tpu as pltpu

F32 = jnp.float32
BF16 = jnp.bfloat16
HI = lax.Precision.HIGHEST

D_MODEL = 2048
HEAD_DIM = 128
GDN_HEADS = 16
GDN_WIDTH = 2048
CONV_WIDTH = 4
GDN_CHUNK = 64
NSA_HEADS = 16
NSA_KV_HEADS = 4
NSA_GROUP = 4
NSA_KV_WIDTH = 512
CMP_BLOCK = 32
SEL_BLOCK = 64
SEL_TOPK = 16
WINDOW = 512
PLE_DIM = 256
FFN_HIDDEN = 5632
PAGE_SIZE = 128
RMS_EPS = 1e-6
NEG_INF = -1e30
FORCE_BONUS = float(NSA_GROUP + 1)
QK_SCALE = HEAD_DIM ** -0.5
LOG2E = 1.4426950408889634
MASK_DIST = 1e32

COL_QKV = 0
COL_ZA = 6144
COL_GMA = 8192
COL_GMN = 10240
COL_QN = 12288
COL_CMP = 14336
COL_SEL = 15360
COL_WIN = 16384
MAIN_WIDTH = 17408
SMALL_WIDTH = 128
SMALL_A, SMALL_B, SMALL_GN = 0, 16, 32

VMEM_LIMIT = 56 * 1024 * 1024


def _cparams(sem):
    return pltpu.CompilerParams(dimension_semantics=sem, vmem_limit_bytes=VMEM_LIMIT)


def _bdot(a, b):
    return jnp.dot(a.astype(BF16), b.astype(BF16), preferred_element_type=F32)


def _bdot_nt(a, b):
    return lax.dot_general(a.astype(BF16), b.astype(BF16), (((1,), (1,)), ((), ())),
                           preferred_element_type=F32)


def _hdot(a, b):
    return jnp.dot(a, b, precision=HI, preferred_element_type=F32)


def _hdot_nt(a, b):
    return lax.dot_general(a, b, (((1,), (1,)), ((), ())), precision=HI, preferred_element_type=F32)


def _sigmoid(x):
    return 1.0 / (1.0 + jnp.exp(-x))


def _silu(x):
    return x * _sigmoid(x)


def _rms_rows(x):
    return lax.rsqrt(jnp.mean(x * x, axis=-1, keepdims=True) + RMS_EPS)


def _inproj_kernel(x_ref, g_ref, w_ref, ws_ref, z_ref, zs_ref, h_scr):
    @pl.when(pl.program_id(1) == 0)
    def _():
        x = x_ref[...]
        h = ((x * _rms_rows(x)) * g_ref[...]).astype(BF16)
        h_scr[...] = h
        zs_ref[...] = jnp.dot(h, ws_ref[...], preferred_element_type=F32)

    z_ref[...] = jnp.dot(h_scr[...], w_ref[...], preferred_element_type=F32)


def _inproj(x2, g_pre1, w_main, w_small, tm, tn=1024):
    n = x2.shape[0]
    return pl.pallas_call(
        _inproj_kernel,
        grid=(n // tm, MAIN_WIDTH // tn),
        in_specs=[
            pl.BlockSpec((tm, D_MODEL), lambda i, j: (i, 0)),
            pl.BlockSpec((1, D_MODEL), lambda i, j: (0, 0)),
            pl.BlockSpec((D_MODEL, tn), lambda i, j: (0, j)),
            pl.BlockSpec((D_MODEL, SMALL_WIDTH), lambda i, j: (0, 0)),
        ],
        out_specs=[
            pl.BlockSpec((tm, tn), lambda i, j: (i, j)),
            pl.BlockSpec((tm, SMALL_WIDTH), lambda i, j: (i, 0)),
        ],
        out_shape=[jax.ShapeDtypeStruct((n, MAIN_WIDTH), F32),
                   jax.ShapeDtypeStruct((n, SMALL_WIDTH), F32)],
        scratch_shapes=[pltpu.VMEM((tm, D_MODEL), BF16)],
        compiler_params=_cparams(("parallel", "arbitrary")),
        name="inproj",
    )(x2, g_pre1.reshape(1, D_MODEL), w_main, w_small)


def _level_mask(ii, jj, s):
    return ((ii // s) % 2 == 1) & (jj // s == ii // s - 1)


def _gdn_kernel(tb, nh, t_valid,
                q_ref, k_ref, v_ref, qh_ref, kh_ref, vh_ref, cpq_ref, cpk_ref, cpv_ref,
                cwq_ref, cwk_ref, cwv_ref, z_ref, zs_ref, hp_ref, nw_ref, s0_ref,
                o_ref, sn_ref, ext_scr, gt_scr, s_scr):
    C = GDN_CHUNK
    nch = tb // C
    hg = pl.program_id(1)
    t = pl.program_id(2)
    nt = pl.num_programs(2)

    @pl.when(t == 0)
    def _():
        s_scr[...] = s0_ref[...]

    def conv_silu(u_ref, halo_ref, cp_ref, cw_ref):
        prev = jnp.where(t == 0, cp_ref[...], halo_ref[...])
        ext_scr[0:8, :] = prev
        ext_scr[8:8 + tb, :] = u_ref[...]
        w = cw_ref[...]
        acc = ext_scr[5:5 + tb, :] * w[0:1, :]
        for j in range(1, CONV_WIDTH):
            acc = acc + ext_scr[5 + j:5 + j + tb, :] * w[j:j + 1, :]
        return _silu(acc)

    q_all = conv_silu(q_ref, qh_ref, cpq_ref, cwq_ref)
    k_all = conv_silu(k_ref, kh_ref, cpk_ref, cwk_ref)
    v_all = conv_silu(v_ref, vh_ref, cpv_ref, cwv_ref)

    zs = zs_ref[...]
    lane = lax.broadcasted_iota(jnp.int32, (GDN_CHUNK, SMALL_WIDTH), 1)
    xa = zs + hp_ref[1:2, :]
    softplus = jnp.maximum(xa, 0.0) + jnp.log1p(jnp.exp(-jnp.abs(xa)))
    g_all = -jnp.exp(hp_ref[0:1, :]) * softplus
    beta_all = _sigmoid(zs)
    if t_valid < tb:
        live = lax.broadcasted_iota(jnp.int32, (tb, 1), 0) < t_valid
        g_all = jnp.where(live, g_all, 0.0)
        beta_all = jnp.where(live, beta_all, 0.0)

    ii = lax.broadcasted_iota(jnp.int32, (C, C), 0)
    jj = lax.broadcasted_iota(jnp.int32, (C, C), 1)
    tril = (ii >= jj).astype(F32)
    eye = (ii == jj).astype(F32)
    gcum = []
    for c in range(nch):
        gc = _hdot(tril, g_all[c * C:(c + 1) * C])
        gcum.append(gc)
        gt_scr[c] = jnp.transpose(gc)

    Ls, rhss, qkds, kdts, qgs, egls = [], [], [], [], [], []
    for hl in range(nh):
        hglob = hg * nh + hl
        hs = slice(hl * HEAD_DIM, (hl + 1) * HEAD_DIM)
        q = q_all[:, hs]
        k = k_all[:, hs]
        q = (q * lax.rsqrt(jnp.sum(q * q, axis=-1, keepdims=True) + RMS_EPS)) * QK_SCALE
        k = k * lax.rsqrt(jnp.sum(k * k, axis=-1, keepdims=True) + RMS_EPS)
        for c in range(nch):
            sl = slice(c * C, (c + 1) * C)
            qc, kc, vc = q[sl], k[sl], v_all[sl, hs]
            bcol = jnp.sum(jnp.where(lane == SMALL_B + hglob, beta_all[sl], 0.0), axis=1, keepdims=True)
            gcol = jnp.sum(jnp.where(lane == SMALL_A + hglob, gcum[c], 0.0), axis=1, keepdims=True)
            grow = gt_scr[c, pl.ds(SMALL_A + hglob, 1), :]
            decay = jnp.exp(jnp.where(ii >= jj, gcol - grow, NEG_INF))
            qkk = _bdot_nt(jnp.concatenate([qc, kc], axis=0), kc)
            Ls.append(jnp.where(ii > jj, qkk[C:] * decay, 0.0) * bcol)
            eg = jnp.exp(gcol)
            g_last = grow[:, C - 1:C]
            rhss.append(jnp.concatenate([vc * bcol, kc * (bcol * eg)], axis=1))
            qkds.append(qkk[:C] * decay)
            kdts.append(jnp.transpose(kc * jnp.exp(g_last - gcol)))
            qgs.append(qc * eg)
            egls.append(jnp.exp(g_last))

    n_inst = nh * nch
    Xs = [eye - jnp.where(_level_mask(ii, jj, 1), L, 0.0) for L in Ls]
    s = 2
    while s < C:
        m = _level_mask(ii, jj, s)
        Ys = [_bdot(jnp.where(m, Ls[i], 0.0), Xs[i]) for i in range(n_inst)]
        Zs = [_bdot(Xs[i], Ys[i]) for i in range(n_inst)]
        Xs = [Xs[i] - Zs[i] for i in range(n_inst)]
        s *= 2
    sols = [_bdot(Xs[i], rhss[i]) for i in range(n_inst)]
    res = [rhss[i] - sols[i] - _hdot(Ls[i], sols[i]) for i in range(n_inst)]
    sols = [sols[i] + _bdot(Xs[i], res[i]) for i in range(n_inst)]
    NPs = [_bdot(kdts[i], sols[i]) for i in range(n_inst)]
    QOs = [_bdot(qkds[i], sols[i]) for i in range(n_inst)]

    Ss = [s_scr[hl] for hl in range(nh)]
    outs = [None] * n_inst
    for c in range(nch):
        for hl in range(nh):
            i = hl * nch + c
            S = Ss[hl]
            outs[i] = _bdot(qgs[i] - QOs[i][:, HEAD_DIM:], S) + QOs[i][:, :HEAD_DIM]
            Ss[hl] = S * egls[i] - _bdot(NPs[i][:, HEAD_DIM:], S) + NPs[i][:, :HEAD_DIM]
    nw = nw_ref[...]
    for hl in range(nh):
        hs = slice(hl * HEAD_DIM, (hl + 1) * HEAD_DIM)
        s_scr[hl] = Ss[hl]
        for c in range(nch):
            sl = slice(c * C, (c + 1) * C)
            o = outs[hl * nch + c]
            o_ref[sl, hs] = ((o * _rms_rows(o)) * nw) * _silu(z_ref[sl, hs])

    @pl.when(t == nt - 1)
    def _():
        sn_ref[...] = s_scr[...]


def _gdn(z3, zs3, conv_prev, conv_w, hp, norm_w, s0, tb, t_valid, nh=4):
    B, T, _ = z3.shape
    nt = T // tb
    hb = tb // 8
    wblk = nh * HEAD_DIM
    cq, ck, cv = COL_QKV // wblk, (COL_QKV + GDN_WIDTH) // wblk, (COL_QKV + 2 * GDN_WIDTH) // wblk
    hpg = GDN_HEADS // nh

    def main(col0):
        return pl.BlockSpec((None, tb, wblk), lambda b, h, t: (b, t, col0 + h))

    def halo(col0):
        return pl.BlockSpec((None, 8, wblk), lambda b, h, t: (b, jnp.maximum(t * hb - 1, 0), col0 + h))

    def cprev(col0):
        return pl.BlockSpec((None, 8, wblk), lambda b, h, t: (b, 0, col0 + h))

    def cw(col0):
        return pl.BlockSpec((CONV_WIDTH, wblk), lambda b, h, t: (0, col0 + h))

    state = pl.BlockSpec((None, nh, HEAD_DIM, HEAD_DIM), lambda b, h, t: (b, h, 0, 0))
    return pl.pallas_call(
        functools.partial(_gdn_kernel, tb, nh, t_valid),
        grid=(B, hpg, nt),
        in_specs=[
            main(cq), main(ck), main(cv), halo(cq), halo(ck), halo(cv),
            cprev(0), cprev(hpg), cprev(2 * hpg), cw(0), cw(hpg), cw(2 * hpg),
            main(COL_ZA // wblk),
            pl.BlockSpec((None, tb, SMALL_WIDTH), lambda b, h, t: (b, t, 0)),
            pl.BlockSpec((8, SMALL_WIDTH), lambda b, h, t: (0, 0)),
            pl.BlockSpec((1, HEAD_DIM), lambda b, h, t: (0, 0)),
            state,
        ],
        out_specs=[pl.BlockSpec((None, tb, wblk), lambda b, h, t: (b, t, h)), state],
        out_shape=[jax.ShapeDtypeStruct((B, T, GDN_WIDTH), F32),
                   jax.ShapeDtypeStruct((B, GDN_HEADS, HEAD_DIM, HEAD_DIM), F32)],
        scratch_shapes=[pltpu.VMEM((tb + 8, wblk), F32),
                        pltpu.VMEM((tb // GDN_CHUNK, SMALL_WIDTH, GDN_CHUNK), F32),
                        pltpu.VMEM((nh, HEAD_DIM, HEAD_DIM), F32)],
        compiler_params=_cparams(("parallel", "parallel", "arbitrary")),
        name="gdn",
    )(z3, z3, z3, z3, z3, z3, conv_prev, conv_prev, conv_prev, conv_w, conv_w, conv_w,
      z3, zs3, hp, norm_w.reshape(1, HEAD_DIM), s0)


def _compress_kernel(x_ref, w_ref, o_ref):
    rows = x_ref.shape[0]
    x = x_ref[...].reshape(rows // CMP_BLOCK, CMP_BLOCK, 2 * NSA_KV_WIDTH)
    o_ref[...] = jnp.sum(x * w_ref[...][None], axis=1)


def _compress_prompt(z3, w_cmp2, tc=256):
    B, T, _ = z3.shape
    nc = T // CMP_BLOCK
    return pl.pallas_call(
        _compress_kernel,
        grid=(B, T // tc),
        in_specs=[pl.BlockSpec((None, tc, 2 * NSA_KV_WIDTH), lambda b, i: (b, i, COL_CMP // 1024)),
                  pl.BlockSpec((CMP_BLOCK, 2 * NSA_KV_WIDTH), lambda b, i: (0, 0))],
        out_specs=pl.BlockSpec((None, tc // CMP_BLOCK, 2 * NSA_KV_WIDTH), lambda b, i: (b, i, 0)),
        out_shape=jax.ShapeDtypeStruct((B, nc, 2 * NSA_KV_WIDTH), F32),
        compiler_params=_cparams(("parallel", "parallel")),
        name="compress_prompt",
    )(z3, w_cmp2)


def _topk_mask(sc, k_sel):
    n = sc.shape[1]
    lanef = lax.broadcasted_iota(jnp.int32, sc.shape, 1).astype(F32)
    sel = jnp.zeros(sc.shape, F32)
    for _ in range(k_sel):
        m = jnp.max(sc, axis=1, keepdims=True)
        idx = jnp.min(jnp.where(sc == m, lanef, float(n)), axis=1, keepdims=True)
        hit = lanef == idx
        sel = jnp.where(hit, 1.0, sel)
        sc = jnp.where(hit, -3.0, sc)
    return sel


def _cmp_attn_kernel(ta, nc, ns, ns_real, t_base, slopes_ref, q_ref, kvc_ref, zs_ref, oc_ref, sel_ref, sc_scr):
    t0 = t_base + pl.program_id(1) * ta
    tpos = t0 + lax.broadcasted_iota(jnp.int32, (ta, nc), 0)
    cend = lax.broadcasted_iota(jnp.int32, (ta, nc), 1) * CMP_BLOCK + (CMP_BLOCK - 1)
    dist = tpos - cend
    valid = dist >= 0
    distf = dist.astype(F32)
    gates = _sigmoid(zs_ref[...])
    pc = lax.broadcasted_iota(jnp.int32, (nc, ns), 0)
    ps = lax.broadcasted_iota(jnp.int32, (nc, ns), 1)
    pool = (pc // (SEL_BLOCK // CMP_BLOCK) == ps).astype(F32)
    tq = t0 + lax.broadcasted_iota(jnp.int32, (ta, ns), 0)
    blk = lax.broadcasted_iota(jnp.int32, (ta, ns), 1)
    cur = tq // SEL_BLOCK
    forced = (blk == 0) | (blk == cur) | (blk == cur - 1)
    avail = blk * SEL_BLOCK <= tq
    for g in range(NSA_KV_HEADS):
        kc = kvc_ref[:, g * HEAD_DIM:(g + 1) * HEAD_DIM]
        vc = kvc_ref[:, NSA_KV_WIDTH + g * HEAD_DIM:NSA_KV_WIDTH + (g + 1) * HEAD_DIM]
        imp = jnp.zeros((ta, nc), F32)
        for r in range(NSA_GROUP):
            hd = g * NSA_GROUP + r
            qh = q_ref[:, hd * HEAD_DIM:(hd + 1) * HEAD_DIM] * QK_SCALE
            s = _hdot_nt(qh, kc) - slopes_ref[hd] * distf
            s = jnp.where(valid, s, NEG_INF)
            e = jnp.exp(s - jnp.max(s, axis=1, keepdims=True))
            p = jnp.where(valid, e / jnp.sum(e, axis=1, keepdims=True), 0.0)
            imp = imp + p
            gi = SMALL_GN + hd * 3
            oc_ref[:, hd * HEAD_DIM:(hd + 1) * HEAD_DIM] = gates[:, gi:gi + 1] * _bdot(p, vc)
        imps = _hdot(imp, pool)
        score = jnp.where(forced, imps + FORCE_BONUS, jnp.where(avail, imps, -1.0))
        if ns_real < ns:
            score = jnp.where(blk < ns_real, score, -2.0)
        sc_scr[g * ta:(g + 1) * ta, :] = score
    sel = _topk_mask(sc_scr[...], min(SEL_TOPK, ns_real))
    for g in range(NSA_KV_HEADS):
        sel_ref[g] = sel[g * ta:(g + 1) * ta, :]


def _cmp_attn(z3, zs3, kvc, slopes, ta, n_tok, ns, ns_real, t_base, name):
    B = z3.shape[0]
    nc = kvc.shape[1]
    return pl.pallas_call(
        functools.partial(_cmp_attn_kernel, ta, nc, ns, ns_real, t_base),
        grid_spec=pltpu.PrefetchScalarGridSpec(
            num_scalar_prefetch=1,
            grid=(B, n_tok // ta),
            in_specs=[
                pl.BlockSpec((None, ta, NSA_HEADS * HEAD_DIM), lambda b, i, sl: (b, i, COL_QN // 2048)),
                pl.BlockSpec((None, nc, 2 * NSA_KV_WIDTH), lambda b, i, sl: (b, 0, 0)),
                pl.BlockSpec((None, ta, SMALL_WIDTH), lambda b, i, sl: (b, i, 0)),
            ],
            out_specs=[
                pl.BlockSpec((None, ta, NSA_HEADS * HEAD_DIM), lambda b, i, sl: (b, i, 0)),
                pl.BlockSpec((None, NSA_KV_HEADS, ta, ns), lambda b, i, sl: (b, 0, i, 0)),
            ],
            scratch_shapes=[pltpu.VMEM((NSA_KV_HEADS * ta, ns), F32)],
        ),
        out_shape=[jax.ShapeDtypeStruct((B, n_tok, NSA_HEADS * HEAD_DIM), F32),
                   jax.ShapeDtypeStruct((B, NSA_KV_HEADS, n_tok, ns), F32)],
        compiler_params=_cparams(("parallel", "parallel")),
        name=name,
    )(slopes, z3, kvc, zs3)


def _sel_win_kernel(T, ns, tkv, wspan, slopes_ref, q_ref, ks_ref, vs_ref, kw_ref, vw_ref, sel_ref,
                    zs_ref, oc_ref, o_ref):
    QB = SEL_BLOCK
    R = NSA_GROUP
    g = pl.program_id(1)
    qb = pl.program_id(2)
    t0 = qb * QB
    q = jnp.concatenate([(q_ref[:, r * HEAD_DIM:(r + 1) * HEAD_DIM] * (QK_SCALE * LOG2E)).astype(BF16)
                         for r in range(R)], axis=0)
    slope2 = [slopes_ref[g * R + r] * LOG2E for r in range(R)]
    selb = sel_ref[...].astype(BF16)

    def head_rows(x, r):
        return x[r * QB:(r + 1) * QB]

    def spread(cols):
        return jnp.concatenate([jnp.broadcast_to(c, (QB, HEAD_DIM)) for c in cols], axis=0)

    ti = lax.broadcasted_iota(jnp.int32, (QB, tkv), 0)
    kj = lax.broadcasted_iota(jnp.int32, (QB, tkv), 1)
    d0 = (ti - kj).astype(F32)
    eb = lax.broadcasted_iota(jnp.int32, (ns, tkv), 0)
    ek = lax.broadcasted_iota(jnp.int32, (ns, tkv), 1) // SEL_BLOCK

    def body(j, carry):
        ms, ls, acc = carry
        k0 = pl.multiple_of(j * tkv, tkv)
        kt = ks_ref[pl.ds(k0, tkv), :]
        vt = vs_ref[pl.ds(k0, tkv), :]
        expand = (eb == ek + j * (tkv // SEL_BLOCK)).astype(BF16)
        keymask = jnp.dot(selb, expand, preferred_element_type=F32)
        distf = d0 + (t0 - k0).astype(F32)
        base = jnp.where((distf >= 0.0) & (keymask > 0.5), distf, MASK_DIST)
        s = _bdot_nt(q, kt)
        ps, ms2, ls2, alphas = [], [], [], []
        for r in range(R):
            sr = head_rows(s, r) - slope2[r] * base
            m_new = jnp.maximum(ms[r], jnp.max(sr, axis=1, keepdims=True))
            p = jnp.exp2(sr - m_new)
            alpha = jnp.exp2(ms[r] - m_new)
            ps.append(p.astype(BF16))
            ms2.append(m_new)
            ls2.append(alpha * ls[r] + jnp.sum(p, axis=1, keepdims=True))
            alphas.append(alpha)
        pv = jnp.dot(jnp.concatenate(ps, axis=0), vt.astype(BF16), preferred_element_type=F32)
        return tuple(ms2), tuple(ls2), spread(alphas) * acc + pv

    n_tiles = (t0 + QB + tkv - 1) // tkv
    init = (tuple(jnp.full((QB, 1), NEG_INF, F32) for _ in range(R)),
            tuple(jnp.zeros((QB, 1), F32) for _ in range(R)),
            jnp.zeros((R * QB, HEAD_DIM), F32))
    _, l_s, acc_s = lax.fori_loop(0, n_tiles, body, init)

    kstart = pl.multiple_of(jnp.clip(t0 - WINDOW, 0, T - wspan), SEL_BLOCK)
    kw = kw_ref[pl.ds(kstart, wspan), :]
    vw = vw_ref[pl.ds(kstart, wspan), :]
    tw = lax.broadcasted_iota(jnp.int32, (QB, wspan), 0)
    kwj = lax.broadcasted_iota(jnp.int32, (QB, wspan), 1)
    dist = (tw - kwj) + (t0 - kstart)
    base_w = jnp.where((dist >= 0) & (dist < WINDOW), dist.astype(F32), MASK_DIST)
    s = _bdot_nt(q, kw)
    pw, lw = [], []
    for r in range(R):
        sr = head_rows(s, r) - slope2[r] * base_w
        e = jnp.exp2(sr - jnp.max(sr, axis=1, keepdims=True))
        pw.append(e.astype(BF16))
        lw.append(jnp.sum(e, axis=1, keepdims=True))
    acc_w = jnp.dot(jnp.concatenate(pw, axis=0), vw.astype(BF16), preferred_element_type=F32)

    gates = _sigmoid(zs_ref[...])
    lane = lax.broadcasted_iota(jnp.int32, (QB, SMALL_WIDTH), 1)
    for r in range(R):
        gi = SMALL_GN + (g * R + r) * 3
        g_s = jnp.sum(jnp.where(lane == gi + 1, gates, 0.0), axis=1, keepdims=True)
        g_w = jnp.sum(jnp.where(lane == gi + 2, gates, 0.0), axis=1, keepdims=True)
        cs = slice(r * HEAD_DIM, (r + 1) * HEAD_DIM)
        o_ref[:, cs] = (oc_ref[:, cs] + (g_s / l_s[r]) * head_rows(acc_s, r)
                        + (g_w / lw[r]) * head_rows(acc_w, r))


def _sel_win_prompt(z3, zs3, selmask, oc, slopes):
    B, T, _ = z3.shape
    ns = T // SEL_BLOCK
    tkv = min(1024, T)
    wspan = min(WINDOW + 2 * SEL_BLOCK, T)
    gw = NSA_GROUP * HEAD_DIM

    def kv(col0):
        return pl.BlockSpec((None, T, HEAD_DIM), lambda b, g, i, sl: (b, 0, col0 // 128 + g))

    return pl.pallas_call(
        functools.partial(_sel_win_kernel, T, ns, tkv, wspan),
        grid_spec=pltpu.PrefetchScalarGridSpec(
            num_scalar_prefetch=1,
            grid=(B, NSA_KV_HEADS, T // SEL_BLOCK),
            in_specs=[
                pl.BlockSpec((None, SEL_BLOCK, gw), lambda b, g, i, sl: (b, i, COL_QN // gw + g)),
                kv(COL_SEL), kv(COL_SEL + NSA_KV_WIDTH), kv(COL_WIN), kv(COL_WIN + NSA_KV_WIDTH),
                pl.BlockSpec((None, None, SEL_BLOCK, ns), lambda b, g, i, sl: (b, g, i, 0)),
                pl.BlockSpec((None, SEL_BLOCK, SMALL_WIDTH), lambda b, g, i, sl: (b, i, 0)),
                pl.BlockSpec((None, SEL_BLOCK, gw), lambda b, g, i, sl: (b, i, g)),
            ],
            out_specs=pl.BlockSpec((None, SEL_BLOCK, gw), lambda b, g, i, sl: (b, i, g)),
        ),
        out_shape=jax.ShapeDtypeStruct((B, T, NSA_HEADS * HEAD_DIM), F32),
        compiler_params=_cparams(("parallel", "parallel", "arbitrary")),
        name="sel_win_prompt",
    )(slopes, z3, z3, z3, z3, z3, selmask, zs3, oc)


def _merge_kernel(gma_ref, gmn_ref, oa_ref, on_ref, x_ref, w_ref, g_ref, o_ref):
    mixed = _sigmoid(gma_ref[...]) * oa_ref[...] + _sigmoid(gmn_ref[...]) * on_ref[...]
    y = jnp.dot(mixed.astype(BF16), w_ref[...], preferred_element_type=F32)
    o_ref[...] = x_ref[...] + (y * _rms_rows(y)) * g_ref[...]


def _merge_out(z2, o_a, o_n, x2, w_o, g_post1, tm):
    n = x2.shape[0]
    row = lambda c: pl.BlockSpec((tm, D_MODEL), lambda i: (i, c))
    return pl.pallas_call(
        _merge_kernel,
        grid=(n // tm,),
        in_specs=[row(COL_GMA // D_MODEL), row(COL_GMN // D_MODEL), row(0), row(0), row(0),
                  pl.BlockSpec((D_MODEL, D_MODEL), lambda i: (0, 0)),
                  pl.BlockSpec((1, D_MODEL), lambda i: (0, 0))],
        out_specs=row(0),
        out_shape=jax.ShapeDtypeStruct((n, D_MODEL), F32),
        compiler_params=_cparams(("parallel",)),
        name="merge_out",
    )(z2, z2, o_a, o_n, x2, w_o, g_post1.reshape(1, D_MODEL))


def _ffn_kernel(x_ref, g2_ref, wg_ref, wu_ref, wo_ref, gp_ref, o_ref, h_scr, acc_scr):
    j = pl.program_id(1)

    @pl.when(j == 0)
    def _():
        x = x_ref[...]
        h_scr[...] = ((x * _rms_rows(x)) * g2_ref[...]).astype(BF16)
        acc_scr[...] = jnp.zeros_like(acc_scr)

    h = h_scr[...]
    gt = jnp.dot(h, wg_ref[...], preferred_element_type=F32)
    up = jnp.dot(h, wu_ref[...], preferred_element_type=F32)
    acc_scr[...] += jnp.dot((_silu(gt) * up).astype(BF16), wo_ref[...], preferred_element_type=F32)

    @pl.when(j == pl.num_programs(1) - 1)
    def _():
        y = acc_scr[...]
        o_ref[...] = x_ref[...] + (y * _rms_rows(y)) * gp_ref[...]


def _ffn(x2, g_pre2, w_ffn_in, w_ffn_out, g_post2, tm, th=512):
    n = x2.shape[0]
    nh = FFN_HIDDEN // th
    return pl.pallas_call(
        _ffn_kernel,
        grid=(n // tm, nh),
        in_specs=[
            pl.BlockSpec((tm, D_MODEL), lambda i, j: (i, 0)),
            pl.BlockSpec((1, D_MODEL), lambda i, j: (0, 0)),
            pl.BlockSpec((D_MODEL, th), lambda i, j: (0, j)),
            pl.BlockSpec((D_MODEL, th), lambda i, j: (0, nh + j)),
            pl.BlockSpec((th, D_MODEL), lambda i, j: (j, 0)),
            pl.BlockSpec((1, D_MODEL), lambda i, j: (0, 0)),
        ],
        out_specs=pl.BlockSpec((tm, D_MODEL), lambda i, j: (i, 0)),
        out_shape=jax.ShapeDtypeStruct((n, D_MODEL), F32),
        scratch_shapes=[pltpu.VMEM((tm, D_MODEL), BF16), pltpu.VMEM((tm, D_MODEL), F32)],
        compiler_params=_cparams(("parallel", "arbitrary")),
        name="ffn",
    )(x2, g_pre2.reshape(1, D_MODEL), w_ffn_in, w_ffn_in, w_ffn_out, g_post2.reshape(1, D_MODEL))


def _ple_kernel(x_ref, p_ref, wg_ref, wp_ref, o_ref):
    x = x_ref[...]
    gate = _sigmoid(jnp.dot(x.astype(BF16), wg_ref[...], preferred_element_type=F32))
    o_ref[...] = x + gate * jnp.dot(p_ref[...].astype(BF16), wp_ref[...], preferred_element_type=F32)


def _ple(x2, ple2, w_ple_gate, w_ple, tm):
    n = x2.shape[0]
    return pl.pallas_call(
        _ple_kernel,
        grid=(n // tm,),
        in_specs=[pl.BlockSpec((tm, D_MODEL), lambda i: (i, 0)),
                  pl.BlockSpec((tm, PLE_DIM), lambda i: (i, 0)),
                  pl.BlockSpec((D_MODEL, D_MODEL), lambda i: (0, 0)),
                  pl.BlockSpec((PLE_DIM, D_MODEL), lambda i: (0, 0))],
        out_specs=pl.BlockSpec((tm, D_MODEL), lambda i: (i, 0)),
        out_shape=jax.ShapeDtypeStruct((n, D_MODEL), F32),
        compiler_params=_cparams(("parallel",)),
        name="ple",
    )(x2, ple2, w_ple_gate, w_ple)


SAMPLE_ROWS = 8
KV_ROW = 2 * NSA_KV_HEADS


def _compress_pages_kernel(npg, pt_ref, *refs):
    w = refs[npg][...]
    o_ref = refs[npg + 1]
    per = PAGE_SIZE // CMP_BLOCK
    for p in range(npg):
        x = refs[p][...].reshape(per, CMP_BLOCK, KV_ROW, HEAD_DIM)
        o_ref[p * per:(p + 1) * per] = jnp.sum(x * w[None], axis=1)


def _compress_sample(cache_rows, page_table, w_cmp3, npg=4):
    Bs, n_pages = page_table.shape
    per = PAGE_SIZE // CMP_BLOCK

    def page(p):
        return pl.BlockSpec((None, PAGE_SIZE * KV_ROW, HEAD_DIM), lambda b, i, pt: (pt[b, i * npg + p], 0, 0))

    return pl.pallas_call(
        functools.partial(_compress_pages_kernel, npg),
        grid_spec=pltpu.PrefetchScalarGridSpec(
            num_scalar_prefetch=1,
            grid=(Bs, n_pages // npg),
            in_specs=[page(p) for p in range(npg)]
            + [pl.BlockSpec((CMP_BLOCK, KV_ROW, HEAD_DIM), lambda b, i, pt: (0, 0, 0))],
            out_specs=pl.BlockSpec((None, npg * per, KV_ROW, HEAD_DIM), lambda b, i, pt: (b, i, 0, 0)),
        ),
        out_shape=jax.ShapeDtypeStruct((Bs, n_pages * per, KV_ROW, HEAD_DIM), F32),
        compiler_params=_cparams(("parallel", "parallel")),
        name="compress_sample",
    )(page_table, *([cache_rows] * npg), w_cmp3)


def _sel_pages_kernel(npg, past_len, pt_ref, slopes_ref, q_ref, sel_ref, *refs):
    pages = refs[:npg]
    m_ref, l_ref, acc_ref, q_scr, selrows_scr, slope_scr = refs[npg:]
    i = pl.program_id(1)
    nrow = NSA_HEADS * SAMPLE_ROWS
    rg = NSA_GROUP * SAMPLE_ROWS
    ns_pad = sel_ref.shape[-1]

    @pl.when(i == 0)
    def _():
        for hd in range(NSA_HEADS):
            rs = slice(hd * SAMPLE_ROWS, (hd + 1) * SAMPLE_ROWS)
            q_scr[rs, :] = q_ref[:, hd * HEAD_DIM:(hd + 1) * HEAD_DIM] * (QK_SCALE * LOG2E)
            selrows_scr[rs, :] = sel_ref[hd // NSA_GROUP]
            slope_scr[rs, :] = jnp.full((SAMPLE_ROWS, HEAD_DIM), slopes_ref[hd] * LOG2E, F32)
        m_ref[...] = jnp.full(m_ref.shape, NEG_INF, F32)
        l_ref[...] = jnp.zeros(l_ref.shape, F32)
        acc_ref[...] = jnp.zeros(acc_ref.shape, F32)

    qb = q_scr[...].astype(BF16)
    selb = selrows_scr[...].astype(BF16)
    slope = slope_scr[...]
    row = lax.broadcasted_iota(jnp.int32, (nrow, PAGE_SIZE), 0)
    pos = lax.broadcasted_iota(jnp.int32, (nrow, PAGE_SIZE), 1)
    d0 = (past_len + row % SAMPLE_ROWS - pos).astype(F32)
    ob = lax.broadcasted_iota(jnp.int32, (ns_pad, PAGE_SIZE), 0)
    ol = lax.broadcasted_iota(jnp.int32, (ns_pad, PAGE_SIZE), 1) // SEL_BLOCK
    for p in range(npg):
        pg = i * npg + p
        page = pages[p]
        flags = jnp.dot(selb, (ob == ol + pg * (PAGE_SIZE // SEL_BLOCK)).astype(BF16),
                        preferred_element_type=F32)
        distf = d0 - (pg * PAGE_SIZE).astype(F32)
        base = jnp.where((flags > 0.5) & (distf >= 0.0), distf, MASK_DIST)
        s = jnp.concatenate(
            [_bdot_nt(qb[g * rg:(g + 1) * rg], page[pl.ds(g, PAGE_SIZE, stride=KV_ROW), :])
             for g in range(NSA_KV_HEADS)], axis=0) - slope * base
        m_old = m_ref[...]
        m_new = jnp.maximum(m_old, jnp.max(s, axis=1, keepdims=True))
        pr = jnp.exp2(s - m_new)
        alpha = jnp.exp2(m_old - m_new)
        l_ref[...] = alpha * l_ref[...] + jnp.sum(pr, axis=1, keepdims=True)
        prb = pr.astype(BF16)
        pv = jnp.concatenate(
            [_bdot(prb[g * rg:(g + 1) * rg], page[pl.ds(NSA_KV_HEADS + g, PAGE_SIZE, stride=KV_ROW), :])
             for g in range(NSA_KV_HEADS)], axis=0)
        acc_ref[...] = alpha * acc_ref[...] + pv
        m_ref[...] = m_new


def _sel_sample(z3s, selmask, cache_rows, page_table, slopes, past_len, npg=4):
    Bs, n_pages = page_table.shape
    nrow = NSA_HEADS * SAMPLE_ROWS
    ns_pad = selmask.shape[-1]

    def page(p):
        return pl.BlockSpec((None, PAGE_SIZE * KV_ROW, HEAD_DIM), lambda b, i, pt, sl: (pt[b, i * npg + p], 0, 0))

    part = pl.BlockSpec((None, nrow, HEAD_DIM), lambda b, i, pt, sl: (b, 0, 0))
    return pl.pallas_call(
        functools.partial(_sel_pages_kernel, npg, past_len),
        grid_spec=pltpu.PrefetchScalarGridSpec(
            num_scalar_prefetch=2,
            grid=(Bs, n_pages // npg),
            in_specs=[
                pl.BlockSpec((None, SAMPLE_ROWS, NSA_HEADS * HEAD_DIM), lambda b, i, pt, sl: (b, 0, COL_QN // 2048)),
                pl.BlockSpec((None, NSA_KV_HEADS, SAMPLE_ROWS, ns_pad), lambda b, i, pt, sl: (b, 0, 0, 0)),
            ] + [page(p) for p in range(npg)],
            out_specs=[part, part, part],
            scratch_shapes=[pltpu.VMEM((nrow, HEAD_DIM), F32), pltpu.VMEM((nrow, ns_pad), F32),
                            pltpu.VMEM((nrow, HEAD_DIM), F32)],
        ),
        out_shape=[jax.ShapeDtypeStruct((Bs, nrow, HEAD_DIM), F32)] * 3,
        compiler_params=_cparams(("parallel", "arbitrary")),
        name="sel_sample",
    )(page_table, slopes, z3s, selmask, *([cache_rows] * npg))


def _finish_sample_kernel(past_len, t_real, nnew, slopes_ref, q_ref, kst_ref, snew_ref, wnew_ref, sel_ref,
                          m_ref, l_ref, acc_ref, oc_ref, zs_ref, o_ref):
    nst = kst_ref.shape[0]
    cur = past_len // SEL_BLOCK
    gates = _sigmoid(zs_ref[...])
    t_new = lax.broadcasted_iota(jnp.int32, (SAMPLE_ROWS, nnew), 0)
    j_new = lax.broadcasted_iota(jnp.int32, (SAMPLE_ROWS, nnew), 1)
    dist_new = t_new - j_new
    ok_new = (dist_new >= 0) & (j_new < t_real)
    t_st = lax.broadcasted_iota(jnp.int32, (SAMPLE_ROWS, nst), 0)
    i_st = lax.broadcasted_iota(jnp.int32, (SAMPLE_ROWS, nst), 1)
    dist_st = t_st + nst - i_st
    ok_st = dist_st < WINDOW
    for hd in range(NSA_HEADS):
        g = hd // NSA_GROUP
        rs = slice(hd * SAMPLE_ROWS, (hd + 1) * SAMPLE_ROWS)
        kc = slice(g * HEAD_DIM, (g + 1) * HEAD_DIM)
        vc = slice(NSA_KV_WIDTH + g * HEAD_DIM, NSA_KV_WIDTH + (g + 1) * HEAD_DIM)
        sl = slopes_ref[hd] * LOG2E
        qh = (q_ref[:, hd * HEAD_DIM:(hd + 1) * HEAD_DIM] * (QK_SCALE * LOG2E)).astype(BF16)
        valid = ok_new & (sel_ref[g][:, cur:cur + 1] > 0.5)
        s = jnp.where(valid, _bdot_nt(qh, snew_ref[:, kc]) - sl * dist_new.astype(F32), NEG_INF)
        m_old = m_ref[rs, 0:1]
        m_new = jnp.maximum(m_old, jnp.max(s, axis=1, keepdims=True))
        pr = jnp.where(valid, jnp.exp2(s - m_new), 0.0)
        alpha = jnp.exp2(m_old - m_new)
        l = alpha * l_ref[rs, 0:1] + jnp.sum(pr, axis=1, keepdims=True)
        o_s = (alpha * acc_ref[rs, :] + _bdot(pr, snew_ref[:, vc])) / l
        s1 = jnp.where(ok_st, _bdot_nt(qh, kst_ref[:, kc]) - sl * dist_st.astype(F32), NEG_INF)
        s2 = jnp.where(ok_new, _bdot_nt(qh, wnew_ref[:, kc]) - sl * dist_new.astype(F32), NEG_INF)
        mw = jnp.maximum(jnp.max(s1, axis=1, keepdims=True), jnp.max(s2, axis=1, keepdims=True))
        e1 = jnp.where(ok_st, jnp.exp2(s1 - mw), 0.0)
        e2 = jnp.where(ok_new, jnp.exp2(s2 - mw), 0.0)
        den = jnp.sum(e1, axis=1, keepdims=True) + jnp.sum(e2, axis=1, keepdims=True)
        o_w = _bdot(e1 / den, kst_ref[:, vc]) + _bdot(e2 / den, wnew_ref[:, vc])
        gi = SMALL_GN + hd * 3
        cs = slice(hd * HEAD_DIM, (hd + 1) * HEAD_DIM)
        o_ref[:, cs] = oc_ref[:, cs] + gates[:, gi + 1:gi + 2] * o_s + gates[:, gi + 2:gi + 3] * o_w


def _finish_sample(z3s, zs3s, state_win2, selmask, m, l, acc, oc, slopes, past_len, t_real):
    Bs, tz, _ = z3s.shape
    nst = state_win2.shape[1]
    nrow = NSA_HEADS * SAMPLE_ROWS
    ns_pad = selmask.shape[-1]
    kvw = 2 * NSA_KV_WIDTH
    part = pl.BlockSpec((None, nrow, HEAD_DIM), lambda b, sl: (b, 0, 0))
    wide = pl.BlockSpec((None, SAMPLE_ROWS, NSA_HEADS * HEAD_DIM), lambda b, sl: (b, 0, 0))
    return pl.pallas_call(
        functools.partial(_finish_sample_kernel, past_len, t_real, tz),
        grid_spec=pltpu.PrefetchScalarGridSpec(
            num_scalar_prefetch=1,
            grid=(Bs,),
            in_specs=[
                pl.BlockSpec((None, SAMPLE_ROWS, NSA_HEADS * HEAD_DIM), lambda b, sl: (b, 0, COL_QN // 2048)),
                pl.BlockSpec((None, nst, kvw), lambda b, sl: (b, 0, 0)),
                pl.BlockSpec((None, tz, kvw), lambda b, sl: (b, 0, COL_SEL // kvw)),
                pl.BlockSpec((None, tz, kvw), lambda b, sl: (b, 0, COL_WIN // kvw)),
                pl.BlockSpec((None, NSA_KV_HEADS, SAMPLE_ROWS, ns_pad), lambda b, sl: (b, 0, 0, 0)),
                part, part, part, wide,
                pl.BlockSpec((None, SAMPLE_ROWS, SMALL_WIDTH), lambda b, sl: (b, 0, 0)),
            ],
            out_specs=wide,
        ),
        out_shape=jax.ShapeDtypeStruct((Bs, SAMPLE_ROWS, NSA_HEADS * HEAD_DIM), F32),
        compiler_params=_cparams(("parallel",)),
        name="finish_sample",
    )(slopes, z3s, state_win2, z3s, z3s, selmask, m, l, acc, oc, zs3s)


def _mix_and_ffn(z3, o_a, o_n, x3, ple3, wts, tm):
    B, T, _ = x3.shape
    n = B * T
    x1 = _merge_out(z3.reshape(n, MAIN_WIDTH), o_a.reshape(n, GDN_WIDTH), o_n.reshape(n, D_MODEL),
                    x3.reshape(n, D_MODEL), wts["w_o"], wts["g_post1"], min(tm, 256))
    x2 = _ffn(x1, wts["g_pre2"], wts["w_ffn_in"], wts["w_ffn_out"], wts["g_post2"], tm)
    x3o = _ple(x2, ple3.reshape(n, PLE_DIM), wts["w_ple_gate"], wts["w_ple"], min(tm, 256))
    return x3o.reshape(B, T, D_MODEL)


def kernel(x_prompt, x_sample, cache_cmp_kv, cache_sel_kv, page_table, state_win_kv, state_gdn, state_conv, p_prompt, p_sample, g_pre1, w_in, conv_w, A_log, dt_bias, gdn_norm_w, w_cmp, w_o, g_post1, g_pre2, w_ffn_in, w_ffn_out, g_post2, w_ple, w_ple_gate):
    B, T, _ = x_prompt.shape
    Bs, Ts, _ = x_sample.shape
    n_pages = page_table.shape[1]
    past_len = n_pages * PAGE_SIZE
    win_buf = state_win_kv.shape[2]
    kvh = (2, NSA_KV_HEADS, HEAD_DIM)
    qkv_w = 3 * GDN_WIDTH

    wi = w_in[0]
    w_main = jnp.concatenate([wi[:, 0:8192], wi[:, 13392:17488], wi[:, 8224:13344]], axis=1).astype(BF16)
    w_small = jnp.concatenate([wi[:, 8192:8224], wi[:, 13344:13392],
                               jnp.zeros((D_MODEL, SMALL_WIDTH - 80), F32)], axis=1).astype(BF16)
    wts = dict(w_o=w_o[0].astype(BF16), g_post1=g_post1[0], g_pre2=g_pre2[0],
               w_ffn_in=w_ffn_in[0].astype(BF16), w_ffn_out=w_ffn_out[0].astype(BF16), g_post2=g_post2[0],
               w_ple=w_ple[0].astype(BF16), w_ple_gate=w_ple_gate[0].astype(BF16))
    hp = jnp.zeros((8, SMALL_WIDTH), F32).at[0, 0:GDN_HEADS].set(A_log[0]).at[1, 0:GDN_HEADS].set(dt_bias[0])
    w_cmp2 = w_cmp[0].reshape(CMP_BLOCK, 2 * NSA_KV_WIDTH)
    heads = jnp.arange(1, NSA_HEADS + 1, dtype=F32)
    slopes = jnp.exp2(-8.0 * heads / NSA_HEADS)

    z2, zs2 = _inproj(x_prompt.reshape(B * T, D_MODEL), g_pre1[0], w_main, w_small, 512)
    z3, zs3 = z2.reshape(B, T, MAIN_WIDTH), zs2.reshape(B, T, SMALL_WIDTH)
    o_a, s_new_p = _gdn(z3, zs3, jnp.zeros((B, 8, qkv_w), F32), conv_w[0], hp, gdn_norm_w[0],
                        jnp.zeros((B, GDN_HEADS, HEAD_DIM, HEAD_DIM), F32), 256, 256)
    kvc = _compress_prompt(z3, w_cmp2)
    ns = T // SEL_BLOCK
    oc, selmask = _cmp_attn(z3, zs3, kvc, slopes, 512, T, ns, ns, 0, "cmp_attn_prompt")
    o_n = _sel_win_prompt(z3, zs3, selmask, oc, slopes)
    y_prompt = _mix_and_ffn(z3, o_a, o_n, x_prompt, p_prompt[0], wts, 512)

    tz = GDN_CHUNK
    xs = jnp.pad(x_sample, ((0, 0), (0, tz - Ts), (0, 0)))
    zs2_, zss2 = _inproj(xs.reshape(Bs * tz, D_MODEL), g_pre1[0], w_main, w_small, Bs * tz)
    z3s, zs3s = zs2_.reshape(Bs, tz, MAIN_WIDTH), zss2.reshape(Bs, tz, SMALL_WIDTH)
    conv_prev = jnp.pad(state_conv[0], ((0, 0), (8 - (CONV_WIDTH - 1), 0), (0, 0)))
    o_a_s, s_new_s = _gdn(z3s, zs3s, conv_prev, conv_w[0], hp, gdn_norm_w[0], state_gdn[0], tz, Ts)
    n_pool = cache_cmp_kv.shape[1]
    kvc_s = _compress_sample(cache_cmp_kv[0].reshape(n_pool, PAGE_SIZE * KV_ROW, HEAD_DIM), page_table,
                             w_cmp[0].reshape(CMP_BLOCK, KV_ROW, HEAD_DIM))
    kvc_s = kvc_s.reshape(Bs, kvc_s.shape[1], 2 * NSA_KV_WIDTH)
    ns_real = -(-(past_len + Ts) // SEL_BLOCK)
    ns_pad = -(-ns_real // 128) * 128
    oc_s, selmask_s = _cmp_attn(z3s, zs3s, kvc_s, slopes, SAMPLE_ROWS, SAMPLE_ROWS, ns_pad, ns_real, past_len,
                                "cmp_attn_sample")
    m_s, l_s, acc_s = _sel_sample(z3s, selmask_s, cache_sel_kv[0].reshape(n_pool, PAGE_SIZE * KV_ROW, HEAD_DIM),
                                  page_table, slopes, past_len)
    state_win2 = state_win_kv[0].reshape(Bs, win_buf, 2 * NSA_KV_WIDTH)
    o_n_s = _finish_sample(z3s, zs3s, state_win2, selmask_s, m_s, l_s, acc_s, oc_s, slopes, past_len, Ts)
    o_n_s = jnp.pad(o_n_s, ((0, 0), (0, tz - SAMPLE_ROWS), (0, 0)))
    ps = jnp.pad(p_sample[0], ((0, 0), (0, tz - Ts), (0, 0)))
    y_sample = _mix_and_ffn(z3s, o_a_s, o_n_s, xs, ps, wts, Bs * tz)[:, :Ts]

    def kv_rows(z, col, lo, hi):
        return z[:, lo:hi, col:col + 2 * NSA_KV_WIDTH].reshape((z.shape[0], hi - lo) + kvh)

    new_win_s = jnp.concatenate([state_win_kv[0][:, Ts:], kv_rows(z3s, COL_WIN, 0, Ts)], axis=1)
    return (y_prompt, y_sample,
            kv_rows(z3, COL_CMP, 0, T)[None], kv_rows(z3, COL_SEL, 0, T)[None],
            kv_rows(z3, COL_WIN, T - win_buf, T)[None], s_new_p[None],
            z3[:, T - (CONV_WIDTH - 1):, 0:qkv_w][None],
            kv_rows(z3s, COL_CMP, 0, Ts)[None], kv_rows(z3s, COL_SEL, 0, Ts)[None],
            new_win_s[None], s_new_s[None],
            z3s[:, Ts - (CONV_WIDTH - 1):Ts, 0:qkv_w][None])
```

```python
import functools

import jax
import jax.numpy as jnp
from jax import lax
from jax.experimental import pallas as pl
from jax.experimental.pallas import tpu as pltpu

F32 = jnp.float32
BF16 = jnp.bfloat16
HI = lax.Precision.HIGHEST

D_MODEL = 2048
HEAD_DIM = 128
GDN_HEADS = 16
GDN_WIDTH = 2048
CONV_WIDTH = 4
GDN_CHUNK = 64
NSA_HEADS = 16
NSA_KV_HEADS = 4
NSA_GROUP = 4
NSA_KV_WIDTH = 512
CMP_BLOCK = 32
SEL_BLOCK = 64
SEL_TOPK = 16
WINDOW = 512
PLE_DIM = 256
FFN_HIDDEN = 5632
PAGE_SIZE = 128
RMS_EPS = 1e-6
NEG_INF = -1e30
FORCE_BONUS = float(NSA_GROUP + 1)
QK_SCALE = HEAD_DIM ** -0.5
LOG2E = 1.4426950408889634
MASK_DIST = 1e32

COL_QKV = 0
COL_ZA = 6144
COL_GMA = 8192
COL_GMN = 10240
COL_QN = 12288
COL_CMP = 14336
COL_SEL = 15360
COL_WIN = 16384
MAIN_WIDTH = 17408
SMALL_WIDTH = 128
SMALL_A, SMALL_B, SMALL_GN = 0, 16, 32

VMEM_LIMIT = 56 * 1024 * 1024


def _cparams(sem):
    return pltpu.CompilerParams(dimension_semantics=sem, vmem_limit_bytes=VMEM_LIMIT)


def _bdot(a, b):
    return jnp.dot(a.astype(BF16), b.astype(BF16), preferred_element_type=F32)


def _bdot_nt(a, b):
    return lax.dot_general(a.astype(BF16), b.astype(BF16), (((1,), (1,)), ((), ())),
                           preferred_element_type=F32)


def _hdot(a, b):
    return jnp.dot(a, b, precision=HI, preferred_element_type=F32)


def _hdot_nt(a, b):
    return lax.dot_general(a, b, (((1,), (1,)), ((), ())), precision=HI, preferred_element_type=F32)


def _sigmoid(x):
    return 1.0 / (1.0 + jnp.exp(-x))


def _silu(x):
    return x * _sigmoid(x)


def _rms_rows(x):
    return lax.rsqrt(jnp.mean(x * x, axis=-1, keepdims=True) + RMS_EPS)


def _inproj_kernel(bounds, x_ref, g_ref, *refs):
    w_refs = refs[:len(bounds)]
    ws_ref, z_ref, zs_ref, h_scr = refs[len(bounds):]
    j = pl.program_id(1)

    @pl.when(j == 0)
    def _():
        x = x_ref[...]
        h = ((x * _rms_rows(x)) * g_ref[...]).astype(BF16)
        h_scr[...] = h
        zs_ref[...] = jnp.dot(h, ws_ref[...], preferred_element_type=F32)

    for (lo, hi), w_ref in zip(bounds, w_refs):
        @pl.when((j >= lo) & (j < hi))
        def _(w_ref=w_ref):
            z_ref[...] = jnp.dot(h_scr[...], w_ref[...], preferred_element_type=F32)


def _inproj(x2, g_pre1, w_slabs, w_small, tm, tn=512):
    n = x2.shape[0]
    bounds, lo = [], 0
    for w in w_slabs:
        bounds.append((lo, lo + w.shape[1] // tn))
        lo = bounds[-1][1]
    assert lo * tn == MAIN_WIDTH

    def slab_spec(lo, hi):
        return pl.BlockSpec((D_MODEL, tn), lambda i, j: (0, jnp.clip(j - lo, 0, hi - lo - 1)))

    return pl.pallas_call(
        functools.partial(_inproj_kernel, tuple(bounds)),
        grid=(n // tm, MAIN_WIDTH // tn),
        in_specs=[
            pl.BlockSpec((tm, D_MODEL), lambda i, j: (i, 0)),
            pl.BlockSpec((1, D_MODEL), lambda i, j: (0, 0)),
        ] + [slab_spec(lo, hi) for lo, hi in bounds] + [
            pl.BlockSpec((D_MODEL, SMALL_WIDTH), lambda i, j: (0, 0)),
        ],
        out_specs=[
            pl.BlockSpec((tm, tn), lambda i, j: (i, j)),
            pl.BlockSpec((tm, SMALL_WIDTH), lambda i, j: (i, 0)),
        ],
        out_shape=[jax.ShapeDtypeStruct((n, MAIN_WIDTH), F32),
                   jax.ShapeDtypeStruct((n, SMALL_WIDTH), F32)],
        scratch_shapes=[pltpu.VMEM((tm, D_MODEL), BF16)],
        compiler_params=_cparams(("parallel", "arbitrary")),
        name="inproj",
    )(x2, g_pre1.reshape(1, D_MODEL), *w_slabs, w_small)


def _level_mask(ii, jj, s):
    return ((ii // s) % 2 == 1) & (jj // s == ii // s - 1)


def _gdn_kernel(tb, nh, t_valid,
                q_ref, k_ref, v_ref, qh_ref, kh_ref, vh_ref, cpq_ref, cpk_ref, cpv_ref,
                cwq_ref, cwk_ref, cwv_ref, z_ref, zs_ref, hp_ref, nw_ref, s0_ref,
                o_ref, sn_ref, ext_scr, gt_scr, s_scr):
    C = GDN_CHUNK
    nch = tb // C
    hg = pl.program_id(1)
    t = pl.program_id(2)
    nt = pl.num_programs(2)

    @pl.when(t == 0)
    def _():
        s_scr[...] = s0_ref[...]

    def conv_silu(u_ref, halo_ref, cp_ref, cw_ref):
        prev = jnp.where(t == 0, cp_ref[...], halo_ref[...])
        ext_scr[0:8, :] = prev
        ext_scr[8:8 + tb, :] = u_ref[...]
        w = cw_ref[...]
        acc = ext_scr[5:5 + tb, :] * w[0:1, :]
        for j in range(1, CONV_WIDTH):
            acc = acc + ext_scr[5 + j:5 + j + tb, :] * w[j:j + 1, :]
        return _silu(acc)

    q_all = conv_silu(q_ref, qh_ref, cpq_ref, cwq_ref)
    k_all = conv_silu(k_ref, kh_ref, cpk_ref, cwk_ref)
    v_all = conv_silu(v_ref, vh_ref, cpv_ref, cwv_ref)

    zs = zs_ref[...]
    lane = lax.broadcasted_iota(jnp.int32, (GDN_CHUNK, SMALL_WIDTH), 1)
    xa = zs + hp_ref[1:2, :]
    softplus = jnp.maximum(xa, 0.0) + jnp.log1p(jnp.exp(-jnp.abs(xa)))
    g_all = -jnp.exp(hp_ref[0:1, :]) * softplus
    beta_all = _sigmoid(zs)
    if t_valid < tb:
        live = lax.broadcasted_iota(jnp.int32, (tb, 1), 0) < t_valid
        g_all = jnp.where(live, g_all, 0.0)
        beta_all = jnp.where(live, beta_all, 0.0)

    ii = lax.broadcasted_iota(jnp.int32, (C, C), 0)
    jj = lax.broadcasted_iota(jnp.int32, (C, C), 1)
    tril = (ii >= jj).astype(F32)
    eye = (ii == jj).astype(F32)
    gcum = []
    for c in range(nch):
        gc = _hdot(tril, g_all[c * C:(c + 1) * C])
        gcum.append(gc)
        gt_scr[c] = jnp.transpose(gc)

    Ls, rhss, qkds, kdts, qgs, egls = [], [], [], [], [], []
    for hl in range(nh):
        hglob = hg * nh + hl
        hs = slice(hl * HEAD_DIM, (hl + 1) * HEAD_DIM)
        q = q_all[:, hs]
        k = k_all[:, hs]
        q = (q * lax.rsqrt(jnp.sum(q * q, axis=-1, keepdims=True) + RMS_EPS)) * QK_SCALE
        k = k * lax.rsqrt(jnp.sum(k * k, axis=-1, keepdims=True) + RMS_EPS)
        for c in range(nch):
            sl = slice(c * C, (c + 1) * C)
            qc, kc, vc = q[sl], k[sl], v_all[sl, hs]
            bcol = jnp.sum(jnp.where(lane == SMALL_B + hglob, beta_all[sl], 0.0), axis=1, keepdims=True)
            gcol = jnp.sum(jnp.where(lane == SMALL_A + hglob, gcum[c], 0.0), axis=1, keepdims=True)
            grow = gt_scr[c, pl.ds(SMALL_A + hglob, 1), :]
            decay = jnp.exp(jnp.where(ii >= jj, gcol - grow, NEG_INF))
            qkk = _bdot_nt(jnp.concatenate([qc, kc], axis=0), kc)
            Ls.append(jnp.where(ii > jj, qkk[C:] * decay, 0.0) * bcol)
            eg = jnp.exp(gcol)
            g_last = grow[:, C - 1:C]
            rhss.append(jnp.concatenate([vc * bcol, kc * (bcol * eg)], axis=1))
            qkds.append(qkk[:C] * decay)
            kdts.append(jnp.transpose(kc * jnp.exp(g_last - gcol)))
            qgs.append(qc * eg)
            egls.append(jnp.exp(g_last))

    n_inst = nh * nch
    Xs = [eye - jnp.where(_level_mask(ii, jj, 1), L, 0.0) for L in Ls]
    s = 2
    while s < C:
        m = _level_mask(ii, jj, s)
        Ys = [_bdot(jnp.where(m, Ls[i], 0.0), Xs[i]) for i in range(n_inst)]
        Zs = [_bdot(Xs[i], Ys[i]) for i in range(n_inst)]
        Xs = [Xs[i] - Zs[i] for i in range(n_inst)]
        s *= 2
    sols = [_bdot(Xs[i], rhss[i]) for i in range(n_inst)]
    res = [rhss[i] - sols[i] - _hdot(Ls[i], sols[i]) for i in range(n_inst)]
    sols = [sols[i] + _bdot(Xs[i], res[i]) for i in range(n_inst)]
    NPs = [_bdot(kdts[i], sols[i]) for i in range(n_inst)]
    QOs = [_bdot(qkds[i], sols[i]) for i in range(n_inst)]

    Ss = [s_scr[hl] for hl in range(nh)]
    outs = [None] * n_inst
    for c in range(nch):
        for hl in range(nh):
            i = hl * nch + c
            S = Ss[hl]
            outs[i] = _bdot(qgs[i] - QOs[i][:, HEAD_DIM:], S) + QOs[i][:, :HEAD_DIM]
            Ss[hl] = S * egls[i] - _bdot(NPs[i][:, HEAD_DIM:], S) + NPs[i][:, :HEAD_DIM]
    nw = nw_ref[...]
    for hl in range(nh):
        hs = slice(hl * HEAD_DIM, (hl + 1) * HEAD_DIM)
        s_scr[hl] = Ss[hl]
        for c in range(nch):
            sl = slice(c * C, (c + 1) * C)
            o = outs[hl * nch + c]
            o_ref[sl, hs] = ((o * _rms_rows(o)) * nw) * _silu(z_ref[sl, hs])

    @pl.when(t == nt - 1)
    def _():
        sn_ref[...] = s_scr[...]


def _gdn(z3, zs3, conv_prev, conv_w, hp, norm_w, s0, tb, t_valid, nh=4):
    B, T, _ = z3.shape
    nt = T // tb
    hb = tb // 8
    wblk = nh * HEAD_DIM
    cq, ck, cv = COL_QKV // wblk, (COL_QKV + GDN_WIDTH) // wblk, (COL_QKV + 2 * GDN_WIDTH) // wblk
    hpg = GDN_HEADS // nh

    def main(col0):
        return pl.BlockSpec((None, tb, wblk), lambda b, h, t: (b, t, col0 + h))

    def halo(col0):
        return pl.BlockSpec((None, 8, wblk), lambda b, h, t: (b, jnp.maximum(t * hb - 1, 0), col0 + h))

    def cprev(col0):
        return pl.BlockSpec((None, 8, wblk), lambda b, h, t: (b, 0, col0 + h))

    def cw(col0):
        return pl.BlockSpec((CONV_WIDTH, wblk), lambda b, h, t: (0, col0 + h))

    state = pl.BlockSpec((None, nh, HEAD_DIM, HEAD_DIM), lambda b, h, t: (b, h, 0, 0))
    return pl.pallas_call(
        functools.partial(_gdn_kernel, tb, nh, t_valid),
        grid=(B, hpg, nt),
        in_specs=[
            main(cq), main(ck), main(cv), halo(cq), halo(ck), halo(cv),
            cprev(0), cprev(hpg), cprev(2 * hpg), cw(0), cw(hpg), cw(2 * hpg),
            main(COL_ZA // wblk),
            pl.BlockSpec((None, tb, SMALL_WIDTH), lambda b, h, t: (b, t, 0)),
            pl.BlockSpec((8, SMALL_WIDTH), lambda b, h, t: (0, 0)),
            pl.BlockSpec((1, HEAD_DIM), lambda b, h, t: (0, 0)),
            state,
        ],
        out_specs=[pl.BlockSpec((None, tb, wblk), lambda b, h, t: (b, t, h)), state],
        out_shape=[jax.ShapeDtypeStruct((B, T, GDN_WIDTH), F32),
                   jax.ShapeDtypeStruct((B, GDN_HEADS, HEAD_DIM, HEAD_DIM), F32)],
        scratch_shapes=[pltpu.VMEM((tb + 8, wblk), F32),
                        pltpu.VMEM((tb // GDN_CHUNK, SMALL_WIDTH, GDN_CHUNK), F32),
                        pltpu.VMEM((nh, HEAD_DIM, HEAD_DIM), F32)],
        compiler_params=_cparams(("parallel", "parallel", "arbitrary")),
        name="gdn",
    )(z3, z3, z3, z3, z3, z3, conv_prev, conv_prev, conv_prev, conv_w, conv_w, conv_w,
      z3, zs3, hp, norm_w.reshape(1, HEAD_DIM), s0)


def _compress_kernel(x_ref, w_ref, o_ref):
    rows = x_ref.shape[0]
    x = x_ref[...].reshape(rows // CMP_BLOCK, CMP_BLOCK, 2 * NSA_KV_WIDTH)
    o_ref[...] = jnp.sum(x * w_ref[...][None], axis=1)


def _compress_prompt(z3, w_cmp2, tc=256):
    B, T, _ = z3.shape
    nc = T // CMP_BLOCK
    return pl.pallas_call(
        _compress_kernel,
        grid=(B, T // tc),
        in_specs=[pl.BlockSpec((None, tc, 2 * NSA_KV_WIDTH), lambda b, i: (b, i, COL_CMP // 1024)),
                  pl.BlockSpec((CMP_BLOCK, 2 * NSA_KV_WIDTH), lambda b, i: (0, 0))],
        out_specs=pl.BlockSpec((None, tc // CMP_BLOCK, 2 * NSA_KV_WIDTH), lambda b, i: (b, i, 0)),
        out_shape=jax.ShapeDtypeStruct((B, nc, 2 * NSA_KV_WIDTH), F32),
        compiler_params=_cparams(("parallel", "parallel")),
        name="compress_prompt",
    )(z3, w_cmp2)


def _topk_mask(sc, k_sel):
    n = sc.shape[1]
    lanef = lax.broadcasted_iota(jnp.int32, sc.shape, 1).astype(F32)
    sel = jnp.zeros(sc.shape, F32)
    for _ in range(k_sel):
        m = jnp.max(sc, axis=1, keepdims=True)
        idx = jnp.min(jnp.where(sc == m, lanef, float(n)), axis=1, keepdims=True)
        hit = lanef == idx
        sel = jnp.where(hit, 1.0, sel)
        sc = jnp.where(hit, -3.0, sc)
    return sel


def _cmp_attn_kernel(ta, nc, ns, ns_real, t_base, slopes_ref, q_ref, kvc_ref, zs_ref, oc_ref, sel_ref, sc_scr):
    t0 = t_base + pl.program_id(1) * ta
    tpos = t0 + lax.broadcasted_iota(jnp.int32, (ta, nc), 0)
    cend = lax.broadcasted_iota(jnp.int32, (ta, nc), 1) * CMP_BLOCK + (CMP_BLOCK - 1)
    dist = tpos - cend
    valid = dist >= 0
    distf = dist.astype(F32)
    gates = _sigmoid(zs_ref[...])
    pc = lax.broadcasted_iota(jnp.int32, (nc, ns), 0)
    ps = lax.broadcasted_iota(jnp.int32, (nc, ns), 1)
    pool = (pc // (SEL_BLOCK // CMP_BLOCK) == ps).astype(F32)
    tq = t0 + lax.broadcasted_iota(jnp.int32, (ta, ns), 0)
    blk = lax.broadcasted_iota(jnp.int32, (ta, ns), 1)
    cur = tq // SEL_BLOCK
    forced = (blk == 0) | (blk == cur) | (blk == cur - 1)
    avail = blk * SEL_BLOCK <= tq
    for g in range(NSA_KV_HEADS):
        kc = kvc_ref[:, g * HEAD_DIM:(g + 1) * HEAD_DIM]
        vc = kvc_ref[:, NSA_KV_WIDTH + g * HEAD_DIM:NSA_KV_WIDTH + (g + 1) * HEAD_DIM]
        imp = jnp.zeros((ta, nc), F32)
        for r in range(NSA_GROUP):
            hd = g * NSA_GROUP + r
            qh = q_ref[:, hd * HEAD_DIM:(hd + 1) * HEAD_DIM] * QK_SCALE
            s = _hdot_nt(qh, kc) - slopes_ref[hd] * distf
            s = jnp.where(valid, s, NEG_INF)
            e = jnp.exp(s - jnp.max(s, axis=1, keepdims=True))
            p = jnp.where(valid, e / jnp.sum(e, axis=1, keepdims=True), 0.0)
            imp = imp + p
            gi = SMALL_GN + hd * 3
            oc_ref[:, hd * HEAD_DIM:(hd + 1) * HEAD_DIM] = gates[:, gi:gi + 1] * _bdot(p, vc)
        imps = _hdot(imp, pool)
        score = jnp.where(forced, imps + FORCE_BONUS, jnp.where(avail, imps, -1.0))
        if ns_real < ns:
            score = jnp.where(blk < ns_real, score, -2.0)
        sc_scr[g * ta:(g + 1) * ta, :] = score
    sel = _topk_mask(sc_scr[...], min(SEL_TOPK, ns_real))
    for g in range(NSA_KV_HEADS):
        sel_ref[g] = sel[g * ta:(g + 1) * ta, :]


def _cmp_attn(z3, zs3, kvc, slopes, ta, n_tok, ns, ns_real, t_base, name):
    B = z3.shape[0]
    nc = kvc.shape[1]
    return pl.pallas_call(
        functools.partial(_cmp_attn_kernel, ta, nc, ns, ns_real, t_base),
        grid_spec=pltpu.PrefetchScalarGridSpec(
            num_scalar_prefetch=1,
            grid=(B, n_tok // ta),
            in_specs=[
                pl.BlockSpec((None, ta, NSA_HEADS * HEAD_DIM), lambda b, i, sl: (b, i, COL_QN // 2048)),
                pl.BlockSpec((None, nc, 2 * NSA_KV_WIDTH), lambda b, i, sl: (b, 0, 0)),
                pl.BlockSpec((None, ta, SMALL_WIDTH), lambda b, i, sl: (b, i, 0)),
            ],
            out_specs=[
                pl.BlockSpec((None, ta, NSA_HEADS * HEAD_DIM), lambda b, i, sl: (b, i, 0)),
                pl.BlockSpec((None, NSA_KV_HEADS, ta, ns), lambda b, i, sl: (b, 0, i, 0)),
            ],
            scratch_shapes=[pltpu.VMEM((NSA_KV_HEADS * ta, ns), F32)],
        ),
        out_shape=[jax.ShapeDtypeStruct((B, n_tok, NSA_HEADS * HEAD_DIM), F32),
                   jax.ShapeDtypeStruct((B, NSA_KV_HEADS, n_tok, ns), F32)],
        compiler_params=_cparams(("parallel", "parallel")),
        name=name,
    )(slopes, z3, kvc, zs3)


def _sel_win_kernel(T, ns, tkv, wspan, slopes_ref, q_ref, ks_ref, vs_ref, kw_ref, vw_ref, sel_ref,
                    zs_ref, oc_ref, o_ref):
    QB = SEL_BLOCK
    R = NSA_GROUP
    g = pl.program_id(1)
    qb = pl.program_id(2)
    t0 = qb * QB
    q = jnp.concatenate([(q_ref[:, r * HEAD_DIM:(r + 1) * HEAD_DIM] * (QK_SCALE * LOG2E)).astype(BF16)
                         for r in range(R)], axis=0)
    slope2 = [slopes_ref[g * R + r] * LOG2E for r in range(R)]
    selb = sel_ref[...].astype(BF16)

    def head_rows(x, r):
        return x[r * QB:(r + 1) * QB]

    def spread(cols):
        return jnp.concatenate([jnp.broadcast_to(c, (QB, HEAD_DIM)) for c in cols], axis=0)

    ti = lax.broadcasted_iota(jnp.int32, (QB, tkv), 0)
    kj = lax.broadcasted_iota(jnp.int32, (QB, tkv), 1)
    d0 = (ti - kj).astype(F32)
    eb = lax.broadcasted_iota(jnp.int32, (ns, tkv), 0)
    ek = lax.broadcasted_iota(jnp.int32, (ns, tkv), 1) // SEL_BLOCK

    def sel_tile(j, carry):
        ms, ls, acc = carry
        k0 = j * tkv
        kt = ks_ref[k0:k0 + tkv, :]
        vt = vs_ref[k0:k0 + tkv, :]
        expand = (eb == ek + j * (tkv // SEL_BLOCK)).astype(BF16)
        keymask = jnp.dot(selb, expand, preferred_element_type=F32)
        distf = d0 + (t0 - k0).astype(F32)
        base = jnp.where((distf >= 0.0) & (keymask > 0.5), distf, MASK_DIST)
        s = _bdot_nt(q, kt)
        ps, ms2, ls2, alphas = [], [], [], []
        for r in range(R):
            sr = head_rows(s, r) - slope2[r] * base
            m_new = jnp.maximum(ms[r], jnp.max(sr, axis=1, keepdims=True))
            p = jnp.exp2(sr - m_new)
            alpha = jnp.exp2(ms[r] - m_new)
            ps.append(p.astype(BF16))
            ms2.append(m_new)
            ls2.append(alpha * ls[r] + jnp.sum(p, axis=1, keepdims=True))
            alphas.append(alpha)
        pv = jnp.dot(jnp.concatenate(ps, axis=0), vt.astype(BF16), preferred_element_type=F32)
        return ms2, ls2, spread(alphas) * acc + pv

    def window():
        kstart = pl.multiple_of(jnp.clip(t0 - WINDOW, 0, T - wspan), SEL_BLOCK)
        kw = kw_ref[pl.ds(kstart, wspan), :]
        vw = vw_ref[pl.ds(kstart, wspan), :]
        tw = lax.broadcasted_iota(jnp.int32, (QB, wspan), 0)
        kwj = lax.broadcasted_iota(jnp.int32, (QB, wspan), 1)
        dist = (tw - kwj) + (t0 - kstart)
        base_w = jnp.where((dist >= 0) & (dist < WINDOW), dist.astype(F32), MASK_DIST)
        s = _bdot_nt(q, kw)
        pw, lw = [], []
        for r in range(R):
            sr = head_rows(s, r) - slope2[r] * base_w
            e = jnp.exp2(sr - jnp.max(sr, axis=1, keepdims=True))
            pw.append(e.astype(BF16))
            lw.append(jnp.sum(e, axis=1, keepdims=True))
        return lw, jnp.dot(jnp.concatenate(pw, axis=0), vw.astype(BF16), preferred_element_type=F32)

    def run(n_tiles):
        carry = ([jnp.full((QB, 1), NEG_INF, F32) for _ in range(R)],
                 [jnp.zeros((QB, 1), F32) for _ in range(R)],
                 jnp.zeros((R * QB, HEAD_DIM), F32))
        lw, acc_w = window()
        for j in range(n_tiles):
            carry = sel_tile(j, carry)
        _, l_s, acc_s = carry
        gates = _sigmoid(zs_ref[...])
        lane = lax.broadcasted_iota(jnp.int32, (QB, SMALL_WIDTH), 1)
        for r in range(R):
            gi = SMALL_GN + (g * R + r) * 3
            g_s = jnp.sum(jnp.where(lane == gi + 1, gates, 0.0), axis=1, keepdims=True)
            g_w = jnp.sum(jnp.where(lane == gi + 2, gates, 0.0), axis=1, keepdims=True)
            cs = slice(r * HEAD_DIM, (r + 1) * HEAD_DIM)
            o_ref[:, cs] = (oc_ref[:, cs] + (g_s / l_s[r]) * head_rows(acc_s, r)
                            + (g_w / lw[r]) * head_rows(acc_w, r))

    need = (t0 + QB + tkv - 1) // tkv
    for n_tiles in range(1, T // tkv + 1):
        pl.when(need == n_tiles)(functools.partial(run, n_tiles))


def _sel_win_prompt(z3, zs3, selmask, oc, slopes):
    B, T, _ = z3.shape
    ns = T // SEL_BLOCK
    tkv = min(1024, T)
    wspan = min(WINDOW + 2 * SEL_BLOCK, T)
    gw = NSA_GROUP * HEAD_DIM

    def kv(col0):
        return pl.BlockSpec((None, T, HEAD_DIM), lambda b, g, i, sl: (b, 0, col0 // 128 + g))

    return pl.pallas_call(
        functools.partial(_sel_win_kernel, T, ns, tkv, wspan),
        grid_spec=pltpu.PrefetchScalarGridSpec(
            num_scalar_prefetch=1,
            grid=(B, NSA_KV_HEADS, T // SEL_BLOCK),
            in_specs=[
                pl.BlockSpec((None, SEL_BLOCK, gw), lambda b, g, i, sl: (b, i, COL_QN // gw + g)),
                kv(COL_SEL), kv(COL_SEL + NSA_KV_WIDTH), kv(COL_WIN), kv(COL_WIN + NSA_KV_WIDTH),
                pl.BlockSpec((None, None, SEL_BLOCK, ns), lambda b, g, i, sl: (b, g, i, 0)),
                pl.BlockSpec((None, SEL_BLOCK, SMALL_WIDTH), lambda b, g, i, sl: (b, i, 0)),
                pl.BlockSpec((None, SEL_BLOCK, gw), lambda b, g, i, sl: (b, i, g)),
            ],
            out_specs=pl.BlockSpec((None, SEL_BLOCK, gw), lambda b, g, i, sl: (b, i, g)),
        ),
        out_shape=jax.ShapeDtypeStruct((B, T, NSA_HEADS * HEAD_DIM), F32),
        compiler_params=_cparams(("parallel", "parallel", "arbitrary")),
        name="sel_win_prompt",
    )(slopes, z3, z3, z3, z3, z3, selmask, zs3, oc)


def _merge_kernel(gma_ref, gmn_ref, oa_ref, on_ref, x_ref, w_ref, g_ref, o_ref):
    mixed = _sigmoid(gma_ref[...]) * oa_ref[...] + _sigmoid(gmn_ref[...]) * on_ref[...]
    y = jnp.dot(mixed.astype(BF16), w_ref[...], preferred_element_type=F32)
    o_ref[...] = x_ref[...] + (y * _rms_rows(y)) * g_ref[...]


def _merge_out(z2, o_a, o_n, x2, w_o, g_post1, tm):
    n = x2.shape[0]
    row = lambda c: pl.BlockSpec((tm, D_MODEL), lambda i: (i, c))
    return pl.pallas_call(
        _merge_kernel,
        grid=(n // tm,),
        in_specs=[row(COL_GMA // D_MODEL), row(COL_GMN // D_MODEL), row(0), row(0), row(0),
                  pl.BlockSpec((D_MODEL, D_MODEL), lambda i: (0, 0)),
                  pl.BlockSpec((1, D_MODEL), lambda i: (0, 0))],
        out_specs=row(0),
        out_shape=jax.ShapeDtypeStruct((n, D_MODEL), F32),
        compiler_params=_cparams(("parallel",)),
        name="merge_out",
    )(z2, z2, o_a, o_n, x2, w_o, g_post1.reshape(1, D_MODEL))


def _ffn_kernel(x_ref, g2_ref, wg_ref, wu_ref, wo_ref, gp_ref, o_ref, h_scr, acc_scr):
    j = pl.program_id(1)

    @pl.when(j == 0)
    def _():
        x = x_ref[...]
        h_scr[...] = ((x * _rms_rows(x)) * g2_ref[...]).astype(BF16)
        acc_scr[...] = jnp.zeros_like(acc_scr)

    h = h_scr[...]
    gt = jnp.dot(h, wg_ref[...], preferred_element_type=F32)
    up = jnp.dot(h, wu_ref[...], preferred_element_type=F32)
    acc_scr[...] += jnp.dot((_silu(gt) * up).astype(BF16), wo_ref[...], preferred_element_type=F32)

    @pl.when(j == pl.num_programs(1) - 1)
    def _():
        y = acc_scr[...]
        o_ref[...] = x_ref[...] + (y * _rms_rows(y)) * gp_ref[...]


def _ffn(x2, g_pre2, w_ffn_in, w_ffn_out, g_post2, tm, th=512):
    n = x2.shape[0]
    nh = FFN_HIDDEN // th
    return pl.pallas_call(
        _ffn_kernel,
        grid=(n // tm, nh),
        in_specs=[
            pl.BlockSpec((tm, D_MODEL), lambda i, j: (i, 0)),
            pl.BlockSpec((1, D_MODEL), lambda i, j: (0, 0)),
            pl.BlockSpec((D_MODEL, th), lambda i, j: (0, j)),
            pl.BlockSpec((D_MODEL, th), lambda i, j: (0, nh + j)),
            pl.BlockSpec((th, D_MODEL), lambda i, j: (j, 0)),
            pl.BlockSpec((1, D_MODEL), lambda i, j: (0, 0)),
        ],
        out_specs=pl.BlockSpec((tm, D_MODEL), lambda i, j: (i, 0)),
        out_shape=jax.ShapeDtypeStruct((n, D_MODEL), F32),
        scratch_shapes=[pltpu.VMEM((tm, D_MODEL), BF16), pltpu.VMEM((tm, D_MODEL), F32)],
        compiler_params=_cparams(("parallel", "arbitrary")),
        name="ffn",
    )(x2, g_pre2.reshape(1, D_MODEL), w_ffn_in, w_ffn_in, w_ffn_out, g_post2.reshape(1, D_MODEL))


def _ple_kernel(x_ref, p_ref, wg_ref, wp_ref, o_ref):
    x = x_ref[...]
    gate = _sigmoid(jnp.dot(x.astype(BF16), wg_ref[...], preferred_element_type=F32))
    o_ref[...] = x + gate * jnp.dot(p_ref[...].astype(BF16), wp_ref[...], preferred_element_type=F32)


def _ple(x2, ple2, w_ple_gate, w_ple, tm):
    n = x2.shape[0]
    return pl.pallas_call(
        _ple_kernel,
        grid=(n // tm,),
        in_specs=[pl.BlockSpec((tm, D_MODEL), lambda i: (i, 0)),
                  pl.BlockSpec((tm, PLE_DIM), lambda i: (i, 0)),
                  pl.BlockSpec((D_MODEL, D_MODEL), lambda i: (0, 0)),
                  pl.BlockSpec((PLE_DIM, D_MODEL), lambda i: (0, 0))],
        out_specs=pl.BlockSpec((tm, D_MODEL), lambda i: (i, 0)),
        out_shape=jax.ShapeDtypeStruct((n, D_MODEL), F32),
        compiler_params=_cparams(("parallel",)),
        name="ple",
    )(x2, ple2, w_ple_gate, w_ple)


SAMPLE_ROWS = 8
KV_ROW = 2 * NSA_KV_HEADS


def _compress_pages_kernel(npg, pt_ref, *refs):
    w = refs[npg][...]
    o_ref = refs[npg + 1]
    per = PAGE_SIZE // CMP_BLOCK
    for p in range(npg):
        x = refs[p][...].reshape(per, CMP_BLOCK, KV_ROW, HEAD_DIM)
        o_ref[p * per:(p + 1) * per] = jnp.sum(x * w[None], axis=1)


def _compress_sample(cache_rows, page_table, w_cmp3, npg=8):
    Bs, n_pages = page_table.shape
    per = PAGE_SIZE // CMP_BLOCK

    def page(p):
        return pl.BlockSpec((None, PAGE_SIZE * KV_ROW, HEAD_DIM), lambda b, i, pt: (pt[b, i * npg + p], 0, 0))

    return pl.pallas_call(
        functools.partial(_compress_pages_kernel, npg),
        grid_spec=pltpu.PrefetchScalarGridSpec(
            num_scalar_prefetch=1,
            grid=(Bs, n_pages // npg),
            in_specs=[page(p) for p in range(npg)]
            + [pl.BlockSpec((CMP_BLOCK, KV_ROW, HEAD_DIM), lambda b, i, pt: (0, 0, 0))],
            out_specs=pl.BlockSpec((None, npg * per, KV_ROW, HEAD_DIM), lambda b, i, pt: (b, i, 0, 0)),
        ),
        out_shape=jax.ShapeDtypeStruct((Bs, n_pages * per, KV_ROW, HEAD_DIM), F32),
        compiler_params=_cparams(("parallel", "parallel")),
        name="compress_sample",
    )(page_table, *([cache_rows] * npg), w_cmp3)


def _sel_pages_kernel(npg, past_len, pt_ref, slopes_ref, q_ref, sel_ref, *refs):
    pages = refs[:npg]
    m_ref, l_ref, acc_ref, q_scr, selrows_scr, slope_scr = refs[npg:]
    i = pl.program_id(1)
    nrow = NSA_HEADS * SAMPLE_ROWS
    rg = NSA_GROUP * SAMPLE_ROWS
    ns_pad = sel_ref.shape[-1]

    @pl.when(i == 0)
    def _():
        for hd in range(NSA_HEADS):
            rs = slice(hd * SAMPLE_ROWS, (hd + 1) * SAMPLE_ROWS)
            q_scr[rs, :] = q_ref[:, hd * HEAD_DIM:(hd + 1) * HEAD_DIM] * (QK_SCALE * LOG2E)
            selrows_scr[rs, :] = sel_ref[hd // NSA_GROUP]
            slope_scr[rs, :] = jnp.full((SAMPLE_ROWS, HEAD_DIM), slopes_ref[hd] * LOG2E, F32)
        m_ref[...] = jnp.full(m_ref.shape, NEG_INF, F32)
        l_ref[...] = jnp.zeros(l_ref.shape, F32)
        acc_ref[...] = jnp.zeros(acc_ref.shape, F32)

    qb = q_scr[...].astype(BF16)
    selb = selrows_scr[...].astype(BF16)
    slope = slope_scr[...]
    row = lax.broadcasted_iota(jnp.int32, (nrow, PAGE_SIZE), 0)
    pos = lax.broadcasted_iota(jnp.int32, (nrow, PAGE_SIZE), 1)
    d0 = (past_len + row % SAMPLE_ROWS - pos).astype(F32)
    ob = lax.broadcasted_iota(jnp.int32, (ns_pad, PAGE_SIZE), 0)
    ol = lax.broadcasted_iota(jnp.int32, (ns_pad, PAGE_SIZE), 1) // SEL_BLOCK
    scores = []
    for p in range(npg):
        pg = i * npg + p
        flags = jnp.dot(selb, (ob == ol + pg * (PAGE_SIZE // SEL_BLOCK)).astype(BF16),
                        preferred_element_type=F32)
        distf = d0 - (pg * PAGE_SIZE).astype(F32)
        base = jnp.where((flags > 0.5) & (distf >= 0.0), distf, MASK_DIST)
        scores.append(jnp.concatenate(
            [_bdot_nt(qb[g * rg:(g + 1) * rg], pages[p][pl.ds(g, PAGE_SIZE, stride=KV_ROW), :])
             for g in range(NSA_KV_HEADS)], axis=0) - slope * base)
    s = jnp.concatenate(scores, axis=1)
    m_old = m_ref[...]
    m_new = jnp.maximum(m_old, jnp.max(s, axis=1, keepdims=True))
    pr = jnp.exp2(s - m_new[:, 0:1])
    alpha = jnp.exp2(m_old - m_new)
    l_ref[...] = alpha * l_ref[...] + jnp.sum(pr, axis=1, keepdims=True)
    prb = pr.astype(BF16)
    pv = jnp.zeros((nrow, HEAD_DIM), F32)
    for p in range(npg):
        ps = prb[:, p * PAGE_SIZE:(p + 1) * PAGE_SIZE]
        pv = pv + jnp.concatenate(
            [_bdot(ps[g * rg:(g + 1) * rg], pages[p][pl.ds(NSA_KV_HEADS + g, PAGE_SIZE, stride=KV_ROW), :])
             for g in range(NSA_KV_HEADS)], axis=0)
    acc_ref[...] = alpha * acc_ref[...] + pv
    m_ref[...] = m_new


def _sel_sample(z3s, selmask, cache_rows, page_table, slopes, past_len, npg=8):
    Bs, n_pages = page_table.shape
    nrow = NSA_HEADS * SAMPLE_ROWS
    ns_pad = selmask.shape[-1]

    def page(p):
        return pl.BlockSpec((None, PAGE_SIZE * KV_ROW, HEAD_DIM), lambda b, i, pt, sl: (pt[b, i * npg + p], 0, 0))

    part = pl.BlockSpec((None, nrow, HEAD_DIM), lambda b, i, pt, sl: (b, 0, 0))
    return pl.pallas_call(
        functools.partial(_sel_pages_kernel, npg, past_len),
        grid_spec=pltpu.PrefetchScalarGridSpec(
            num_scalar_prefetch=2,
            grid=(Bs, n_pages // npg),
            in_specs=[
                pl.BlockSpec((None, SAMPLE_ROWS, NSA_HEADS * HEAD_DIM), lambda b, i, pt, sl: (b, 0, COL_QN // 2048)),
                pl.BlockSpec((None, NSA_KV_HEADS, SAMPLE_ROWS, ns_pad), lambda b, i, pt, sl: (b, 0, 0, 0)),
            ] + [page(p) for p in range(npg)],
            out_specs=[part, part, part],
            scratch_shapes=[pltpu.VMEM((nrow, HEAD_DIM), F32), pltpu.VMEM((nrow, ns_pad), F32),
                            pltpu.VMEM((nrow, HEAD_DIM), F32)],
        ),
        out_shape=[jax.ShapeDtypeStruct((Bs, nrow, HEAD_DIM), F32)] * 3,
        compiler_params=_cparams(("parallel", "arbitrary")),
        name="sel_sample",
    )(page_table, slopes, z3s, selmask, *([cache_rows] * npg))


def _finish_sample_kernel(past_len, t_real, nnew, slopes_ref, q_ref, kst_ref, snew_ref, wnew_ref, sel_ref,
                          m_ref, l_ref, acc_ref, oc_ref, zs_ref, o_ref):
    nst = kst_ref.shape[0]
    cur = past_len // SEL_BLOCK
    gates = _sigmoid(zs_ref[...])
    t_new = lax.broadcasted_iota(jnp.int32, (SAMPLE_ROWS, nnew), 0)
    j_new = lax.broadcasted_iota(jnp.int32, (SAMPLE_ROWS, nnew), 1)
    dist_new = t_new - j_new
    ok_new = (dist_new >= 0) & (j_new < t_real)
    t_st = lax.broadcasted_iota(jnp.int32, (SAMPLE_ROWS, nst), 0)
    i_st = lax.broadcasted_iota(jnp.int32, (SAMPLE_ROWS, nst), 1)
    dist_st = t_st + nst - i_st
    ok_st = dist_st < WINDOW
    for hd in range(NSA_HEADS):
        g = hd // NSA_GROUP
        rs = slice(hd * SAMPLE_ROWS, (hd + 1) * SAMPLE_ROWS)
        kc = slice(g * HEAD_DIM, (g + 1) * HEAD_DIM)
        vc = slice(NSA_KV_WIDTH + g * HEAD_DIM, NSA_KV_WIDTH + (g + 1) * HEAD_DIM)
        sl = slopes_ref[hd] * LOG2E
        qh = (q_ref[:, hd * HEAD_DIM:(hd + 1) * HEAD_DIM] * (QK_SCALE * LOG2E)).astype(BF16)
        valid = ok_new & (sel_ref[g][:, cur:cur + 1] > 0.5)
        s = jnp.where(valid, _bdot_nt(qh, snew_ref[:, kc]) - sl * dist_new.astype(F32), NEG_INF)
        m_old = m_ref[rs, 0:1]
        m_new = jnp.maximum(m_old, jnp.max(s, axis=1, keepdims=True))
        pr = jnp.where(valid, jnp.exp2(s - m_new), 0.0)
        alpha = jnp.exp2(m_old - m_new)
        l = alpha * l_ref[rs, 0:1] + jnp.sum(pr, axis=1, keepdims=True)
        o_s = (alpha * acc_ref[rs, :] + _bdot(pr, snew_ref[:, vc])) / l
        s1 = jnp.where(ok_st, _bdot_nt(qh, kst_ref[:, kc]) - sl * dist_st.astype(F32), NEG_INF)
        s2 = jnp.where(ok_new, _bdot_nt(qh, wnew_ref[:, kc]) - sl * dist_new.astype(F32), NEG_INF)
        mw = jnp.maximum(jnp.max(s1, axis=1, keepdims=True), jnp.max(s2, axis=1, keepdims=True))
        e1 = jnp.where(ok_st, jnp.exp2(s1 - mw), 0.0)
        e2 = jnp.where(ok_new, jnp.exp2(s2 - mw), 0.0)
        den = jnp.sum(e1, axis=1, keepdims=True) + jnp.sum(e2, axis=1, keepdims=True)
        o_w = _bdot(e1 / den, kst_ref[:, vc]) + _bdot(e2 / den, wnew_ref[:, vc])
        gi = SMALL_GN + hd * 3
        cs = slice(hd * HEAD_DIM, (hd + 1) * HEAD_DIM)
        o_ref[:, cs] = oc_ref[:, cs] + gates[:, gi + 1:gi + 2] * o_s + gates[:, gi + 2:gi + 3] * o_w


def _finish_sample(z3s, zs3s, state_win2, selmask, m, l, acc, oc, slopes, past_len, t_real):
    Bs, tz, _ = z3s.shape
    nst = state_win2.shape[1]
    nrow = NSA_HEADS * SAMPLE_ROWS
    ns_pad = selmask.shape[-1]
    kvw = 2 * NSA_KV_WIDTH
    part = pl.BlockSpec((None, nrow, HEAD_DIM), lambda b, sl: (b, 0, 0))
    wide = pl.BlockSpec((None, SAMPLE_ROWS, NSA_HEADS * HEAD_DIM), lambda b, sl: (b, 0, 0))
    return pl.pallas_call(
        functools.partial(_finish_sample_kernel, past_len, t_real, tz),
        grid_spec=pltpu.PrefetchScalarGridSpec(
            num_scalar_prefetch=1,
            grid=(Bs,),
            in_specs=[
                pl.BlockSpec((None, SAMPLE_ROWS, NSA_HEADS * HEAD_DIM), lambda b, sl: (b, 0, COL_QN // 2048)),
                pl.BlockSpec((None, nst, kvw), lambda b, sl: (b, 0, 0)),
                pl.BlockSpec((None, tz, kvw), lambda b, sl: (b, 0, COL_SEL // kvw)),
                pl.BlockSpec((None, tz, kvw), lambda b, sl: (b, 0, COL_WIN // kvw)),
                pl.BlockSpec((None, NSA_KV_HEADS, SAMPLE_ROWS, ns_pad), lambda b, sl: (b, 0, 0, 0)),
                part, part, part, wide,
                pl.BlockSpec((None, SAMPLE_ROWS, SMALL_WIDTH), lambda b, sl: (b, 0, 0)),
            ],
            out_specs=wide,
        ),
        out_shape=jax.ShapeDtypeStruct((Bs, SAMPLE_ROWS, NSA_HEADS * HEAD_DIM), F32),
        compiler_params=_cparams(("parallel",)),
        name="finish_sample",
    )(slopes, z3s, state_win2, z3s, z3s, selmask, m, l, acc, oc, zs3s)


def _mix_and_ffn(z3, o_a, o_n, x3, ple3, wts, tm):
    B, T, _ = x3.shape
    n = B * T
    x1 = _merge_out(z3.reshape(n, MAIN_WIDTH), o_a.reshape(n, GDN_WIDTH), o_n.reshape(n, D_MODEL),
                    x3.reshape(n, D_MODEL), wts["w_o"], wts["g_post1"], min(tm, 256))
    x2 = _ffn(x1, wts["g_pre2"], wts["w_ffn_in"], wts["w_ffn_out"], wts["g_post2"], tm)
    x3o = _ple(x2, ple3.reshape(n, PLE_DIM), wts["w_ple_gate"], wts["w_ple"], min(tm, 256))
    return x3o.reshape(B, T, D_MODEL)


def kernel(x_prompt, x_sample, cache_cmp_kv, cache_sel_kv, page_table, state_win_kv, state_gdn, state_conv, p_prompt, p_sample, g_pre1, w_in, conv_w, A_log, dt_bias, gdn_norm_w, w_cmp, w_o, g_post1, g_pre2, w_ffn_in, w_ffn_out, g_post2, w_ple, w_ple_gate):
    B, T, _ = x_prompt.shape
    Bs, Ts, _ = x_sample.shape
    n_pages = page_table.shape[1]
    past_len = n_pages * PAGE_SIZE
    win_buf = state_win_kv.shape[2]
    kvh = (2, NSA_KV_HEADS, HEAD_DIM)
    qkv_w = 3 * GDN_WIDTH

    wi = w_in[0]
    w_main = (wi[:, 0:8192].astype(BF16), wi[:, 13392:17488].astype(BF16), wi[:, 8224:13344].astype(BF16))
    w_small = jnp.concatenate([wi[:, 8192:8224], wi[:, 13344:13392],
                               jnp.zeros((D_MODEL, SMALL_WIDTH - 80), F32)], axis=1).astype(BF16)
    wts = dict(w_o=w_o[0].astype(BF16), g_post1=g_post1[0], g_pre2=g_pre2[0],
               w_ffn_in=w_ffn_in[0].astype(BF16), w_ffn_out=w_ffn_out[0].astype(BF16), g_post2=g_post2[0],
               w_ple=w_ple[0].astype(BF16), w_ple_gate=w_ple_gate[0].astype(BF16))
    hp = jnp.zeros((8, SMALL_WIDTH), F32).at[0, 0:GDN_HEADS].set(A_log[0]).at[1, 0:GDN_HEADS].set(dt_bias[0])
    w_cmp2 = w_cmp[0].reshape(CMP_BLOCK, 2 * NSA_KV_WIDTH)
    heads = jnp.arange(1, NSA_HEADS + 1, dtype=F32)
    slopes = jnp.exp2(-8.0 * heads / NSA_HEADS)

    z2, zs2 = _inproj(x_prompt.reshape(B * T, D_MODEL), g_pre1[0], w_main, w_small, 1024)
    z3, zs3 = z2.reshape(B, T, MAIN_WIDTH), zs2.reshape(B, T, SMALL_WIDTH)
    o_a, s_new_p = _gdn(z3, zs3, jnp.zeros((B, 8, qkv_w), F32), conv_w[0], hp, gdn_norm_w[0],
                        jnp.zeros((B, GDN_HEADS, HEAD_DIM, HEAD_DIM), F32), 256, 256)
    kvc = _compress_prompt(z3, w_cmp2)
    ns = T // SEL_BLOCK
    oc, selmask = _cmp_attn(z3, zs3, kvc, slopes, 512, T, ns, ns, 0, "cmp_attn_prompt")
    o_n = _sel_win_prompt(z3, zs3, selmask, oc, slopes)
    y_prompt = _mix_and_ffn(z3, o_a, o_n, x_prompt, p_prompt[0], wts, 512)

    tz = GDN_CHUNK
    xs = jnp.pad(x_sample, ((0, 0), (0, tz - Ts), (0, 0)))
    zs2_, zss2 = _inproj(xs.reshape(Bs * tz, D_MODEL), g_pre1[0], w_main, w_small, Bs * tz)
    z3s, zs3s = zs2_.reshape(Bs, tz, MAIN_WIDTH), zss2.reshape(Bs, tz, SMALL_WIDTH)
    conv_prev = jnp.pad(state_conv[0], ((0, 0), (8 - (CONV_WIDTH - 1), 0), (0, 0)))
    o_a_s, s_new_s = _gdn(z3s, zs3s, conv_prev, conv_w[0], hp, gdn_norm_w[0], state_gdn[0], tz, Ts, nh=GDN_HEADS)
    n_pool = cache_cmp_kv.shape[1]
    kvc_s = _compress_sample(cache_cmp_kv[0].reshape(n_pool, PAGE_SIZE * KV_ROW, HEAD_DIM), page_table,
                             w_cmp[0].reshape(CMP_BLOCK, KV_ROW, HEAD_DIM))
    kvc_s = kvc_s.reshape(Bs, kvc_s.shape[1], 2 * NSA_KV_WIDTH)
    ns_real = -(-(past_len + Ts) // SEL_BLOCK)
    ns_pad = -(-ns_real // 128) * 128
    oc_s, selmask_s = _cmp_attn(z3s, zs3s, kvc_s, slopes, SAMPLE_ROWS, SAMPLE_ROWS, ns_pad, ns_real, past_len,
                                "cmp_attn_sample")
    m_s, l_s, acc_s = _sel_sample(z3s, selmask_s, cache_sel_kv[0].reshape(n_pool, PAGE_SIZE * KV_ROW, HEAD_DIM),
                                  page_table, slopes, past_len)
    state_win2 = state_win_kv[0].reshape(Bs, win_buf, 2 * NSA_KV_WIDTH)
    o_n_s = _finish_sample(z3s, zs3s, state_win2, selmask_s, m_s, l_s, acc_s, oc_s, slopes, past_len, Ts)
    o_n_s = jnp.pad(o_n_s, ((0, 0), (0, tz - SAMPLE_ROWS), (0, 0)))
    ps = jnp.pad(p_sample[0], ((0, 0), (0, tz - Ts), (0, 0)))
    y_sample = _mix_and_ffn(z3s, o_a_s, o_n_s, xs, ps, wts, Bs * tz)[:, :Ts]

    def kv_rows(z, col, lo, hi):
        return z[:, lo:hi, col:col + 2 * NSA_KV_WIDTH].reshape((z.shape[0], hi - lo) + kvh)

    new_win_s = jnp.concatenate([state_win_kv[0][:, Ts:], kv_rows(z3s, COL_WIN, 0, Ts)], axis=1)
    return (y_prompt, y_sample,
            kv_rows(z3, COL_CMP, 0, T)[None], kv_rows(z3, COL_SEL, 0, T)[None],
            kv_rows(z3, COL_WIN, T - win_buf, T)[None], s_new_p[None],
            z3[:, T - (CONV_WIDTH - 1):, 0:qkv_w][None],
            kv_rows(z3s, COL_CMP, 0, Ts)[None], kv_rows(z3s, COL_SEL, 0, Ts)[None],
            new_win_s[None], s_new_s[None],
            z3s[:, Ts - (CONV_WIDTH - 1):Ts, 0:qkv_w][None])
```

```python
import functools

import jax
import jax.numpy as jnp
from jax import lax
from jax.experimental import pallas as pl
from jax.experimental.pallas import tpu as pltpu

F32 = jnp.float32
BF16 = jnp.bfloat16
HI = lax.Precision.HIGHEST

D_MODEL = 2048
HEAD_DIM = 128
GDN_HEADS = 16
GDN_WIDTH = 2048
CONV_WIDTH = 4
GDN_CHUNK = 64
NSA_HEADS = 16
NSA_KV_HEADS = 4
NSA_GROUP = 4
NSA_KV_WIDTH = 512
CMP_BLOCK = 32
SEL_BLOCK = 64
SEL_TOPK = 16
WINDOW = 512
PLE_DIM = 256
FFN_HIDDEN = 5632
PAGE_SIZE = 128
RMS_EPS = 1e-6
NEG_INF = -1e30
FORCE_BONUS = float(NSA_GROUP + 1)
QK_SCALE = HEAD_DIM ** -0.5
LOG2E = 1.4426950408889634
MASK_DIST = 1e32

COL_QKV = 0
COL_ZA = 6144
COL_GMA = 8192
COL_GMN = 10240
COL_QN = 12288
COL_CMP = 14336
COL_SEL = 15360
COL_WIN = 16384
MAIN_WIDTH = 17408
SMALL_WIDTH = 128
SMALL_A, SMALL_B, SMALL_GN = 0, 16, 32

VMEM_LIMIT = 56 * 1024 * 1024


def _cparams(sem):
    return pltpu.CompilerParams(dimension_semantics=sem, vmem_limit_bytes=VMEM_LIMIT)


def _bdot(a, b):
    return jnp.dot(a.astype(BF16), b.astype(BF16), preferred_element_type=F32)


def _bdot_nt(a, b):
    return lax.dot_general(a.astype(BF16), b.astype(BF16), (((1,), (1,)), ((), ())),
                           preferred_element_type=F32)


def _hdot(a, b):
    return jnp.dot(a, b, precision=HI, preferred_element_type=F32)


def _hdot_nt(a, b):
    return lax.dot_general(a, b, (((1,), (1,)), ((), ())), precision=HI, preferred_element_type=F32)


def _sigmoid(x):
    return 1.0 / (1.0 + jnp.exp(-x))


def _silu(x):
    return x * _sigmoid(x)


def _rms_rows(x):
    return lax.rsqrt(jnp.mean(x * x, axis=-1, keepdims=True) + RMS_EPS)


def _inproj_kernel(bounds, x_ref, g_ref, *refs):
    w_refs = refs[:len(bounds)]
    ws_ref, z_ref, zs_ref, h_scr = refs[len(bounds):]
    j = pl.program_id(1)

    @pl.when(j == 0)
    def _():
        x = x_ref[...]
        h = ((x * _rms_rows(x)) * g_ref[...]).astype(BF16)
        h_scr[...] = h
        zs_ref[...] = jnp.dot(h, ws_ref[...], preferred_element_type=F32)

    for (lo, hi), w_ref in zip(bounds, w_refs):
        @pl.when((j >= lo) & (j < hi))
        def _(w_ref=w_ref):
            z_ref[...] = jnp.dot(h_scr[...], w_ref[...], preferred_element_type=F32)


def _inproj(x2, g_pre1, w_slabs, w_small, tm, tn=512):
    n = x2.shape[0]
    bounds, lo = [], 0
    for w in w_slabs:
        bounds.append((lo, lo + w.shape[1] // tn))
        lo = bounds[-1][1]
    assert lo * tn == MAIN_WIDTH

    def slab_spec(lo, hi):
        return pl.BlockSpec((D_MODEL, tn), lambda i, j: (0, jnp.clip(j - lo, 0, hi - lo - 1)))

    return pl.pallas_call(
        functools.partial(_inproj_kernel, tuple(bounds)),
        grid=(n // tm, MAIN_WIDTH // tn),
        in_specs=[
            pl.BlockSpec((tm, D_MODEL), lambda i, j: (i, 0)),
            pl.BlockSpec((1, D_MODEL), lambda i, j: (0, 0)),
        ] + [slab_spec(lo, hi) for lo, hi in bounds] + [
            pl.BlockSpec((D_MODEL, SMALL_WIDTH), lambda i, j: (0, 0)),
        ],
        out_specs=[
            pl.BlockSpec((tm, tn), lambda i, j: (i, j)),
            pl.BlockSpec((tm, SMALL_WIDTH), lambda i, j: (i, 0)),
        ],
        out_shape=[jax.ShapeDtypeStruct((n, MAIN_WIDTH), F32),
                   jax.ShapeDtypeStruct((n, SMALL_WIDTH), F32)],
        scratch_shapes=[pltpu.VMEM((tm, D_MODEL), BF16)],
        compiler_params=_cparams(("parallel", "arbitrary")),
        name="inproj",
    )(x2, g_pre1.reshape(1, D_MODEL), *w_slabs, w_small)


def _level_mask(ii, jj, s):
    return ((ii // s) % 2 == 1) & (jj // s == ii // s - 1)


def _gdn_kernel(tb, nh, t_valid,
                q_ref, k_ref, v_ref, qh_ref, kh_ref, vh_ref, cpq_ref, cpk_ref, cpv_ref,
                cwq_ref, cwk_ref, cwv_ref, z_ref, zs_ref, hp_ref, nw_ref, s0_ref,
                o_ref, sn_ref, ext_scr, gt_scr, s_scr):
    C = GDN_CHUNK
    nch = tb // C
    hg = pl.program_id(1)
    t = pl.program_id(2)
    nt = pl.num_programs(2)

    @pl.when(t == 0)
    def _():
        s_scr[...] = s0_ref[...]

    def conv_silu(u_ref, halo_ref, cp_ref, cw_ref):
        prev = jnp.where(t == 0, cp_ref[...], halo_ref[...])
        ext_scr[0:8, :] = prev
        ext_scr[8:8 + tb, :] = u_ref[...]
        w = cw_ref[...]
        acc = ext_scr[5:5 + tb, :] * w[0:1, :]
        for j in range(1, CONV_WIDTH):
            acc = acc + ext_scr[5 + j:5 + j + tb, :] * w[j:j + 1, :]
        return _silu(acc)

    q_all = conv_silu(q_ref, qh_ref, cpq_ref, cwq_ref)
    k_all = conv_silu(k_ref, kh_ref, cpk_ref, cwk_ref)
    v_all = conv_silu(v_ref, vh_ref, cpv_ref, cwv_ref)

    zs = zs_ref[...]
    lane = lax.broadcasted_iota(jnp.int32, (GDN_CHUNK, SMALL_WIDTH), 1)
    xa = zs + hp_ref[1:2, :]
    softplus = jnp.maximum(xa, 0.0) + jnp.log1p(jnp.exp(-jnp.abs(xa)))
    g_all = -jnp.exp(hp_ref[0:1, :]) * softplus
    beta_all = _sigmoid(zs)
    if t_valid < tb:
        live = lax.broadcasted_iota(jnp.int32, (tb, 1), 0) < t_valid
        g_all = jnp.where(live, g_all, 0.0)
        beta_all = jnp.where(live, beta_all, 0.0)

    ii = lax.broadcasted_iota(jnp.int32, (C, C), 0)
    jj = lax.broadcasted_iota(jnp.int32, (C, C), 1)
    tril = (ii >= jj).astype(F32)
    eye = (ii == jj).astype(F32)
    gcum = []
    for c in range(nch):
        gc = _hdot(tril, g_all[c * C:(c + 1) * C])
        gcum.append(gc)
        gt_scr[c] = jnp.transpose(gc)

    Ls, rhss, qkds, kdts, qgs, egls = [], [], [], [], [], []
    for hl in range(nh):
        hglob = hg * nh + hl
        hs = slice(hl * HEAD_DIM, (hl + 1) * HEAD_DIM)
        q = q_all[:, hs]
        k = k_all[:, hs]
        q = (q * lax.rsqrt(jnp.sum(q * q, axis=-1, keepdims=True) + RMS_EPS)) * QK_SCALE
        k = k * lax.rsqrt(jnp.sum(k * k, axis=-1, keepdims=True) + RMS_EPS)
        for c in range(nch):
            sl = slice(c * C, (c + 1) * C)
            qc, kc, vc = q[sl], k[sl], v_all[sl, hs]
            bcol = jnp.sum(jnp.where(lane == SMALL_B + hglob, beta_all[sl], 0.0), axis=1, keepdims=True)
            gcol = jnp.sum(jnp.where(lane == SMALL_A + hglob, gcum[c], 0.0), axis=1, keepdims=True)
            grow = gt_scr[c, pl.ds(SMALL_A + hglob, 1), :]
            decay = jnp.exp(jnp.where(ii >= jj, gcol - grow, NEG_INF))
            qkk = _bdot_nt(jnp.concatenate([qc, kc], axis=0), kc)
            Ls.append(jnp.where(ii > jj, qkk[C:] * decay, 0.0) * bcol)
            eg = jnp.exp(gcol)
            g_last = grow[:, C - 1:C]
            rhss.append(jnp.concatenate([vc * bcol, kc * (bcol * eg)], axis=1))
            qkds.append(qkk[:C] * decay)
            kdts.append(jnp.transpose(kc * jnp.exp(g_last - gcol)))
            qgs.append(qc * eg)
            egls.append(jnp.exp(g_last))

    n_inst = nh * nch
    Xs = [eye - jnp.where(_level_mask(ii, jj, 1), L, 0.0) for L in Ls]
    s = 2
    while s < C:
        m = _level_mask(ii, jj, s)
        Ys = [_bdot(jnp.where(m, Ls[i], 0.0), Xs[i]) for i in range(n_inst)]
        Zs = [_bdot(Xs[i], Ys[i]) for i in range(n_inst)]
        Xs = [Xs[i] - Zs[i] for i in range(n_inst)]
        s *= 2
    sols = [_bdot(Xs[i], rhss[i]) for i in range(n_inst)]
    res = [rhss[i] - sols[i] - _hdot(Ls[i], sols[i]) for i in range(n_inst)]
    sols = [sols[i] + _bdot(Xs[i], res[i]) for i in range(n_inst)]
    NPs = [_bdot(kdts[i], sols[i]) for i in range(n_inst)]
    QOs = [_bdot(qkds[i], sols[i]) for i in range(n_inst)]

    Ss = [s_scr[hl] for hl in range(nh)]
    outs = [None] * n_inst
    for c in range(nch):
        for hl in range(nh):
            i = hl * nch + c
            S = Ss[hl]
            outs[i] = _bdot(qgs[i] - QOs[i][:, HEAD_DIM:], S) + QOs[i][:, :HEAD_DIM]
            Ss[hl] = S * egls[i] - _bdot(NPs[i][:, HEAD_DIM:], S) + NPs[i][:, :HEAD_DIM]
    nw = nw_ref[...]
    for hl in range(nh):
        hs = slice(hl * HEAD_DIM, (hl + 1) * HEAD_DIM)
        s_scr[hl] = Ss[hl]
        for c in range(nch):
            sl = slice(c * C, (c + 1) * C)
            o = outs[hl * nch + c]
            o_ref[sl, hs] = ((o * _rms_rows(o)) * nw) * _silu(z_ref[sl, hs])

    @pl.when(t == nt - 1)
    def _():
        sn_ref[...] = s_scr[...]


def _gdn(z3, zs3, conv_prev, conv_w, hp, norm_w, s0, tb, t_valid, nh=4):
    B, T, _ = z3.shape
    nt = T // tb
    hb = tb // 8
    wblk = nh * HEAD_DIM
    cq, ck, cv = COL_QKV // wblk, (COL_QKV + GDN_WIDTH) // wblk, (COL_QKV + 2 * GDN_WIDTH) // wblk
    hpg = GDN_HEADS // nh

    def main(col0):
        return pl.BlockSpec((None, tb, wblk), lambda b, h, t: (b, t, col0 + h))

    def halo(col0):
        return pl.BlockSpec((None, 8, wblk), lambda b, h, t: (b, jnp.maximum(t * hb - 1, 0), col0 + h))

    def cprev(col0):
        return pl.BlockSpec((None, 8, wblk), lambda b, h, t: (b, 0, col0 + h))

    def cw(col0):
        return pl.BlockSpec((CONV_WIDTH, wblk), lambda b, h, t: (0, col0 + h))

    state = pl.BlockSpec((None, nh, HEAD_DIM, HEAD_DIM), lambda b, h, t: (b, h, 0, 0))
    return pl.pallas_call(
        functools.partial(_gdn_kernel, tb, nh, t_valid),
        grid=(B, hpg, nt),
        in_specs=[
            main(cq), main(ck), main(cv), halo(cq), halo(ck), halo(cv),
            cprev(0), cprev(hpg), cprev(2 * hpg), cw(0), cw(hpg), cw(2 * hpg),
            main(COL_ZA // wblk),
            pl.BlockSpec((None, tb, SMALL_WIDTH), lambda b, h, t: (b, t, 0)),
            pl.BlockSpec((8, SMALL_WIDTH), lambda b, h, t: (0, 0)),
            pl.BlockSpec((1, HEAD_DIM), lambda b, h, t: (0, 0)),
            state,
        ],
        out_specs=[pl.BlockSpec((None, tb, wblk), lambda b, h, t: (b, t, h)), state],
        out_shape=[jax.ShapeDtypeStruct((B, T, GDN_WIDTH), F32),
                   jax.ShapeDtypeStruct((B, GDN_HEADS, HEAD_DIM, HEAD_DIM), F32)],
        scratch_shapes=[pltpu.VMEM((tb + 8, wblk), F32),
                        pltpu.VMEM((tb // GDN_CHUNK, SMALL_WIDTH, GDN_CHUNK), F32),
                        pltpu.VMEM((nh, HEAD_DIM, HEAD_DIM), F32)],
        compiler_params=_cparams(("parallel", "parallel", "arbitrary")),
        name="gdn",
    )(z3, z3, z3, z3, z3, z3, conv_prev, conv_prev, conv_prev, conv_w, conv_w, conv_w,
      z3, zs3, hp, norm_w.reshape(1, HEAD_DIM), s0)


def _compress_kernel(x_ref, w_ref, o_ref):
    rows = x_ref.shape[0]
    x = x_ref[...].reshape(rows // CMP_BLOCK, CMP_BLOCK, 2 * NSA_KV_WIDTH)
    o_ref[...] = jnp.sum(x * w_ref[...][None], axis=1)


def _compress_prompt(z3, w_cmp2, tc=256):
    B, T, _ = z3.shape
    nc = T // CMP_BLOCK
    return pl.pallas_call(
        _compress_kernel,
        grid=(B, T // tc),
        in_specs=[pl.BlockSpec((None, tc, 2 * NSA_KV_WIDTH), lambda b, i: (b, i, COL_CMP // 1024)),
                  pl.BlockSpec((CMP_BLOCK, 2 * NSA_KV_WIDTH), lambda b, i: (0, 0))],
        out_specs=pl.BlockSpec((None, tc // CMP_BLOCK, 2 * NSA_KV_WIDTH), lambda b, i: (b, i, 0)),
        out_shape=jax.ShapeDtypeStruct((B, nc, 2 * NSA_KV_WIDTH), F32),
        compiler_params=_cparams(("parallel", "parallel")),
        name="compress_prompt",
    )(z3, w_cmp2)


def _topk_mask(sc, k_sel):
    lane = lax.broadcasted_iota(jnp.int32, sc.shape, 1)
    sel = jnp.zeros(sc.shape, F32)
    for _ in range(k_sel):
        hit = lane == jnp.argmax(sc, axis=1, keepdims=True)
        sel = jnp.where(hit, 1.0, sel)
        sc = jnp.where(hit, -3.0, sc)
    return sel


def _cmp_attn_kernel(ta, nc, ns, ns_real, t_base, slopes_ref, q_ref, kvc_ref, zs_ref, oc_ref, sel_ref, sc_scr):
    t0 = t_base + pl.program_id(1) * ta
    tpos = t0 + lax.broadcasted_iota(jnp.int32, (ta, nc), 0)
    cend = lax.broadcasted_iota(jnp.int32, (ta, nc), 1) * CMP_BLOCK + (CMP_BLOCK - 1)
    dist = tpos - cend
    valid = dist >= 0
    distf = dist.astype(F32)
    gates = _sigmoid(zs_ref[...])
    pc = lax.broadcasted_iota(jnp.int32, (nc, ns), 0)
    ps = lax.broadcasted_iota(jnp.int32, (nc, ns), 1)
    pool = (pc // (SEL_BLOCK // CMP_BLOCK) == ps).astype(F32)
    tq = t0 + lax.broadcasted_iota(jnp.int32, (ta, ns), 0)
    blk = lax.broadcasted_iota(jnp.int32, (ta, ns), 1)
    cur = tq // SEL_BLOCK
    forced = (blk == 0) | (blk == cur) | (blk == cur - 1)
    avail = blk * SEL_BLOCK <= tq
    for g in range(NSA_KV_HEADS):
        kc = kvc_ref[:, g * HEAD_DIM:(g + 1) * HEAD_DIM]
        vc = kvc_ref[:, NSA_KV_WIDTH + g * HEAD_DIM:NSA_KV_WIDTH + (g + 1) * HEAD_DIM]
        imp = jnp.zeros((ta, nc), F32)
        for r in range(NSA_GROUP):
            hd = g * NSA_GROUP + r
            qh = q_ref[:, hd * HEAD_DIM:(hd + 1) * HEAD_DIM] * QK_SCALE
            s = _hdot_nt(qh, kc) - slopes_ref[hd] * distf
            s = jnp.where(valid, s, NEG_INF)
            e = jnp.exp(s - jnp.max(s, axis=1, keepdims=True))
            p = jnp.where(valid, e / jnp.sum(e, axis=1, keepdims=True), 0.0)
            imp = imp + p
            gi = SMALL_GN + hd * 3
            oc_ref[:, hd * HEAD_DIM:(hd + 1) * HEAD_DIM] = gates[:, gi:gi + 1] * _bdot(p, vc)
        imps = _hdot(imp, pool)
        score = jnp.where(forced, imps + FORCE_BONUS, jnp.where(avail, imps, -1.0))
        if ns_real < ns:
            score = jnp.where(blk < ns_real, score, -2.0)
        sc_scr[g * ta:(g + 1) * ta, :] = score
    sel = _topk_mask(sc_scr[...], min(SEL_TOPK, ns_real))
    for g in range(NSA_KV_HEADS):
        sel_ref[g] = sel[g * ta:(g + 1) * ta, :]


def _cmp_attn(z3, zs3, kvc, slopes, ta, n_tok, ns, ns_real, t_base, name):
    B = z3.shape[0]
    nc = kvc.shape[1]
    return pl.pallas_call(
        functools.partial(_cmp_attn_kernel, ta, nc, ns, ns_real, t_base),
        grid_spec=pltpu.PrefetchScalarGridSpec(
            num_scalar_prefetch=1,
            grid=(B, n_tok // ta),
            in_specs=[
                pl.BlockSpec((None, ta, NSA_HEADS * HEAD_DIM), lambda b, i, sl: (b, i, COL_QN // 2048)),
                pl.BlockSpec((None, nc, 2 * NSA_KV_WIDTH), lambda b, i, sl: (b, 0, 0)),
                pl.BlockSpec((None, ta, SMALL_WIDTH), lambda b, i, sl: (b, i, 0)),
            ],
            out_specs=[
                pl.BlockSpec((None, ta, NSA_HEADS * HEAD_DIM), lambda b, i, sl: (b, i, 0)),
                pl.BlockSpec((None, NSA_KV_HEADS, ta, ns), lambda b, i, sl: (b, 0, i, 0)),
            ],
            scratch_shapes=[pltpu.VMEM((NSA_KV_HEADS * ta, ns), F32)],
        ),
        out_shape=[jax.ShapeDtypeStruct((B, n_tok, NSA_HEADS * HEAD_DIM), F32),
                   jax.ShapeDtypeStruct((B, NSA_KV_HEADS, n_tok, ns), F32)],
        compiler_params=_cparams(("parallel", "parallel")),
        name=name,
    )(slopes, z3, kvc, zs3)


def _sel_win_kernel(T, QB, ns, tkv, wspan, slopes_ref, q_ref, ks_ref, vs_ref, kw_ref, vw_ref, sel_ref,
                    zs_ref, oc_ref, o_ref):
    R = NSA_GROUP
    g = pl.program_id(1)
    qb = pl.program_id(2)
    t0 = qb * QB
    q = jnp.concatenate([(q_ref[:, r * HEAD_DIM:(r + 1) * HEAD_DIM] * (QK_SCALE * LOG2E)).astype(BF16)
                         for r in range(R)], axis=0)
    slope2 = [slopes_ref[g * R + r] * LOG2E for r in range(R)]
    selb = sel_ref[...].astype(BF16)

    def head_rows(x, r):
        return x[r * QB:(r + 1) * QB]

    def spread(cols):
        return jnp.concatenate([jnp.broadcast_to(c, (QB, 2 * HEAD_DIM)) for c in cols], axis=0)

    ti = lax.broadcasted_iota(jnp.int32, (QB, tkv), 0)
    kj = lax.broadcasted_iota(jnp.int32, (QB, tkv), 1)
    d0 = (ti - kj).astype(F32)
    eb = lax.broadcasted_iota(jnp.int32, (ns, tkv), 0)
    ek = lax.broadcasted_iota(jnp.int32, (ns, tkv), 1) // SEL_BLOCK

    def sel_scores(j):
        k0 = j * tkv
        expand = (eb == ek + j * (tkv // SEL_BLOCK)).astype(BF16)
        keymask = jnp.dot(selb, expand, preferred_element_type=F32)
        distf = d0 + (t0 - k0).astype(F32)
        base = jnp.where((distf >= 0.0) & (keymask > 0.5), distf, MASK_DIST)
        return _bdot_nt(q, ks_ref[k0:k0 + tkv, :]), base

    def with_ones(v):
        return jnp.concatenate([v.astype(BF16), jnp.ones(v.shape, BF16)], axis=1)

    def sel_tile(j, s, base, carry):
        ms, acc = carry
        k0 = j * tkv
        ps, ms2, alphas = [], [], []
        for r in range(R):
            sr = head_rows(s, r) - slope2[r] * base
            m_new = jnp.maximum(ms[r], jnp.max(sr, axis=1, keepdims=True))
            ps.append(jnp.exp2((sr - m_new).astype(BF16)))
            alphas.append(jnp.exp2(ms[r] - m_new))
            ms2.append(m_new)
        pv = jnp.dot(jnp.concatenate(ps, axis=0), with_ones(vs_ref[k0:k0 + tkv, :]),
                     preferred_element_type=F32)
        return ms2, spread(alphas) * acc + pv

    def window():
        kstart = pl.multiple_of(jnp.clip(t0 - WINDOW, 0, T - wspan), QB)
        kw = kw_ref[pl.ds(kstart, wspan), :]
        vw = vw_ref[pl.ds(kstart, wspan), :]
        tw = lax.broadcasted_iota(jnp.int32, (QB, wspan), 0)
        kwj = lax.broadcasted_iota(jnp.int32, (QB, wspan), 1)
        dist = (tw - kwj) + (t0 - kstart)
        base_w = jnp.where((dist >= 0) & (dist < WINDOW), dist.astype(F32), MASK_DIST)
        s = _bdot_nt(q, kw)
        pw = []
        for r in range(R):
            sr = head_rows(s, r) - slope2[r] * base_w
            pw.append(jnp.exp2((sr - jnp.max(sr, axis=1, keepdims=True)).astype(BF16)))
        return jnp.dot(jnp.concatenate(pw, axis=0), with_ones(vw), preferred_element_type=F32)

    def run(n_tiles):
        carry = ([jnp.full((QB, 1), NEG_INF, F32) for _ in range(R)],
                 jnp.zeros((R * QB, 2 * HEAD_DIM), F32))
        nxt = sel_scores(0)
        acc_w = window()
        for j in range(n_tiles):
            cur, nxt = nxt, (sel_scores(j + 1) if j + 1 < n_tiles else None)
            carry = sel_tile(j, cur[0], cur[1], carry)
        acc_s = carry[1]
        o_s = acc_s[:, :HEAD_DIM] / acc_s[:, HEAD_DIM:]
        o_w = acc_w[:, :HEAD_DIM] / acc_w[:, HEAD_DIM:]
        gates = _sigmoid(zs_ref[...])
        lane = lax.broadcasted_iota(jnp.int32, (QB, SMALL_WIDTH), 1)
        for r in range(R):
            gi = SMALL_GN + (g * R + r) * 3
            g_s = jnp.sum(jnp.where(lane == gi + 1, gates, 0.0), axis=1, keepdims=True)
            g_w = jnp.sum(jnp.where(lane == gi + 2, gates, 0.0), axis=1, keepdims=True)
            cs = slice(r * HEAD_DIM, (r + 1) * HEAD_DIM)
            o_ref[:, cs] = oc_ref[:, cs] + g_s * head_rows(o_s, r) + g_w * head_rows(o_w, r)

    need = (t0 + QB + tkv - 1) // tkv
    for n_tiles in range(1, T // tkv + 1):
        pl.when(need == n_tiles)(functools.partial(run, n_tiles))


def _sel_win_prompt(z3, zs3, selmask, oc, slopes):
    B, T, _ = z3.shape
    ns = T // SEL_BLOCK
    tkv = min(1024, T)
    qb = 128
    wspan = min(WINDOW + 2 * qb, T)
    gw = NSA_GROUP * HEAD_DIM

    def kv(col0):
        return pl.BlockSpec((None, T, HEAD_DIM), lambda b, g, i, sl: (b, 0, col0 // 128 + g))

    return pl.pallas_call(
        functools.partial(_sel_win_kernel, T, qb, ns, tkv, wspan),
        grid_spec=pltpu.PrefetchScalarGridSpec(
            num_scalar_prefetch=1,
            grid=(B, NSA_KV_HEADS, T // qb),
            in_specs=[
                pl.BlockSpec((None, qb, gw), lambda b, g, i, sl: (b, i, COL_QN // gw + g)),
                kv(COL_SEL), kv(COL_SEL + NSA_KV_WIDTH), kv(COL_WIN), kv(COL_WIN + NSA_KV_WIDTH),
                pl.BlockSpec((None, None, qb, ns), lambda b, g, i, sl: (b, g, i, 0)),
                pl.BlockSpec((None, qb, SMALL_WIDTH), lambda b, g, i, sl: (b, i, 0)),
                pl.BlockSpec((None, qb, gw), lambda b, g, i, sl: (b, i, g)),
            ],
            out_specs=pl.BlockSpec((None, qb, gw), lambda b, g, i, sl: (b, i, g)),
        ),
        out_shape=jax.ShapeDtypeStruct((B, T, NSA_HEADS * HEAD_DIM), F32),
        compiler_params=_cparams(("parallel", "parallel", "arbitrary")),
        name="sel_win_prompt",
    )(slopes, z3, z3, z3, z3, z3, selmask, zs3, oc)


def _merge_kernel(gma_ref, gmn_ref, oa_ref, on_ref, x_ref, w_ref, g_ref, o_ref):
    mixed = _sigmoid(gma_ref[...]) * oa_ref[...] + _sigmoid(gmn_ref[...]) * on_ref[...]
    y = jnp.dot(mixed.astype(BF16), w_ref[...], preferred_element_type=F32)
    o_ref[...] = x_ref[...] + (y * _rms_rows(y)) * g_ref[...]


def _merge_out(z2, o_a, o_n, x2, w_o, g_post1, tm):
    n = x2.shape[0]
    row = lambda c: pl.BlockSpec((tm, D_MODEL), lambda i: (i, c))
    return pl.pallas_call(
        _merge_kernel,
        grid=(n // tm,),
        in_specs=[row(COL_GMA // D_MODEL), row(COL_GMN // D_MODEL), row(0), row(0), row(0),
                  pl.BlockSpec((D_MODEL, D_MODEL), lambda i: (0, 0)),
                  pl.BlockSpec((1, D_MODEL), lambda i: (0, 0))],
        out_specs=row(0),
        out_shape=jax.ShapeDtypeStruct((n, D_MODEL), F32),
        compiler_params=_cparams(("parallel",)),
        name="merge_out",
    )(z2, z2, o_a, o_n, x2, w_o, g_post1.reshape(1, D_MODEL))


def _ffn_kernel(x_ref, g2_ref, wg_ref, wu_ref, wo_ref, gp_ref, o_ref, h_scr, acc_scr):
    j = pl.program_id(1)

    @pl.when(j == 0)
    def _():
        x = x_ref[...]
        h_scr[...] = ((x * _rms_rows(x)) * g2_ref[...]).astype(BF16)
        acc_scr[...] = jnp.zeros_like(acc_scr)

    h = h_scr[...]
    gt = jnp.dot(h, wg_ref[...], preferred_element_type=F32)
    up = jnp.dot(h, wu_ref[...], preferred_element_type=F32)
    acc_scr[...] += jnp.dot((_silu(gt) * up).astype(BF16), wo_ref[...], preferred_element_type=F32)

    @pl.when(j == pl.num_programs(1) - 1)
    def _():
        y = acc_scr[...]
        o_ref[...] = x_ref[...] + (y * _rms_rows(y)) * gp_ref[...]


def _ffn(x2, g_pre2, w_ffn_in, w_ffn_out, g_post2, tm, th=512):
    n = x2.shape[0]
    nh = FFN_HIDDEN // th
    return pl.pallas_call(
        _ffn_kernel,
        grid=(n // tm, nh),
        in_specs=[
            pl.BlockSpec((tm, D_MODEL), lambda i, j: (i, 0)),
            pl.BlockSpec((1, D_MODEL), lambda i, j: (0, 0)),
            pl.BlockSpec((D_MODEL, th), lambda i, j: (0, j)),
            pl.BlockSpec((D_MODEL, th), lambda i, j: (0, nh + j)),
            pl.BlockSpec((th, D_MODEL), lambda i, j: (j, 0)),
            pl.BlockSpec((1, D_MODEL), lambda i, j: (0, 0)),
        ],
        out_specs=pl.BlockSpec((tm, D_MODEL), lambda i, j: (i, 0)),
        out_shape=jax.ShapeDtypeStruct((n, D_MODEL), F32),
        scratch_shapes=[pltpu.VMEM((tm, D_MODEL), BF16), pltpu.VMEM((tm, D_MODEL), F32)],
        compiler_params=_cparams(("parallel", "arbitrary")),
        name="ffn",
    )(x2, g_pre2.reshape(1, D_MODEL), w_ffn_in, w_ffn_in, w_ffn_out, g_post2.reshape(1, D_MODEL))


def _ple_kernel(x_ref, p_ref, wg_ref, wp_ref, o_ref):
    x = x_ref[...]
    gate = _sigmoid(jnp.dot(x.astype(BF16), wg_ref[...], preferred_element_type=F32))
    o_ref[...] = x + gate * jnp.dot(p_ref[...].astype(BF16), wp_ref[...], preferred_element_type=F32)


def _ple(x2, ple2, w_ple_gate, w_ple, tm):
    n = x2.shape[0]
    return pl.pallas_call(
        _ple_kernel,
        grid=(n // tm,),
        in_specs=[pl.BlockSpec((tm, D_MODEL), lambda i: (i, 0)),
                  pl.BlockSpec((tm, PLE_DIM), lambda i: (i, 0)),
                  pl.BlockSpec((D_MODEL, D_MODEL), lambda i: (0, 0)),
                  pl.BlockSpec((PLE_DIM, D_MODEL), lambda i: (0, 0))],
        out_specs=pl.BlockSpec((tm, D_MODEL), lambda i: (i, 0)),
        out_shape=jax.ShapeDtypeStruct((n, D_MODEL), F32),
        compiler_params=_cparams(("parallel",)),
        name="ple",
    )(x2, ple2, w_ple_gate, w_ple)


SAMPLE_ROWS = 8
KV_ROW = 2 * NSA_KV_HEADS


def _compress_pages_kernel(npg, pt_ref, *refs):
    w = refs[npg][...]
    o_ref = refs[npg + 1]
    per = PAGE_SIZE // CMP_BLOCK
    for p in range(npg):
        x = refs[p][...].reshape(per, CMP_BLOCK, KV_ROW, HEAD_DIM)
        o_ref[p * per:(p + 1) * per] = jnp.sum(x * w[None], axis=1)


def _compress_sample(cache_rows, page_table, w_cmp3, npg=8):
    Bs, n_pages = page_table.shape
    per = PAGE_SIZE // CMP_BLOCK

    def page(p):
        return pl.BlockSpec((None, PAGE_SIZE * KV_ROW, HEAD_DIM), lambda b, i, pt: (pt[b, i * npg + p], 0, 0))

    return pl.pallas_call(
        functools.partial(_compress_pages_kernel, npg),
        grid_spec=pltpu.PrefetchScalarGridSpec(
            num_scalar_prefetch=1,
            grid=(Bs, n_pages // npg),
            in_specs=[page(p) for p in range(npg)]
            + [pl.BlockSpec((CMP_BLOCK, KV_ROW, HEAD_DIM), lambda b, i, pt: (0, 0, 0))],
            out_specs=pl.BlockSpec((None, npg * per, KV_ROW, HEAD_DIM), lambda b, i, pt: (b, i, 0, 0)),
        ),
        out_shape=jax.ShapeDtypeStruct((Bs, n_pages * per, KV_ROW, HEAD_DIM), F32),
        compiler_params=_cparams(("parallel", "parallel")),
        name="compress_sample",
    )(page_table, *([cache_rows] * npg), w_cmp3)


def _sel_pages_kernel(npg, past_len, pt_ref, slopes_ref, q_ref, sel_ref, *refs):
    pages = refs[:npg]
    m_ref, l_ref, acc_ref, q_scr, selrows_scr, slope_scr = refs[npg:]
    i = pl.program_id(1)
    nrow = NSA_HEADS * SAMPLE_ROWS
    rg = NSA_GROUP * SAMPLE_ROWS
    ns_pad = sel_ref.shape[-1]

    @pl.when(i == 0)
    def _():
        for hd in range(NSA_HEADS):
            rs = slice(hd * SAMPLE_ROWS, (hd + 1) * SAMPLE_ROWS)
            q_scr[rs, :] = q_ref[:, hd * HEAD_DIM:(hd + 1) * HEAD_DIM] * (QK_SCALE * LOG2E)
            selrows_scr[rs, :] = sel_ref[hd // NSA_GROUP]
            slope_scr[rs, :] = jnp.full((SAMPLE_ROWS, HEAD_DIM), slopes_ref[hd] * LOG2E, F32)
        m_ref[...] = jnp.full(m_ref.shape, NEG_INF, F32)
        l_ref[...] = jnp.zeros(l_ref.shape, F32)
        acc_ref[...] = jnp.zeros(acc_ref.shape, F32)

    qb = q_scr[...].astype(BF16)
    selb = selrows_scr[...].astype(BF16)
    slope = slope_scr[...]
    row = lax.broadcasted_iota(jnp.int32, (nrow, PAGE_SIZE), 0)
    pos = lax.broadcasted_iota(jnp.int32, (nrow, PAGE_SIZE), 1)
    d0 = (past_len + row % SAMPLE_ROWS - pos).astype(F32)
    ob = lax.broadcasted_iota(jnp.int32, (ns_pad, PAGE_SIZE), 0)
    ol = lax.broadcasted_iota(jnp.int32, (ns_pad, PAGE_SIZE), 1) // SEL_BLOCK
    scores = []
    for p in range(npg):
        pg = i * npg + p
        flags = jnp.dot(selb, (ob == ol + pg * (PAGE_SIZE // SEL_BLOCK)).astype(BF16),
                        preferred_element_type=F32)
        distf = d0 - (pg * PAGE_SIZE).astype(F32)
        base = jnp.where((flags > 0.5) & (distf >= 0.0), distf, MASK_DIST)
        scores.append(jnp.concatenate(
            [_bdot_nt(qb[g * rg:(g + 1) * rg], pages[p][pl.ds(g, PAGE_SIZE, stride=KV_ROW), :])
             for g in range(NSA_KV_HEADS)], axis=0) - slope * base)
    s = jnp.concatenate(scores, axis=1)
    m_old = m_ref[...]
    m_new = jnp.maximum(m_old, jnp.max(s, axis=1, keepdims=True))
    pr = jnp.exp2(s - m_new[:, 0:1])
    alpha = jnp.exp2(m_old - m_new)
    l_ref[...] = alpha * l_ref[...] + jnp.sum(pr, axis=1, keepdims=True)
    prb = pr.astype(BF16)
    pv = jnp.zeros((nrow, HEAD_DIM), F32)
    for p in range(npg):
        ps = prb[:, p * PAGE_SIZE:(p + 1) * PAGE_SIZE]
        pv = pv + jnp.concatenate(
            [_bdot(ps[g * rg:(g + 1) * rg], pages[p][pl.ds(NSA_KV_HEADS + g, PAGE_SIZE, stride=KV_ROW), :])
             for g in range(NSA_KV_HEADS)], axis=0)
    acc_ref[...] = alpha * acc_ref[...] + pv
    m_ref[...] = m_new


def _sel_sample(z3s, selmask, cache_rows, page_table, slopes, past_len, npg=8):
    Bs, n_pages = page_table.shape
    nrow = NSA_HEADS * SAMPLE_ROWS
    ns_pad = selmask.shape[-1]

    def page(p):
        return pl.BlockSpec((None, PAGE_SIZE * KV_ROW, HEAD_DIM), lambda b, i, pt, sl: (pt[b, i * npg + p], 0, 0))

    part = pl.BlockSpec((None, nrow, HEAD_DIM), lambda b, i, pt, sl: (b, 0, 0))
    return pl.pallas_call(
        functools.partial(_sel_pages_kernel, npg, past_len),
        grid_spec=pltpu.PrefetchScalarGridSpec(
            num_scalar_prefetch=2,
            grid=(Bs, n_pages // npg),
            in_specs=[
                pl.BlockSpec((None, SAMPLE_ROWS, NSA_HEADS * HEAD_DIM), lambda b, i, pt, sl: (b, 0, COL_QN // 2048)),
                pl.BlockSpec((None, NSA_KV_HEADS, SAMPLE_ROWS, ns_pad), lambda b, i, pt, sl: (b, 0, 0, 0)),
            ] + [page(p) for p in range(npg)],
            out_specs=[part, part, part],
            scratch_shapes=[pltpu.VMEM((nrow, HEAD_DIM), F32), pltpu.VMEM((nrow, ns_pad), F32),
                            pltpu.VMEM((nrow, HEAD_DIM), F32)],
        ),
        out_shape=[jax.ShapeDtypeStruct((Bs, nrow, HEAD_DIM), F32)] * 3,
        compiler_params=_cparams(("parallel", "arbitrary")),
        name="sel_sample",
    )(page_table, slopes, z3s, selmask, *([cache_rows] * npg))


def _finish_sample_kernel(past_len, t_real, nnew, slopes_ref, q_ref, kst_ref, snew_ref, wnew_ref, sel_ref,
                          m_ref, l_ref, acc_ref, oc_ref, zs_ref, o_ref):
    nst = kst_ref.shape[0]
    cur = past_len // SEL_BLOCK
    gates = _sigmoid(zs_ref[...])
    t_new = lax.broadcasted_iota(jnp.int32, (SAMPLE_ROWS, nnew), 0)
    j_new = lax.broadcasted_iota(jnp.int32, (SAMPLE_ROWS, nnew), 1)
    dist_new = t_new - j_new
    ok_new = (dist_new >= 0) & (j_new < t_real)
    t_st = lax.broadcasted_iota(jnp.int32, (SAMPLE_ROWS, nst), 0)
    i_st = lax.broadcasted_iota(jnp.int32, (SAMPLE_ROWS, nst), 1)
    dist_st = t_st + nst - i_st
    ok_st = dist_st < WINDOW
    for hd in range(NSA_HEADS):
        g = hd // NSA_GROUP
        rs = slice(hd * SAMPLE_ROWS, (hd + 1) * SAMPLE_ROWS)
        kc = slice(g * HEAD_DIM, (g + 1) * HEAD_DIM)
        vc = slice(NSA_KV_WIDTH + g * HEAD_DIM, NSA_KV_WIDTH + (g + 1) * HEAD_DIM)
        sl = slopes_ref[hd] * LOG2E
        qh = (q_ref[:, hd * HEAD_DIM:(hd + 1) * HEAD_DIM] * (QK_SCALE * LOG2E)).astype(BF16)
        valid = ok_new & (sel_ref[g][:, cur:cur + 1] > 0.5)
        s = jnp.where(valid, _bdot_nt(qh, snew_ref[:, kc]) - sl * dist_new.astype(F32), NEG_INF)
        m_old = m_ref[rs, 0:1]
        m_new = jnp.maximum(m_old, jnp.max(s, axis=1, keepdims=True))
        pr = jnp.where(valid, jnp.exp2(s - m_new), 0.0)
        alpha = jnp.exp2(m_old - m_new)
        l = alpha * l_ref[rs, 0:1] + jnp.sum(pr, axis=1, keepdims=True)
        o_s = (alpha * acc_ref[rs, :] + _bdot(pr, snew_ref[:, vc])) / l
        s1 = jnp.where(ok_st, _bdot_nt(qh, kst_ref[:, kc]) - sl * dist_st.astype(F32), NEG_INF)
        s2 = jnp.where(ok_new, _bdot_nt(qh, wnew_ref[:, kc]) - sl * dist_new.astype(F32), NEG_INF)
        mw = jnp.maximum(jnp.max(s1, axis=1, keepdims=True), jnp.max(s2, axis=1, keepdims=True))
        e1 = jnp.where(ok_st, jnp.exp2(s1 - mw), 0.0)
        e2 = jnp.where(ok_new, jnp.exp2(s2 - mw), 0.0)
        den = jnp.sum(e1, axis=1, keepdims=True) + jnp.sum(e2, axis=1, keepdims=True)
        o_w = _bdot(e1 / den, kst_ref[:, vc]) + _bdot(e2 / den, wnew_ref[:, vc])
        gi = SMALL_GN + hd * 3
        cs = slice(hd * HEAD_DIM, (hd + 1) * HEAD_DIM)
        o_ref[:, cs] = oc_ref[:, cs] + gates[:, gi + 1:gi + 2] * o_s + gates[:, gi + 2:gi + 3] * o_w


def _finish_sample(z3s, zs3s, state_win2, selmask, m, l, acc, oc, slopes, past_len, t_real):
    Bs, tz, _ = z3s.shape
    nst = state_win2.shape[1]
    nrow = NSA_HEADS * SAMPLE_ROWS
    ns_pad = selmask.shape[-1]
    kvw = 2 * NSA_KV_WIDTH
    part = pl.BlockSpec((None, nrow, HEAD_DIM), lambda b, sl: (b, 0, 0))
    wide = pl.BlockSpec((None, SAMPLE_ROWS, NSA_HEADS * HEAD_DIM), lambda b, sl: (b, 0, 0))
    return pl.pallas_call(
        functools.partial(_finish_sample_kernel, past_len, t_real, tz),
        grid_spec=pltpu.PrefetchScalarGridSpec(
            num_scalar_prefetch=1,
            grid=(Bs,),
            in_specs=[
                pl.BlockSpec((None, SAMPLE_ROWS, NSA_HEADS * HEAD_DIM), lambda b, sl: (b, 0, COL_QN // 2048)),
                pl.BlockSpec((None, nst, kvw), lambda b, sl: (b, 0, 0)),
                pl.BlockSpec((None, tz, kvw), lambda b, sl: (b, 0, COL_SEL // kvw)),
                pl.BlockSpec((None, tz, kvw), lambda b, sl: (b, 0, COL_WIN // kvw)),
                pl.BlockSpec((None, NSA_KV_HEADS, SAMPLE_ROWS, ns_pad), lambda b, sl: (b, 0, 0, 0)),
                part, part, part, wide,
                pl.BlockSpec((None, SAMPLE_ROWS, SMALL_WIDTH), lambda b, sl: (b, 0, 0)),
            ],
            out_specs=wide,
        ),
        out_shape=jax.ShapeDtypeStruct((Bs, SAMPLE_ROWS, NSA_HEADS * HEAD_DIM), F32),
        compiler_params=_cparams(("parallel",)),
        name="finish_sample",
    )(slopes, z3s, state_win2, z3s, z3s, selmask, m, l, acc, oc, zs3s)


def _mix_and_ffn(z3, o_a, o_n, x3, ple3, wts, tm):
    B, T, _ = x3.shape
    n = B * T
    x1 = _merge_out(z3.reshape(n, MAIN_WIDTH), o_a.reshape(n, GDN_WIDTH), o_n.reshape(n, D_MODEL),
                    x3.reshape(n, D_MODEL), wts["w_o"], wts["g_post1"], min(tm, 256))
    x2 = _ffn(x1, wts["g_pre2"], wts["w_ffn_in"], wts["w_ffn_out"], wts["g_post2"], tm)
    x3o = _ple(x2, ple3.reshape(n, PLE_DIM), wts["w_ple_gate"], wts["w_ple"], min(tm, 256))
    return x3o.reshape(B, T, D_MODEL)


def kernel(x_prompt, x_sample, cache_cmp_kv, cache_sel_kv, page_table, state_win_kv, state_gdn, state_conv, p_prompt, p_sample, g_pre1, w_in, conv_w, A_log, dt_bias, gdn_norm_w, w_cmp, w_o, g_post1, g_pre2, w_ffn_in, w_ffn_out, g_post2, w_ple, w_ple_gate):
    B, T, _ = x_prompt.shape
    Bs, Ts, _ = x_sample.shape
    n_pages = page_table.shape[1]
    past_len = n_pages * PAGE_SIZE
    win_buf = state_win_kv.shape[2]
    kvh = (2, NSA_KV_HEADS, HEAD_DIM)
    qkv_w = 3 * GDN_WIDTH

    wi = w_in[0]
    w_main = (wi[:, 0:8192].astype(BF16), wi[:, 13392:17488].astype(BF16), wi[:, 8224:13344].astype(BF16))
    w_small = jnp.concatenate([wi[:, 8192:8224], wi[:, 13344:13392],
                               jnp.zeros((D_MODEL, SMALL_WIDTH - 80), F32)], axis=1).astype(BF16)
    wts = dict(w_o=w_o[0].astype(BF16), g_post1=g_post1[0], g_pre2=g_pre2[0],
               w_ffn_in=w_ffn_in[0].astype(BF16), w_ffn_out=w_ffn_out[0].astype(BF16), g_post2=g_post2[0],
               w_ple=w_ple[0].astype(BF16), w_ple_gate=w_ple_gate[0].astype(BF16))
    hp = jnp.zeros((8, SMALL_WIDTH), F32).at[0, 0:GDN_HEADS].set(A_log[0]).at[1, 0:GDN_HEADS].set(dt_bias[0])
    w_cmp2 = w_cmp[0].reshape(CMP_BLOCK, 2 * NSA_KV_WIDTH)
    heads = jnp.arange(1, NSA_HEADS + 1, dtype=F32)
    slopes = jnp.exp2(-8.0 * heads / NSA_HEADS)

    z2, zs2 = _inproj(x_prompt.reshape(B * T, D_MODEL), g_pre1[0], w_main, w_small, 1024)
    z3, zs3 = z2.reshape(B, T, MAIN_WIDTH), zs2.reshape(B, T, SMALL_WIDTH)
    o_a, s_new_p = _gdn(z3, zs3, jnp.zeros((B, 8, qkv_w), F32), conv_w[0], hp, gdn_norm_w[0],
                        jnp.zeros((B, GDN_HEADS, HEAD_DIM, HEAD_DIM), F32), 256, 256)
    kvc = _compress_prompt(z3, w_cmp2)
    ns = T // SEL_BLOCK
    oc, selmask = _cmp_attn(z3, zs3, kvc, slopes, 512, T, ns, ns, 0, "cmp_attn_prompt")
    o_n = _sel_win_prompt(z3, zs3, selmask, oc, slopes)
    y_prompt = _mix_and_ffn(z3, o_a, o_n, x_prompt, p_prompt[0], wts, 512)

    tz = GDN_CHUNK
    xs = jnp.pad(x_sample, ((0, 0), (0, tz - Ts), (0, 0)))
    zs2_, zss2 = _inproj(xs.reshape(Bs * tz, D_MODEL), g_pre1[0], w_main, w_small, Bs * tz)
    z3s, zs3s = zs2_.reshape(Bs, tz, MAIN_WIDTH), zss2.reshape(Bs, tz, SMALL_WIDTH)
    conv_prev = jnp.pad(state_conv[0], ((0, 0), (8 - (CONV_WIDTH - 1), 0), (0, 0)))
    o_a_s, s_new_s = _gdn(z3s, zs3s, conv_prev, conv_w[0], hp, gdn_norm_w[0], state_gdn[0], tz, Ts, nh=GDN_HEADS)
    n_pool = cache_cmp_kv.shape[1]
    kvc_s = _compress_sample(cache_cmp_kv[0].reshape(n_pool, PAGE_SIZE * KV_ROW, HEAD_DIM), page_table,
                             w_cmp[0].reshape(CMP_BLOCK, KV_ROW, HEAD_DIM))
    kvc_s = kvc_s.reshape(Bs, kvc_s.shape[1], 2 * NSA_KV_WIDTH)
    ns_real = -(-(past_len + Ts) // SEL_BLOCK)
    ns_pad = -(-ns_real // 128) * 128
    oc_s, selmask_s = _cmp_attn(z3s, zs3s, kvc_s, slopes, SAMPLE_ROWS, SAMPLE_ROWS, ns_pad, ns_real, past_len,
                                "cmp_attn_sample")
    m_s, l_s, acc_s = _sel_sample(z3s, selmask_s, cache_sel_kv[0].reshape(n_pool, PAGE_SIZE * KV_ROW, HEAD_DIM),
                                  page_table, slopes, past_len)
    state_win2 = state_win_kv[0].reshape(Bs, win_buf, 2 * NSA_KV_WIDTH)
    o_n_s = _finish_sample(z3s, zs3s, state_win2, selmask_s, m_s, l_s, acc_s, oc_s, slopes, past_len, Ts)
    o_n_s = jnp.pad(o_n_s, ((0, 0), (0, tz - SAMPLE_ROWS), (0, 0)))
    ps = jnp.pad(p_sample[0], ((0, 0), (0, tz - Ts), (0, 0)))
    y_sample = _mix_and_ffn(z3s, o_a_s, o_n_s, xs, ps, wts, Bs * tz)[:, :Ts]

    def kv_rows(z, col, lo, hi):
        return z[:, lo:hi, col:col + 2 * NSA_KV_WIDTH].reshape((z.shape[0], hi - lo) + kvh)

    new_win_s = jnp.concatenate([state_win_kv[0][:, Ts:], kv_rows(z3s, COL_WIN, 0, Ts)], axis=1)
    return (y_prompt, y_sample,
            kv_rows(z3, COL_CMP, 0, T)[None], kv_rows(z3, COL_SEL, 0, T)[None],
            kv_rows(z3, COL_WIN, T - win_buf, T)[None], s_new_p[None],
            z3[:, T - (CONV_WIDTH - 1):, 0:qkv_w][None],
            kv_rows(z3s, COL_CMP, 0, Ts)[None], kv_rows(z3s, COL_SEL, 0, Ts)[None],
            new_win_s[None], s_new_s[None],
            z3s[:, Ts - (CONV_WIDTH - 1):Ts, 0:qkv_w][None])
```

```python
import functools

import jax
import jax.numpy as jnp
from jax import lax
from jax.experimental import pallas as pl
from jax.experimental.pallas import tpu as pltpu

F32 = jnp.float32
BF16 = jnp.bfloat16
HI = lax.Precision.HIGHEST

D_MODEL = 2048
HEAD_DIM = 128
GDN_HEADS = 16
GDN_WIDTH = 2048
CONV_WIDTH = 4
GDN_CHUNK = 64
NSA_HEADS = 16
NSA_KV_HEADS = 4
NSA_GROUP = 4
NSA_KV_WIDTH = 512
CMP_BLOCK = 32
SEL_BLOCK = 64
SEL_TOPK = 16
WINDOW = 512
PLE_DIM = 256
FFN_HIDDEN = 5632
PAGE_SIZE = 128
RMS_EPS = 1e-6
NEG_INF = -1e30
FORCE_BONUS = float(NSA_GROUP + 1)
QK_SCALE = HEAD_DIM ** -0.5
LOG2E = 1.4426950408889634
MASK_DIST = 1e32

COL_QKV = 0
COL_ZA = 6144
COL_GMA = 8192
COL_GMN = 10240
COL_QN = 12288
COL_CMP = 14336
COL_SEL = 15360
COL_WIN = 16384
MAIN_WIDTH = 17408
SMALL_WIDTH = 128
SMALL_A, SMALL_B, SMALL_GN = 0, 16, 32

VMEM_LIMIT = 56 * 1024 * 1024


def _cparams(sem):
    return pltpu.CompilerParams(dimension_semantics=sem, vmem_limit_bytes=VMEM_LIMIT)


def _bdot(a, b):
    return jnp.dot(a.astype(BF16), b.astype(BF16), preferred_element_type=F32)


def _bdot_nt(a, b):
    return lax.dot_general(a.astype(BF16), b.astype(BF16), (((1,), (1,)), ((), ())),
                           preferred_element_type=F32)


def _hdot(a, b):
    return jnp.dot(a, b, precision=HI, preferred_element_type=F32)


def _hdot_nt(a, b):
    return lax.dot_general(a, b, (((1,), (1,)), ((), ())), precision=HI, preferred_element_type=F32)


def _sigmoid(x):
    return 1.0 / (1.0 + jnp.exp(-x))


def _silu(x):
    return x * _sigmoid(x)


def _rms_rows(x):
    return lax.rsqrt(jnp.mean(x * x, axis=-1, keepdims=True) + RMS_EPS)


def _inproj_kernel(bounds, x_ref, g_ref, *refs):
    w_refs = refs[:len(bounds)]
    ws_ref, z_ref, zs_ref, h_scr = refs[len(bounds):]
    j = pl.program_id(1)

    @pl.when(j == 0)
    def _():
        x = x_ref[...]
        h = ((x * _rms_rows(x)) * g_ref[...]).astype(BF16)
        h_scr[...] = h
        zs_ref[...] = jnp.dot(h, ws_ref[...], preferred_element_type=F32)

    for (lo, hi), w_ref in zip(bounds, w_refs):
        @pl.when((j >= lo) & (j < hi))
        def _(w_ref=w_ref):
            z_ref[...] = jnp.dot(h_scr[...], w_ref[...], preferred_element_type=F32)


def _inproj(x2, g_pre1, w_slabs, w_small, tm, tn=512):
    n = x2.shape[0]
    bounds, lo = [], 0
    for w in w_slabs:
        bounds.append((lo, lo + w.shape[1] // tn))
        lo = bounds[-1][1]
    assert lo * tn == MAIN_WIDTH

    def slab_spec(lo, hi):
        return pl.BlockSpec((D_MODEL, tn), lambda i, j: (0, jnp.clip(j - lo, 0, hi - lo - 1)))

    return pl.pallas_call(
        functools.partial(_inproj_kernel, tuple(bounds)),
        grid=(n // tm, MAIN_WIDTH // tn),
        in_specs=[
            pl.BlockSpec((tm, D_MODEL), lambda i, j: (i, 0)),
            pl.BlockSpec((1, D_MODEL), lambda i, j: (0, 0)),
        ] + [slab_spec(lo, hi) for lo, hi in bounds] + [
            pl.BlockSpec((D_MODEL, SMALL_WIDTH), lambda i, j: (0, 0)),
        ],
        out_specs=[
            pl.BlockSpec((tm, tn), lambda i, j: (i, j)),
            pl.BlockSpec((tm, SMALL_WIDTH), lambda i, j: (i, 0)),
        ],
        out_shape=[jax.ShapeDtypeStruct((n, MAIN_WIDTH), F32),
                   jax.ShapeDtypeStruct((n, SMALL_WIDTH), F32)],
        scratch_shapes=[pltpu.VMEM((tm, D_MODEL), BF16)],
        compiler_params=_cparams(("parallel", "arbitrary")),
        name="inproj",
    )(x2, g_pre1.reshape(1, D_MODEL), *w_slabs, w_small)


def _level_mask(ii, jj, s):
    return ((ii // s) % 2 == 1) & (jj // s == ii // s - 1)


def _gdn_kernel(tb, nh, t_valid,
                q_ref, k_ref, v_ref, qh_ref, kh_ref, vh_ref, cpq_ref, cpk_ref, cpv_ref,
                cwq_ref, cwk_ref, cwv_ref, z_ref, zs_ref, hp_ref, nw_ref, s0_ref,
                o_ref, sn_ref, ext_scr, gt_scr, s_scr):
    C = GDN_CHUNK
    nch = tb // C
    hg = pl.program_id(1)
    t = pl.program_id(2)
    nt = pl.num_programs(2)

    @pl.when(t == 0)
    def _():
        s_scr[...] = s0_ref[...]

    def conv_silu(u_ref, halo_ref, cp_ref, cw_ref):
        prev = jnp.where(t == 0, cp_ref[...], halo_ref[...])
        ext_scr[0:8, :] = prev
        ext_scr[8:8 + tb, :] = u_ref[...]
        w = cw_ref[...]
        acc = ext_scr[5:5 + tb, :] * w[0:1, :]
        for j in range(1, CONV_WIDTH):
            acc = acc + ext_scr[5 + j:5 + j + tb, :] * w[j:j + 1, :]
        return _silu(acc)

    q_all = conv_silu(q_ref, qh_ref, cpq_ref, cwq_ref)
    k_all = conv_silu(k_ref, kh_ref, cpk_ref, cwk_ref)
    v_all = conv_silu(v_ref, vh_ref, cpv_ref, cwv_ref)

    zs = zs_ref[...]
    lane = lax.broadcasted_iota(jnp.int32, (GDN_CHUNK, SMALL_WIDTH), 1)
    xa = zs + hp_ref[1:2, :]
    softplus = jnp.maximum(xa, 0.0) + jnp.log1p(jnp.exp(-jnp.abs(xa)))
    g_all = -jnp.exp(hp_ref[0:1, :]) * softplus
    beta_all = _sigmoid(zs)
    if t_valid < tb:
        live = lax.broadcasted_iota(jnp.int32, (tb, 1), 0) < t_valid
        g_all = jnp.where(live, g_all, 0.0)
        beta_all = jnp.where(live, beta_all, 0.0)

    ii = lax.broadcasted_iota(jnp.int32, (C, C), 0)
    jj = lax.broadcasted_iota(jnp.int32, (C, C), 1)
    tril = (ii >= jj).astype(F32)
    eye = (ii == jj).astype(F32)
    gcum = []
    for c in range(nch):
        gc = _hdot(tril, g_all[c * C:(c + 1) * C])
        gcum.append(gc)
        gt_scr[c] = jnp.transpose(gc)

    Ls, rhss, qkds, kdts, qgs, egls = [], [], [], [], [], []
    for hl in range(nh):
        hglob = hg * nh + hl
        hs = slice(hl * HEAD_DIM, (hl + 1) * HEAD_DIM)
        q = q_all[:, hs]
        k = k_all[:, hs]
        q = (q * lax.rsqrt(jnp.sum(q * q, axis=-1, keepdims=True) + RMS_EPS)) * QK_SCALE
        k = k * lax.rsqrt(jnp.sum(k * k, axis=-1, keepdims=True) + RMS_EPS)
        for c in range(nch):
            sl = slice(c * C, (c + 1) * C)
            qc, kc, vc = q[sl], k[sl], v_all[sl, hs]
            bcol = jnp.sum(jnp.where(lane == SMALL_B + hglob, beta_all[sl], 0.0), axis=1, keepdims=True)
            gcol = jnp.sum(jnp.where(lane == SMALL_A + hglob, gcum[c], 0.0), axis=1, keepdims=True)
            grow = gt_scr[c, pl.ds(SMALL_A + hglob, 1), :]
            decay = jnp.exp(jnp.where(ii >= jj, gcol - grow, NEG_INF))
            qkk = _bdot_nt(jnp.concatenate([qc, kc], axis=0), kc)
            Ls.append(jnp.where(ii > jj, qkk[C:] * decay, 0.0) * bcol)
            eg = jnp.exp(gcol)
            g_last = grow[:, C - 1:C]
            rhss.append(jnp.concatenate([vc * bcol, kc * (bcol * eg)], axis=1))
            qkds.append(qkk[:C] * decay)
            kdts.append(jnp.transpose(kc * jnp.exp(g_last - gcol)))
            qgs.append(qc * eg)
            egls.append(jnp.exp(g_last))

    n_inst = nh * nch
    Xs = [eye - jnp.where(_level_mask(ii, jj, 1), L, 0.0) for L in Ls]
    s = 2
    while s < C:
        m = _level_mask(ii, jj, s)
        Ys = [_bdot(jnp.where(m, Ls[i], 0.0), Xs[i]) for i in range(n_inst)]
        Zs = [_bdot(Xs[i], Ys[i]) for i in range(n_inst)]
        Xs = [Xs[i] - Zs[i] for i in range(n_inst)]
        s *= 2
    sols = [_bdot(Xs[i], rhss[i]) for i in range(n_inst)]
    res = [rhss[i] - sols[i] - _hdot(Ls[i], sols[i]) for i in range(n_inst)]
    sols = [sols[i] + _bdot(Xs[i], res[i]) for i in range(n_inst)]
    NPs = [_bdot(kdts[i], sols[i]) for i in range(n_inst)]
    QOs = [_bdot(qkds[i], sols[i]) for i in range(n_inst)]

    Ss = [s_scr[hl] for hl in range(nh)]
    outs = [None] * n_inst
    for c in range(nch):
        for hl in range(nh):
            i = hl * nch + c
            S = Ss[hl]
            outs[i] = _bdot(qgs[i] - QOs[i][:, HEAD_DIM:], S) + QOs[i][:, :HEAD_DIM]
            Ss[hl] = S * egls[i] - _bdot(NPs[i][:, HEAD_DIM:], S) + NPs[i][:, :HEAD_DIM]
    nw = nw_ref[...]
    for hl in range(nh):
        hs = slice(hl * HEAD_DIM, (hl + 1) * HEAD_DIM)
        s_scr[hl] = Ss[hl]
        for c in range(nch):
            sl = slice(c * C, (c + 1) * C)
            o = outs[hl * nch + c]
            o_ref[sl, hs] = ((o * _rms_rows(o)) * nw) * _silu(z_ref[sl, hs])

    @pl.when(t == nt - 1)
    def _():
        sn_ref[...] = s_scr[...]


def _gdn(z3, zs3, conv_prev, conv_w, hp, norm_w, s0, tb, t_valid, nh=4):
    B, T, _ = z3.shape
    nt = T // tb
    hb = tb // 8
    wblk = nh * HEAD_DIM
    cq, ck, cv = COL_QKV // wblk, (COL_QKV + GDN_WIDTH) // wblk, (COL_QKV + 2 * GDN_WIDTH) // wblk
    hpg = GDN_HEADS // nh

    def main(col0):
        return pl.BlockSpec((None, tb, wblk), lambda b, h, t: (b, t, col0 + h))

    def halo(col0):
        return pl.BlockSpec((None, 8, wblk), lambda b, h, t: (b, jnp.maximum(t * hb - 1, 0), col0 + h))

    def cprev(col0):
        return pl.BlockSpec((None, 8, wblk), lambda b, h, t: (b, 0, col0 + h))

    def cw(col0):
        return pl.BlockSpec((CONV_WIDTH, wblk), lambda b, h, t: (0, col0 + h))

    state = pl.BlockSpec((None, nh, HEAD_DIM, HEAD_DIM), lambda b, h, t: (b, h, 0, 0))
    return pl.pallas_call(
        functools.partial(_gdn_kernel, tb, nh, t_valid),
        grid=(B, hpg, nt),
        in_specs=[
            main(cq), main(ck), main(cv), halo(cq), halo(ck), halo(cv),
            cprev(0), cprev(hpg), cprev(2 * hpg), cw(0), cw(hpg), cw(2 * hpg),
            main(COL_ZA // wblk),
            pl.BlockSpec((None, tb, SMALL_WIDTH), lambda b, h, t: (b, t, 0)),
            pl.BlockSpec((8, SMALL_WIDTH), lambda b, h, t: (0, 0)),
            pl.BlockSpec((1, HEAD_DIM), lambda b, h, t: (0, 0)),
            state,
        ],
        out_specs=[pl.BlockSpec((None, tb, wblk), lambda b, h, t: (b, t, h)), state],
        out_shape=[jax.ShapeDtypeStruct((B, T, GDN_WIDTH), F32),
                   jax.ShapeDtypeStruct((B, GDN_HEADS, HEAD_DIM, HEAD_DIM), F32)],
        scratch_shapes=[pltpu.VMEM((tb + 8, wblk), F32),
                        pltpu.VMEM((tb // GDN_CHUNK, SMALL_WIDTH, GDN_CHUNK), F32),
                        pltpu.VMEM((nh, HEAD_DIM, HEAD_DIM), F32)],
        compiler_params=_cparams(("parallel", "parallel", "arbitrary")),
        name="gdn",
    )(z3, z3, z3, z3, z3, z3, conv_prev, conv_prev, conv_prev, conv_w, conv_w, conv_w,
      z3, zs3, hp, norm_w.reshape(1, HEAD_DIM), s0)


def _compress_kernel(x_ref, w_ref, o_ref):
    rows = x_ref.shape[0]
    x = x_ref[...].reshape(rows // CMP_BLOCK, CMP_BLOCK, 2 * NSA_KV_WIDTH)
    o_ref[...] = jnp.sum(x * w_ref[...][None], axis=1)


def _compress_prompt(z3, w_cmp2, tc=256):
    B, T, _ = z3.shape
    nc = T // CMP_BLOCK
    return pl.pallas_call(
        _compress_kernel,
        grid=(B, T // tc),
        in_specs=[pl.BlockSpec((None, tc, 2 * NSA_KV_WIDTH), lambda b, i: (b, i, COL_CMP // 1024)),
                  pl.BlockSpec((CMP_BLOCK, 2 * NSA_KV_WIDTH), lambda b, i: (0, 0))],
        out_specs=pl.BlockSpec((None, tc // CMP_BLOCK, 2 * NSA_KV_WIDTH), lambda b, i: (b, i, 0)),
        out_shape=jax.ShapeDtypeStruct((B, nc, 2 * NSA_KV_WIDTH), F32),
        compiler_params=_cparams(("parallel", "parallel")),
        name="compress_prompt",
    )(z3, w_cmp2)


def _topk_mask(sc, k_sel, axis):
    n = sc.shape[axis]
    idxf = lax.broadcasted_iota(jnp.int32, sc.shape, axis).astype(F32)
    sel = jnp.zeros(sc.shape, F32)
    for _ in range(k_sel):
        m = jnp.max(sc, axis=axis, keepdims=True)
        idx = jnp.min(jnp.where(sc == m, idxf, float(n)), axis=axis, keepdims=True)
        hit = idxf == idx
        sel = jnp.where(hit, 1.0, sel)
        sc = jnp.where(hit, -3.0, sc)
    return sel


def _cmp_attn_kernel(ta, nc, ns, ns_real, t_base, slopes_ref, q_ref, kvc_ref, zs_ref, oc_ref, sel_ref, sc_scr):
    t0 = t_base + pl.program_id(1) * ta
    tpos = t0 + lax.broadcasted_iota(jnp.int32, (ta, nc), 0)
    cend = lax.broadcasted_iota(jnp.int32, (ta, nc), 1) * CMP_BLOCK + (CMP_BLOCK - 1)
    dist = tpos - cend
    valid = dist >= 0
    distf = dist.astype(F32)
    gates = _sigmoid(zs_ref[...])
    blocks_on_rows = ta % 128 == 0
    tok_ax, blk_ax = (1, 0) if blocks_on_rows else (0, 1)
    sshape = (ns, ta) if blocks_on_rows else (ta, ns)
    pshape = (ns, nc) if blocks_on_rows else (nc, ns)
    pool = (lax.broadcasted_iota(jnp.int32, pshape, tok_ax) // (SEL_BLOCK // CMP_BLOCK)
            == lax.broadcasted_iota(jnp.int32, pshape, blk_ax)).astype(F32)
    tq = t0 + lax.broadcasted_iota(jnp.int32, sshape, tok_ax)
    blk = lax.broadcasted_iota(jnp.int32, sshape, blk_ax)
    cur = tq // SEL_BLOCK
    forced = (blk == 0) | (blk == cur) | (blk == cur - 1)
    avail = blk * SEL_BLOCK <= tq
    for g in range(NSA_KV_HEADS):
        kc = kvc_ref[:, g * HEAD_DIM:(g + 1) * HEAD_DIM]
        vc = kvc_ref[:, NSA_KV_WIDTH + g * HEAD_DIM:NSA_KV_WIDTH + (g + 1) * HEAD_DIM]
        imp = jnp.zeros((ta, nc), F32)
        for r in range(NSA_GROUP):
            hd = g * NSA_GROUP + r
            qh = q_ref[:, hd * HEAD_DIM:(hd + 1) * HEAD_DIM] * QK_SCALE
            s = _hdot_nt(qh, kc) - slopes_ref[hd] * distf
            s = jnp.where(valid, s, NEG_INF)
            e = jnp.exp(s - jnp.max(s, axis=1, keepdims=True))
            p = jnp.where(valid, e / jnp.sum(e, axis=1, keepdims=True), 0.0)
            imp = imp + p
            gi = SMALL_GN + hd * 3
            oc_ref[:, hd * HEAD_DIM:(hd + 1) * HEAD_DIM] = gates[:, gi:gi + 1] * _bdot(p, vc)
        imps = _hdot_nt(pool, imp) if blocks_on_rows else _hdot(imp, pool)
        score = jnp.where(forced, imps + FORCE_BONUS, jnp.where(avail, imps, -1.0))
        if ns_real < ns:
            score = jnp.where(blk < ns_real, score, -2.0)
        if blocks_on_rows:
            sc_scr[:, g * ta:(g + 1) * ta] = score
        else:
            sc_scr[g * ta:(g + 1) * ta, :] = score
    sel = _topk_mask(sc_scr[...], min(SEL_TOPK, ns_real), blk_ax)
    for g in range(NSA_KV_HEADS):
        if blocks_on_rows:
            sel_ref[g] = jnp.transpose(sel[:, g * ta:(g + 1) * ta])
        else:
            sel_ref[g] = sel[g * ta:(g + 1) * ta, :]


def _cmp_attn(z3, zs3, kvc, slopes, ta, n_tok, ns, ns_real, t_base, name):
    B = z3.shape[0]
    nc = kvc.shape[1]
    return pl.pallas_call(
        functools.partial(_cmp_attn_kernel, ta, nc, ns, ns_real, t_base),
        grid_spec=pltpu.PrefetchScalarGridSpec(
            num_scalar_prefetch=1,
            grid=(B, n_tok // ta),
            in_specs=[
                pl.BlockSpec((None, ta, NSA_HEADS * HEAD_DIM), lambda b, i, sl: (b, i, COL_QN // 2048)),
                pl.BlockSpec((None, nc, 2 * NSA_KV_WIDTH), lambda b, i, sl: (b, 0, 0)),
                pl.BlockSpec((None, ta, SMALL_WIDTH), lambda b, i, sl: (b, i, 0)),
            ],
            out_specs=[
                pl.BlockSpec((None, ta, NSA_HEADS * HEAD_DIM), lambda b, i, sl: (b, i, 0)),
                pl.BlockSpec((None, NSA_KV_HEADS, ta, ns), lambda b, i, sl: (b, 0, i, 0)),
            ],
            scratch_shapes=[pltpu.VMEM((ns, NSA_KV_HEADS * ta) if ta % 128 == 0 else (NSA_KV_HEADS * ta, ns), F32)],
        ),
        out_shape=[jax.ShapeDtypeStruct((B, n_tok, NSA_HEADS * HEAD_DIM), F32),
                   jax.ShapeDtypeStruct((B, NSA_KV_HEADS, n_tok, ns), F32)],
        compiler_params=_cparams(("parallel", "parallel")),
        name=name,
    )(slopes, z3, kvc, zs3)


def _sel_win_kernel(T, QB, ns, tkv, wspan, slopes_ref, q_ref, ks_ref, vs_ref, kw_ref, vw_ref, sel_ref,
                    zs_ref, oc_ref, o_ref):
    R = NSA_GROUP
    g = pl.program_id(1)
    qb = pl.program_id(2)
    t0 = qb * QB
    q = jnp.concatenate([(q_ref[:, r * HEAD_DIM:(r + 1) * HEAD_DIM] * (QK_SCALE * LOG2E)).astype(BF16)
                         for r in range(R)], axis=0)
    slope2 = [slopes_ref[g * R + r] * LOG2E for r in range(R)]
    selb = sel_ref[...].astype(BF16)

    def head_rows(x, r):
        return x[r * QB:(r + 1) * QB]

    def spread(cols):
        return jnp.concatenate([jnp.broadcast_to(c, (QB, 2 * HEAD_DIM)) for c in cols], axis=0)

    ti = lax.broadcasted_iota(jnp.int32, (QB, tkv), 0)
    kj = lax.broadcasted_iota(jnp.int32, (QB, tkv), 1)
    d0 = (ti - kj).astype(F32)
    eb = lax.broadcasted_iota(jnp.int32, (ns, tkv), 0)
    ek = lax.broadcasted_iota(jnp.int32, (ns, tkv), 1) // SEL_BLOCK

    def sel_scores(j):
        k0 = j * tkv
        expand = (eb == ek + j * (tkv // SEL_BLOCK)).astype(BF16)
        keymask = jnp.dot(selb, expand, preferred_element_type=F32)
        distf = d0 + (t0 - k0).astype(F32)
        base = jnp.where((distf >= 0.0) & (keymask > 0.5), distf, MASK_DIST)
        return _bdot_nt(q, ks_ref[k0:k0 + tkv, :]), base

    def with_ones(v):
        return jnp.concatenate([v.astype(BF16), jnp.ones(v.shape, BF16)], axis=1)

    def sel_tile(j, s, base, carry):
        ms, acc = carry
        k0 = j * tkv
        ps, ms2, alphas = [], [], []
        for r in range(R):
            sr = head_rows(s, r) - slope2[r] * base
            m_new = jnp.maximum(ms[r], jnp.max(sr, axis=1, keepdims=True))
            ps.append(jnp.exp2((sr - m_new).astype(BF16)))
            alphas.append(jnp.exp2(ms[r] - m_new))
            ms2.append(m_new)
        pv = jnp.dot(jnp.concatenate(ps, axis=0), with_ones(vs_ref[k0:k0 + tkv, :]),
                     preferred_element_type=F32)
        return ms2, spread(alphas) * acc + pv

    def window():
        kstart = pl.multiple_of(jnp.clip(t0 - WINDOW, 0, T - wspan), QB)
        kw = kw_ref[pl.ds(kstart, wspan), :]
        vw = vw_ref[pl.ds(kstart, wspan), :]
        tw = lax.broadcasted_iota(jnp.int32, (QB, wspan), 0)
        kwj = lax.broadcasted_iota(jnp.int32, (QB, wspan), 1)
        dist = (tw - kwj) + (t0 - kstart)
        base_w = jnp.where((dist >= 0) & (dist < WINDOW), dist.astype(F32), MASK_DIST)
        s = _bdot_nt(q, kw)
        pw = []
        for r in range(R):
            sr = head_rows(s, r) - slope2[r] * base_w
            pw.append(jnp.exp2((sr - jnp.max(sr, axis=1, keepdims=True)).astype(BF16)))
        return jnp.dot(jnp.concatenate(pw, axis=0), with_ones(vw), preferred_element_type=F32)

    def run(n_tiles):
        carry = ([jnp.full((QB, 1), NEG_INF, F32) for _ in range(R)],
                 jnp.zeros((R * QB, 2 * HEAD_DIM), F32))
        nxt = sel_scores(0)
        acc_w = window()
        for j in range(n_tiles):
            cur, nxt = nxt, (sel_scores(j + 1) if j + 1 < n_tiles else None)
            carry = sel_tile(j, cur[0], cur[1], carry)
        acc_s = carry[1]
        o_s = acc_s[:, :HEAD_DIM] / acc_s[:, HEAD_DIM:]
        o_w = acc_w[:, :HEAD_DIM] / acc_w[:, HEAD_DIM:]
        gates = _sigmoid(zs_ref[...])
        lane = lax.broadcasted_iota(jnp.int32, (QB, SMALL_WIDTH), 1)
        for r in range(R):
            gi = SMALL_GN + (g * R + r) * 3
            g_s = jnp.sum(jnp.where(lane == gi + 1, gates, 0.0), axis=1, keepdims=True)
            g_w = jnp.sum(jnp.where(lane == gi + 2, gates, 0.0), axis=1, keepdims=True)
            cs = slice(r * HEAD_DIM, (r + 1) * HEAD_DIM)
            o_ref[:, cs] = oc_ref[:, cs] + g_s * head_rows(o_s, r) + g_w * head_rows(o_w, r)

    need = (t0 + QB + tkv - 1) // tkv
    for n_tiles in range(1, T // tkv + 1):
        pl.when(need == n_tiles)(functools.partial(run, n_tiles))


def _sel_win_prompt(z3, zs3, selmask, oc, slopes):
    B, T, _ = z3.shape
    ns = T // SEL_BLOCK
    tkv = min(1024, T)
    qb = 128
    wspan = min(WINDOW + 2 * qb, T)
    gw = NSA_GROUP * HEAD_DIM

    def kv(col0):
        return pl.BlockSpec((None, T, HEAD_DIM), lambda b, g, i, sl: (b, 0, col0 // 128 + g))

    return pl.pallas_call(
        functools.partial(_sel_win_kernel, T, qb, ns, tkv, wspan),
        grid_spec=pltpu.PrefetchScalarGridSpec(
            num_scalar_prefetch=1,
            grid=(B, NSA_KV_HEADS, T // qb),
            in_specs=[
                pl.BlockSpec((None, qb, gw), lambda b, g, i, sl: (b, i, COL_QN // gw + g)),
                kv(COL_SEL), kv(COL_SEL + NSA_KV_WIDTH), kv(COL_WIN), kv(COL_WIN + NSA_KV_WIDTH),
                pl.BlockSpec((None, None, qb, ns), lambda b, g, i, sl: (b, g, i, 0)),
                pl.BlockSpec((None, qb, SMALL_WIDTH), lambda b, g, i, sl: (b, i, 0)),
                pl.BlockSpec((None, qb, gw), lambda b, g, i, sl: (b, i, g)),
            ],
            out_specs=pl.BlockSpec((None, qb, gw), lambda b, g, i, sl: (b, i, g)),
        ),
        out_shape=jax.ShapeDtypeStruct((B, T, NSA_HEADS * HEAD_DIM), F32),
        compiler_params=_cparams(("parallel", "parallel", "arbitrary")),
        name="sel_win_prompt",
    )(slopes, z3, z3, z3, z3, z3, selmask, zs3, oc)


def _merge_kernel(gma_ref, gmn_ref, oa_ref, on_ref, x_ref, w_ref, g_ref, o_ref):
    mixed = _sigmoid(gma_ref[...]) * oa_ref[...] + _sigmoid(gmn_ref[...]) * on_ref[...]
    y = jnp.dot(mixed.astype(BF16), w_ref[...], preferred_element_type=F32)
    o_ref[...] = x_ref[...] + (y * _rms_rows(y)) * g_ref[...]


def _merge_out(z2, o_a, o_n, x2, w_o, g_post1, tm):
    n = x2.shape[0]
    row = lambda c: pl.BlockSpec((tm, D_MODEL), lambda i: (i, c))
    return pl.pallas_call(
        _merge_kernel,
        grid=(n // tm,),
        in_specs=[row(COL_GMA // D_MODEL), row(COL_GMN // D_MODEL), row(0), row(0), row(0),
                  pl.BlockSpec((D_MODEL, D_MODEL), lambda i: (0, 0)),
                  pl.BlockSpec((1, D_MODEL), lambda i: (0, 0))],
        out_specs=row(0),
        out_shape=jax.ShapeDtypeStruct((n, D_MODEL), F32),
        compiler_params=_cparams(("parallel",)),
        name="merge_out",
    )(z2, z2, o_a, o_n, x2, w_o, g_post1.reshape(1, D_MODEL))


def _ffn_kernel(x_ref, g2_ref, wg_ref, wu_ref, wo_ref, gp_ref, p_ref, wpg_ref, wp_ref, o_ref, h_scr, acc_scr):
    j = pl.program_id(1)

    @pl.when(j == 0)
    def _():
        x = x_ref[...]
        h_scr[...] = ((x * _rms_rows(x)) * g2_ref[...]).astype(BF16)
        acc_scr[...] = jnp.zeros_like(acc_scr)

    h = h_scr[...]
    gt = jnp.dot(h, wg_ref[...], preferred_element_type=F32)
    up = jnp.dot(h, wu_ref[...], preferred_element_type=F32)
    acc_scr[...] += jnp.dot((_silu(gt) * up).astype(BF16), wo_ref[...], preferred_element_type=F32)

    @pl.when(j == pl.num_programs(1) - 1)
    def _():
        y = acc_scr[...]
        x = x_ref[...] + (y * _rms_rows(y)) * gp_ref[...]
        gate = _sigmoid(jnp.dot(x.astype(BF16), wpg_ref[...], preferred_element_type=F32))
        o_ref[...] = x + gate * jnp.dot(p_ref[...].astype(BF16), wp_ref[...], preferred_element_type=F32)


def _ffn_ple(x2, ple2, g_pre2, w_ffn_in, w_ffn_out, g_post2, w_ple_gate, w_ple, tm, th=512):
    n = x2.shape[0]
    nh = FFN_HIDDEN // th
    once = pl.Buffered(1)
    return pl.pallas_call(
        _ffn_kernel,
        grid=(n // tm, nh),
        in_specs=[
            pl.BlockSpec((tm, D_MODEL), lambda i, j: (i, 0)),
            pl.BlockSpec((1, D_MODEL), lambda i, j: (0, 0)),
            pl.BlockSpec((D_MODEL, th), lambda i, j: (0, j)),
            pl.BlockSpec((D_MODEL, th), lambda i, j: (0, nh + j)),
            pl.BlockSpec((th, D_MODEL), lambda i, j: (j, 0)),
            pl.BlockSpec((1, D_MODEL), lambda i, j: (0, 0)),
            pl.BlockSpec((tm, PLE_DIM), lambda i, j: (i, 0)),
            pl.BlockSpec((D_MODEL, D_MODEL), lambda i, j: (0, 0), pipeline_mode=once),
            pl.BlockSpec((PLE_DIM, D_MODEL), lambda i, j: (0, 0), pipeline_mode=once),
        ],
        out_specs=pl.BlockSpec((tm, D_MODEL), lambda i, j: (i, 0)),
        out_shape=jax.ShapeDtypeStruct((n, D_MODEL), F32),
        scratch_shapes=[pltpu.VMEM((tm, D_MODEL), BF16), pltpu.VMEM((tm, D_MODEL), F32)],
        compiler_params=_cparams(("parallel", "arbitrary")),
        name="ffn_ple",
    )(x2, g_pre2.reshape(1, D_MODEL), w_ffn_in, w_ffn_in, w_ffn_out, g_post2.reshape(1, D_MODEL),
      ple2, w_ple_gate, w_ple)


SAMPLE_ROWS = 8
KV_ROW = 2 * NSA_KV_HEADS


def _compress_pages_kernel(npg, pt_ref, *refs):
    w = refs[npg][...]
    o_ref = refs[npg + 1]
    per = PAGE_SIZE // CMP_BLOCK
    for p in range(npg):
        x = refs[p][...].reshape(per, CMP_BLOCK, KV_ROW, HEAD_DIM)
        o_ref[p * per:(p + 1) * per] = jnp.sum(x * w[None], axis=1)


def _compress_sample(cache_rows, page_table, w_cmp3, npg=16):
    Bs, n_pages = page_table.shape
    per = PAGE_SIZE // CMP_BLOCK

    def page(p):
        return pl.BlockSpec((None, PAGE_SIZE * KV_ROW, HEAD_DIM), lambda b, i, pt: (pt[b, i * npg + p], 0, 0))

    return pl.pallas_call(
        functools.partial(_compress_pages_kernel, npg),
        grid_spec=pltpu.PrefetchScalarGridSpec(
            num_scalar_prefetch=1,
            grid=(Bs, n_pages // npg),
            in_specs=[page(p) for p in range(npg)]
            + [pl.BlockSpec((CMP_BLOCK, KV_ROW, HEAD_DIM), lambda b, i, pt: (0, 0, 0))],
            out_specs=pl.BlockSpec((None, npg * per, KV_ROW, HEAD_DIM), lambda b, i, pt: (b, i, 0, 0)),
        ),
        out_shape=jax.ShapeDtypeStruct((Bs, n_pages * per, KV_ROW, HEAD_DIM), F32),
        compiler_params=_cparams(("parallel", "parallel")),
        name="compress_sample",
    )(page_table, *([cache_rows] * npg), w_cmp3)


def _sel_pages_kernel(npg, past_len, pt_ref, slopes_ref, q_ref, sel_ref, *refs):
    pages = refs[:npg]
    m_ref, l_ref, acc_ref, q_scr, selrows_scr, slope_scr = refs[npg:]
    i = pl.program_id(1)
    nrow = NSA_HEADS * SAMPLE_ROWS
    rg = NSA_GROUP * SAMPLE_ROWS
    ns_pad = sel_ref.shape[-1]

    @pl.when(i == 0)
    def _():
        for hd in range(NSA_HEADS):
            rs = slice(hd * SAMPLE_ROWS, (hd + 1) * SAMPLE_ROWS)
            q_scr[rs, :] = q_ref[:, hd * HEAD_DIM:(hd + 1) * HEAD_DIM] * (QK_SCALE * LOG2E)
            selrows_scr[rs, :] = sel_ref[hd // NSA_GROUP]
            slope_scr[rs, :] = jnp.full((SAMPLE_ROWS, HEAD_DIM), slopes_ref[hd] * LOG2E, F32)
        m_ref[...] = jnp.full(m_ref.shape, NEG_INF, F32)
        l_ref[...] = jnp.zeros(l_ref.shape, F32)
        acc_ref[...] = jnp.zeros(acc_ref.shape, F32)

    qb = q_scr[...].astype(BF16)
    selb = selrows_scr[...].astype(BF16)
    slope = slope_scr[...]
    row = lax.broadcasted_iota(jnp.int32, (nrow, PAGE_SIZE), 0)
    pos = lax.broadcasted_iota(jnp.int32, (nrow, PAGE_SIZE), 1)
    d0 = (past_len + row % SAMPLE_ROWS - pos).astype(F32)
    ob = lax.broadcasted_iota(jnp.int32, (ns_pad, PAGE_SIZE), 0)
    ol = lax.broadcasted_iota(jnp.int32, (ns_pad, PAGE_SIZE), 1) // SEL_BLOCK
    scores = []
    for p in range(npg):
        pg = i * npg + p
        flags = jnp.dot(selb, (ob == ol + pg * (PAGE_SIZE // SEL_BLOCK)).astype(BF16),
                        preferred_element_type=F32)
        distf = d0 - (pg * PAGE_SIZE).astype(F32)
        base = jnp.where((flags > 0.5) & (distf >= 0.0), distf, MASK_DIST)
        scores.append(jnp.concatenate(
            [_bdot_nt(qb[g * rg:(g + 1) * rg], pages[p][pl.ds(g, PAGE_SIZE, stride=KV_ROW), :])
             for g in range(NSA_KV_HEADS)], axis=0) - slope * base)
    s = jnp.concatenate(scores, axis=1)
    m_old = m_ref[...]
    m_new = jnp.maximum(m_old, jnp.max(s, axis=1, keepdims=True))
    pr = jnp.exp2(s - m_new[:, 0:1])
    alpha = jnp.exp2(m_old - m_new)
    l_ref[...] = alpha * l_ref[...] + jnp.sum(pr, axis=1, keepdims=True)
    prb = pr.astype(BF16)
    pv = jnp.zeros((nrow, HEAD_DIM), F32)
    for p in range(npg):
        ps = prb[:, p * PAGE_SIZE:(p + 1) * PAGE_SIZE]
        pv = pv + jnp.concatenate(
            [_bdot(ps[g * rg:(g + 1) * rg], pages[p][pl.ds(NSA_KV_HEADS + g, PAGE_SIZE, stride=KV_ROW), :])
             for g in range(NSA_KV_HEADS)], axis=0)
    acc_ref[...] = alpha * acc_ref[...] + pv
    m_ref[...] = m_new


def _sel_sample(z3s, selmask, cache_rows, page_table, slopes, past_len, npg=8):
    Bs, n_pages = page_table.shape
    nrow = NSA_HEADS * SAMPLE_ROWS
    ns_pad = selmask.shape[-1]

    def page(p):
        return pl.BlockSpec((None, PAGE_SIZE * KV_ROW, HEAD_DIM), lambda b, i, pt, sl: (pt[b, i * npg + p], 0, 0))

    part = pl.BlockSpec((None, nrow, HEAD_DIM), lambda b, i, pt, sl: (b, 0, 0))
    return pl.pallas_call(
        functools.partial(_sel_pages_kernel, npg, past_len),
        grid_spec=pltpu.PrefetchScalarGridSpec(
            num_scalar_prefetch=2,
            grid=(Bs, n_pages // npg),
            in_specs=[
                pl.BlockSpec((None, SAMPLE_ROWS, NSA_HEADS * HEAD_DIM), lambda b, i, pt, sl: (b, 0, COL_QN // 2048)),
                pl.BlockSpec((None, NSA_KV_HEADS, SAMPLE_ROWS, ns_pad), lambda b, i, pt, sl: (b, 0, 0, 0)),
            ] + [page(p) for p in range(npg)],
            out_specs=[part, part, part],
            scratch_shapes=[pltpu.VMEM((nrow, HEAD_DIM), F32), pltpu.VMEM((nrow, ns_pad), F32),
                            pltpu.VMEM((nrow, HEAD_DIM), F32)],
        ),
        out_shape=[jax.ShapeDtypeStruct((Bs, nrow, HEAD_DIM), F32)] * 3,
        compiler_params=_cparams(("parallel", "arbitrary")),
        name="sel_sample",
    )(page_table, slopes, z3s, selmask, *([cache_rows] * npg))


def _finish_sample_kernel(past_len, t_real, nnew, slopes_ref, q_ref, kst_ref, snew_ref, wnew_ref, sel_ref,
                          m_ref, l_ref, acc_ref, oc_ref, zs_ref, o_ref):
    nst = kst_ref.shape[0]
    cur = past_len // SEL_BLOCK
    gates = _sigmoid(zs_ref[...])
    t_new = lax.broadcasted_iota(jnp.int32, (SAMPLE_ROWS, nnew), 0)
    j_new = lax.broadcasted_iota(jnp.int32, (SAMPLE_ROWS, nnew), 1)
    dist_new = t_new - j_new
    ok_new = (dist_new >= 0) & (j_new < t_real)
    t_st = lax.broadcasted_iota(jnp.int32, (SAMPLE_ROWS, nst), 0)
    i_st = lax.broadcasted_iota(jnp.int32, (SAMPLE_ROWS, nst), 1)
    dist_st = t_st + nst - i_st
    ok_st = dist_st < WINDOW
    for hd in range(NSA_HEADS):
        g = hd // NSA_GROUP
        rs = slice(hd * SAMPLE_ROWS, (hd + 1) * SAMPLE_ROWS)
        kc = slice(g * HEAD_DIM, (g + 1) * HEAD_DIM)
        vc = slice(NSA_KV_WIDTH + g * HEAD_DIM, NSA_KV_WIDTH + (g + 1) * HEAD_DIM)
        sl = slopes_ref[hd] * LOG2E
        qh = (q_ref[:, hd * HEAD_DIM:(hd + 1) * HEAD_DIM] * (QK_SCALE * LOG2E)).astype(BF16)
        valid = ok_new & (sel_ref[g][:, cur:cur + 1] > 0.5)
        s = jnp.where(valid, _bdot_nt(qh, snew_ref[:, kc]) - sl * dist_new.astype(F32), NEG_INF)
        m_old = m_ref[rs, 0:1]
        m_new = jnp.maximum(m_old, jnp.max(s, axis=1, keepdims=True))
        pr = jnp.where(valid, jnp.exp2(s - m_new), 0.0)
        alpha = jnp.exp2(m_old - m_new)
        l = alpha * l_ref[rs, 0:1] + jnp.sum(pr, axis=1, keepdims=True)
        o_s = (alpha * acc_ref[rs, :] + _bdot(pr, snew_ref[:, vc])) / l
        s1 = jnp.where(ok_st, _bdot_nt(qh, kst_ref[:, kc]) - sl * dist_st.astype(F32), NEG_INF)
        s2 = jnp.where(ok_new, _bdot_nt(qh, wnew_ref[:, kc]) - sl * dist_new.astype(F32), NEG_INF)
        mw = jnp.maximum(jnp.max(s1, axis=1, keepdims=True), jnp.max(s2, axis=1, keepdims=True))
        e1 = jnp.where(ok_st, jnp.exp2(s1 - mw), 0.0)
        e2 = jnp.where(ok_new, jnp.exp2(s2 - mw), 0.0)
        den = jnp.sum(e1, axis=1, keepdims=True) + jnp.sum(e2, axis=1, keepdims=True)
        o_w = _bdot(e1 / den, kst_ref[:, vc]) + _bdot(e2 / den, wnew_ref[:, vc])
        gi = SMALL_GN + hd * 3
        cs = slice(hd * HEAD_DIM, (hd + 1) * HEAD_DIM)
        o_ref[:, cs] = oc_ref[:, cs] + gates[:, gi + 1:gi + 2] * o_s + gates[:, gi + 2:gi + 3] * o_w


def _finish_sample(z3s, zs3s, state_win2, selmask, m, l, acc, oc, slopes, past_len, t_real):
    Bs, tz, _ = z3s.shape
    nst = state_win2.shape[1]
    nrow = NSA_HEADS * SAMPLE_ROWS
    ns_pad = selmask.shape[-1]
    kvw = 2 * NSA_KV_WIDTH
    part = pl.BlockSpec((None, nrow, HEAD_DIM), lambda b, sl: (b, 0, 0))
    wide = pl.BlockSpec((None, SAMPLE_ROWS, NSA_HEADS * HEAD_DIM), lambda b, sl: (b, 0, 0))
    return pl.pallas_call(
        functools.partial(_finish_sample_kernel, past_len, t_real, tz),
        grid_spec=pltpu.PrefetchScalarGridSpec(
            num_scalar_prefetch=1,
            grid=(Bs,),
            in_specs=[
                pl.BlockSpec((None, SAMPLE_ROWS, NSA_HEADS * HEAD_DIM), lambda b, sl: (b, 0, COL_QN // 2048)),
                pl.BlockSpec((None, nst, kvw), lambda b, sl: (b, 0, 0)),
                pl.BlockSpec((None, tz, kvw), lambda b, sl: (b, 0, COL_SEL // kvw)),
                pl.BlockSpec((None, tz, kvw), lambda b, sl: (b, 0, COL_WIN // kvw)),
                pl.BlockSpec((None, NSA_KV_HEADS, SAMPLE_ROWS, ns_pad), lambda b, sl: (b, 0, 0, 0)),
                part, part, part, wide,
                pl.BlockSpec((None, SAMPLE_ROWS, SMALL_WIDTH), lambda b, sl: (b, 0, 0)),
            ],
            out_specs=wide,
        ),
        out_shape=jax.ShapeDtypeStruct((Bs, SAMPLE_ROWS, NSA_HEADS * HEAD_DIM), F32),
        compiler_params=_cparams(("parallel",)),
        name="finish_sample",
    )(slopes, z3s, state_win2, z3s, z3s, selmask, m, l, acc, oc, zs3s)


def _mix_and_ffn(z3, o_a, o_n, x3, ple3, wts, tm):
    B, T, _ = x3.shape
    n = B * T
    x1 = _merge_out(z3.reshape(n, MAIN_WIDTH), o_a.reshape(n, GDN_WIDTH), o_n.reshape(n, D_MODEL),
                    x3.reshape(n, D_MODEL), wts["w_o"], wts["g_post1"], min(tm, 256))
    x3o = _ffn_ple(x1, ple3.reshape(n, PLE_DIM), wts["g_pre2"], wts["w_ffn_in"], wts["w_ffn_out"], wts["g_post2"],
                   wts["w_ple_gate"], wts["w_ple"], tm)
    return x3o.reshape(B, T, D_MODEL)


def kernel(x_prompt, x_sample, cache_cmp_kv, cache_sel_kv, page_table, state_win_kv, state_gdn, state_conv, p_prompt, p_sample, g_pre1, w_in, conv_w, A_log, dt_bias, gdn_norm_w, w_cmp, w_o, g_post1, g_pre2, w_ffn_in, w_ffn_out, g_post2, w_ple, w_ple_gate):
    B, T, _ = x_prompt.shape
    Bs, Ts, _ = x_sample.shape
    n_pages = page_table.shape[1]
    past_len = n_pages * PAGE_SIZE
    win_buf = state_win_kv.shape[2]
    kvh = (2, NSA_KV_HEADS, HEAD_DIM)
    qkv_w = 3 * GDN_WIDTH

    wi = w_in[0]
    w_main = (wi[:, 0:8192].astype(BF16), wi[:, 13392:17488].astype(BF16), wi[:, 8224:13344].astype(BF16))
    w_small = jnp.concatenate([wi[:, 8192:8224], wi[:, 13344:13392],
                               jnp.zeros((D_MODEL, SMALL_WIDTH - 80), F32)], axis=1).astype(BF16)
    wts = dict(w_o=w_o[0].astype(BF16), g_post1=g_post1[0], g_pre2=g_pre2[0],
               w_ffn_in=w_ffn_in[0].astype(BF16), w_ffn_out=w_ffn_out[0].astype(BF16), g_post2=g_post2[0],
               w_ple=w_ple[0].astype(BF16), w_ple_gate=w_ple_gate[0].astype(BF16))
    hp = jnp.zeros((8, SMALL_WIDTH), F32).at[0, 0:GDN_HEADS].set(A_log[0]).at[1, 0:GDN_HEADS].set(dt_bias[0])
    w_cmp2 = w_cmp[0].reshape(CMP_BLOCK, 2 * NSA_KV_WIDTH)
    heads = jnp.arange(1, NSA_HEADS + 1, dtype=F32)
    slopes = jnp.exp2(-8.0 * heads / NSA_HEADS)

    z2, zs2 = _inproj(x_prompt.reshape(B * T, D_MODEL), g_pre1[0], w_main, w_small, 1024)
    z3, zs3 = z2.reshape(B, T, MAIN_WIDTH), zs2.reshape(B, T, SMALL_WIDTH)
    o_a, s_new_p = _gdn(z3, zs3, jnp.zeros((B, 8, qkv_w), F32), conv_w[0], hp, gdn_norm_w[0],
                        jnp.zeros((B, GDN_HEADS, HEAD_DIM, HEAD_DIM), F32), 256, 256)
    kvc = _compress_prompt(z3, w_cmp2)
    ns = T // SEL_BLOCK
    oc, selmask = _cmp_attn(z3, zs3, kvc, slopes, 512, T, ns, ns, 0, "cmp_attn_prompt")
    o_n = _sel_win_prompt(z3, zs3, selmask, oc, slopes)
    y_prompt = _mix_and_ffn(z3, o_a, o_n, x_prompt, p_prompt[0], wts, 512)

    tz = GDN_CHUNK
    xs = jnp.pad(x_sample, ((0, 0), (0, tz - Ts), (0, 0)))
    zs2_, zss2 = _inproj(xs.reshape(Bs * tz, D_MODEL), g_pre1[0], w_main, w_small, Bs * tz)
    z3s, zs3s = zs2_.reshape(Bs, tz, MAIN_WIDTH), zss2.reshape(Bs, tz, SMALL_WIDTH)
    conv_prev = jnp.pad(state_conv[0], ((0, 0), (8 - (CONV_WIDTH - 1), 0), (0, 0)))
    o_a_s, s_new_s = _gdn(z3s, zs3s, conv_prev, conv_w[0], hp, gdn_norm_w[0], state_gdn[0], tz, Ts, nh=GDN_HEADS)
    n_pool = cache_cmp_kv.shape[1]
    kvc_s = _compress_sample(cache_cmp_kv[0].reshape(n_pool, PAGE_SIZE * KV_ROW, HEAD_DIM), page_table,
                             w_cmp[0].reshape(CMP_BLOCK, KV_ROW, HEAD_DIM))
    kvc_s = kvc_s.reshape(Bs, kvc_s.shape[1], 2 * NSA_KV_WIDTH)
    ns_real = -(-(past_len + Ts) // SEL_BLOCK)
    ns_pad = -(-ns_real // 128) * 128
    oc_s, selmask_s = _cmp_attn(z3s, zs3s, kvc_s, slopes, SAMPLE_ROWS, SAMPLE_ROWS, ns_pad, ns_real, past_len,
                                "cmp_attn_sample")
    m_s, l_s, acc_s = _sel_sample(z3s, selmask_s, cache_sel_kv[0].reshape(n_pool, PAGE_SIZE * KV_ROW, HEAD_DIM),
                                  page_table, slopes, past_len)
    state_win2 = state_win_kv[0].reshape(Bs, win_buf, 2 * NSA_KV_WIDTH)
    o_n_s = _finish_sample(z3s, zs3s, state_win2, selmask_s, m_s, l_s, acc_s, oc_s, slopes, past_len, Ts)
    o_n_s = jnp.pad(o_n_s, ((0, 0), (0, tz - SAMPLE_ROWS), (0, 0)))
    ps = jnp.pad(p_sample[0], ((0, 0), (0, tz - Ts), (0, 0)))
    y_sample = _mix_and_ffn(z3s, o_a_s, o_n_s, xs, ps, wts, Bs * tz)[:, :Ts]

    def kv_rows(z, col, lo, hi):
        return z[:, lo:hi, col:col + 2 * NSA_KV_WIDTH].reshape((z.shape[0], hi - lo) + kvh)

    new_win_s = jnp.concatenate([state_win_kv[0][:, Ts:], kv_rows(z3s, COL_WIN, 0, Ts)], axis=1)
    return (y_prompt, y_sample,
            kv_rows(z3, COL_CMP, 0, T)[None], kv_rows(z3, COL_SEL, 0, T)[None],
            kv_rows(z3, COL_WIN, T - win_buf, T)[None], s_new_p[None],
            z3[:, T - (CONV_WIDTH - 1):, 0:qkv_w][None],
            kv_rows(z3s, COL_CMP, 0, Ts)[None], kv_rows(z3s, COL_SEL, 0, Ts)[None],
            new_win_s[None], s_new_s[None],
            z3s[:, Ts - (CONV_WIDTH - 1):Ts, 0:qkv_w][None])
```

```python
import functools

import jax
import jax.numpy as jnp
from jax import lax
from jax.experimental import pallas as pl
from jax.experimental.pallas import tpu as pltpu

F32 = jnp.float32
BF16 = jnp.bfloat16
HI = lax.Precision.HIGHEST

D_MODEL = 2048
HEAD_DIM = 128
GDN_HEADS = 16
GDN_WIDTH = 2048
CONV_WIDTH = 4
GDN_CHUNK = 64
NSA_HEADS = 16
NSA_KV_HEADS = 4
NSA_GROUP = 4
NSA_KV_WIDTH = 512
CMP_BLOCK = 32
SEL_BLOCK = 64
SEL_TOPK = 16
WINDOW = 512
PLE_DIM = 256
FFN_HIDDEN = 5632
PAGE_SIZE = 128
RMS_EPS = 1e-6
NEG_INF = -1e30
FORCE_BONUS = float(NSA_GROUP + 1)
QK_SCALE = HEAD_DIM ** -0.5
LOG2E = 1.4426950408889634
MASK_DIST = 1e32

COL_QKV = 0
COL_ZA = 6144
COL_GMA = 8192
COL_GMN = 10240
COL_QN = 12288
COL_CMP = 14336
COL_SEL = 15360
COL_WIN = 16384
MAIN_WIDTH = 17408
KV_COPY_WIDTH = 2048
SMALL_WIDTH = 128
SMALL_A, SMALL_B, SMALL_GN = 0, 16, 32

VMEM_LIMIT = 56 * 1024 * 1024


def _cparams(sem):
    return pltpu.CompilerParams(dimension_semantics=sem, vmem_limit_bytes=VMEM_LIMIT)


def _bdot(a, b):
    return jnp.dot(a.astype(BF16), b.astype(BF16), preferred_element_type=F32)


def _bdot_nt(a, b):
    return lax.dot_general(a.astype(BF16), b.astype(BF16), (((1,), (1,)), ((), ())),
                           preferred_element_type=F32)


def _hdot(a, b):
    return jnp.dot(a, b, precision=HI, preferred_element_type=F32)


def _hdot_nt(a, b):
    return lax.dot_general(a, b, (((1,), (1,)), ((), ())), precision=HI, preferred_element_type=F32)


def _sigmoid(x):
    return 1.0 / (1.0 + jnp.exp(-x))


def _silu(x):
    return x * _sigmoid(x)


def _rms_rows(x):
    return lax.rsqrt(jnp.mean(x * x, axis=-1, keepdims=True) + RMS_EPS)


def _inproj_kernel(bounds, kv_lo, x_ref, g_ref, *refs):
    w_refs = refs[:len(bounds)]
    ws_ref, z_ref, zs_ref, zkv_ref, h_scr = refs[len(bounds):]
    j = pl.program_id(1)

    @pl.when(j == 0)
    def _():
        x = x_ref[...]
        h = ((x * _rms_rows(x)) * g_ref[...]).astype(BF16)
        h_scr[...] = h
        zs_ref[...] = jnp.dot(h, ws_ref[...], preferred_element_type=F32)

    for (lo, hi), w_ref in zip(bounds, w_refs):
        @pl.when((j >= lo) & (j < hi))
        def _(w_ref=w_ref, hi=hi):
            z = jnp.dot(h_scr[...], w_ref[...], preferred_element_type=F32)
            z_ref[...] = z
            if hi > kv_lo:
                @pl.when(j >= kv_lo)
                def _():
                    zkv_ref[...] = z.astype(BF16)


def _inproj(x2, g_pre1, w_slabs, w_small, tm, tn=512):
    n = x2.shape[0]
    bounds, lo = [], 0
    for w in w_slabs:
        bounds.append((lo, lo + w.shape[1] // tn))
        lo = bounds[-1][1]
    assert lo * tn == MAIN_WIDTH

    def slab_spec(lo, hi):
        return pl.BlockSpec((D_MODEL, tn), lambda i, j: (0, jnp.clip(j - lo, 0, hi - lo - 1)))

    kv_lo, kv_n = COL_SEL // tn, KV_COPY_WIDTH // tn
    assert COL_SEL + KV_COPY_WIDTH == MAIN_WIDTH
    return pl.pallas_call(
        functools.partial(_inproj_kernel, tuple(bounds), kv_lo),
        grid=(n // tm, MAIN_WIDTH // tn),
        in_specs=[
            pl.BlockSpec((tm, D_MODEL), lambda i, j: (i, 0)),
            pl.BlockSpec((1, D_MODEL), lambda i, j: (0, 0)),
        ] + [slab_spec(lo, hi) for lo, hi in bounds] + [
            pl.BlockSpec((D_MODEL, SMALL_WIDTH), lambda i, j: (0, 0)),
        ],
        out_specs=[
            pl.BlockSpec((tm, tn), lambda i, j: (i, j)),
            pl.BlockSpec((tm, SMALL_WIDTH), lambda i, j: (i, 0)),
            pl.BlockSpec((tm, tn), lambda i, j: (i, jnp.clip(j - kv_lo, 0, kv_n - 1))),
        ],
        out_shape=[jax.ShapeDtypeStruct((n, MAIN_WIDTH), F32),
                   jax.ShapeDtypeStruct((n, SMALL_WIDTH), F32),
                   jax.ShapeDtypeStruct((n, KV_COPY_WIDTH), BF16)],
        scratch_shapes=[pltpu.VMEM((tm, D_MODEL), BF16)],
        compiler_params=_cparams(("parallel", "arbitrary")),
        name="inproj",
    )(x2, g_pre1.reshape(1, D_MODEL), *w_slabs, w_small)


def _level_mask(ii, jj, s):
    return ((ii // s) % 2 == 1) & (jj // s == ii // s - 1)


def _gdn_kernel(tb, nh, t_valid,
                q_ref, k_ref, v_ref, qh_ref, kh_ref, vh_ref, cpq_ref, cpk_ref, cpv_ref,
                cwq_ref, cwk_ref, cwv_ref, z_ref, zs_ref, hp_ref, nw_ref, s0_ref,
                o_ref, sn_ref, ext_scr, gt_scr, s_scr):
    C = GDN_CHUNK
    nch = tb // C
    hg = pl.program_id(1)
    t = pl.program_id(2)
    nt = pl.num_programs(2)

    @pl.when(t == 0)
    def _():
        s_scr[...] = s0_ref[...]

    def conv_silu(u_ref, halo_ref, cp_ref, cw_ref):
        prev = jnp.where(t == 0, cp_ref[...], halo_ref[...])
        ext_scr[0:8, :] = prev
        ext_scr[8:8 + tb, :] = u_ref[...]
        w = cw_ref[...]
        acc = ext_scr[5:5 + tb, :] * w[0:1, :]
        for j in range(1, CONV_WIDTH):
            acc = acc + ext_scr[5 + j:5 + j + tb, :] * w[j:j + 1, :]
        return _silu(acc)

    q_all = conv_silu(q_ref, qh_ref, cpq_ref, cwq_ref)
    k_all = conv_silu(k_ref, kh_ref, cpk_ref, cwk_ref)
    v_all = conv_silu(v_ref, vh_ref, cpv_ref, cwv_ref)

    zs = zs_ref[...]
    lane = lax.broadcasted_iota(jnp.int32, (GDN_CHUNK, SMALL_WIDTH), 1)
    xa = zs + hp_ref[1:2, :]
    softplus = jnp.maximum(xa, 0.0) + jnp.log1p(jnp.exp(-jnp.abs(xa)))
    g_all = -jnp.exp(hp_ref[0:1, :]) * softplus
    beta_all = _sigmoid(zs)
    if t_valid < tb:
        live = lax.broadcasted_iota(jnp.int32, (tb, 1), 0) < t_valid
        g_all = jnp.where(live, g_all, 0.0)
        beta_all = jnp.where(live, beta_all, 0.0)

    ii = lax.broadcasted_iota(jnp.int32, (C, C), 0)
    jj = lax.broadcasted_iota(jnp.int32, (C, C), 1)
    tril = (ii >= jj).astype(F32)
    eye = (ii == jj).astype(F32)
    gcum = []
    for c in range(nch):
        gc = _hdot(tril, g_all[c * C:(c + 1) * C])
        gcum.append(gc)
        gt_scr[c] = jnp.transpose(gc)

    Ls, rhss, qkds, kdts, qgs, egls = [], [], [], [], [], []
    for hl in range(nh):
        hglob = hg * nh + hl
        hs = slice(hl * HEAD_DIM, (hl + 1) * HEAD_DIM)
        q = q_all[:, hs]
        k = k_all[:, hs]
        q = (q * lax.rsqrt(jnp.sum(q * q, axis=-1, keepdims=True) + RMS_EPS)) * QK_SCALE
        k = k * lax.rsqrt(jnp.sum(k * k, axis=-1, keepdims=True) + RMS_EPS)
        for c in range(nch):
            sl = slice(c * C, (c + 1) * C)
            qc, kc, vc = q[sl], k[sl], v_all[sl, hs]
            bcol = jnp.sum(jnp.where(lane == SMALL_B + hglob, beta_all[sl], 0.0), axis=1, keepdims=True)
            gcol = jnp.sum(jnp.where(lane == SMALL_A + hglob, gcum[c], 0.0), axis=1, keepdims=True)
            grow = gt_scr[c, pl.ds(SMALL_A + hglob, 1), :]
            decay = jnp.exp(jnp.where(ii >= jj, gcol - grow, NEG_INF))
            qkk = _bdot_nt(jnp.concatenate([qc, kc], axis=0), kc)
            Ls.append(jnp.where(ii > jj, qkk[C:] * decay, 0.0) * bcol)
            eg = jnp.exp(gcol)
            g_last = grow[:, C - 1:C]
            rhss.append(jnp.concatenate([vc * bcol, kc * (bcol * eg)], axis=1))
            qkds.append(qkk[:C] * decay)
            kdts.append(jnp.transpose(kc * jnp.exp(g_last - gcol)))
            qgs.append(qc * eg)
            egls.append(jnp.exp(g_last))

    n_inst = nh * nch
    Xs = [eye - jnp.where(_level_mask(ii, jj, 1), L, 0.0) for L in Ls]
    s = 2
    while s < C:
        m = _level_mask(ii, jj, s)
        Ys = [_bdot(jnp.where(m, Ls[i], 0.0), Xs[i]) for i in range(n_inst)]
        Zs = [_bdot(Xs[i], Ys[i]) for i in range(n_inst)]
        Xs = [Xs[i] - Zs[i] for i in range(n_inst)]
        s *= 2
    sols = [_bdot(Xs[i], rhss[i]) for i in range(n_inst)]
    res = [rhss[i] - sols[i] - _hdot(Ls[i], sols[i]) for i in range(n_inst)]
    sols = [sols[i] + _bdot(Xs[i], res[i]) for i in range(n_inst)]
    NPs = [_bdot(kdts[i], sols[i]) for i in range(n_inst)]
    QOs = [_bdot(qkds[i], sols[i]) for i in range(n_inst)]

    Ss = [s_scr[hl] for hl in range(nh)]
    outs = [None] * n_inst
    for c in range(nch):
        for hl in range(nh):
            i = hl * nch + c
            S = Ss[hl]
            outs[i] = _bdot(qgs[i] - QOs[i][:, HEAD_DIM:], S) + QOs[i][:, :HEAD_DIM]
            Ss[hl] = S * egls[i] - _bdot(NPs[i][:, HEAD_DIM:], S) + NPs[i][:, :HEAD_DIM]
    nw = nw_ref[...]
    for hl in range(nh):
        hs = slice(hl * HEAD_DIM, (hl + 1) * HEAD_DIM)
        s_scr[hl] = Ss[hl]
        for c in range(nch):
            sl = slice(c * C, (c + 1) * C)
            o = outs[hl * nch + c]
            o_ref[sl, hs] = ((o * _rms_rows(o)) * nw) * _silu(z_ref[sl, hs])

    @pl.when(t == nt - 1)
    def _():
        sn_ref[...] = s_scr[...]


def _gdn(z3, zs3, conv_prev, conv_w, hp, norm_w, s0, tb, t_valid, nh=4):
    B, T, _ = z3.shape
    nt = T // tb
    hb = tb // 8
    wblk = nh * HEAD_DIM
    cq, ck, cv = COL_QKV // wblk, (COL_QKV + GDN_WIDTH) // wblk, (COL_QKV + 2 * GDN_WIDTH) // wblk
    hpg = GDN_HEADS // nh

    def main(col0):
        return pl.BlockSpec((None, tb, wblk), lambda b, h, t: (b, t, col0 + h))

    def halo(col0):
        return pl.BlockSpec((None, 8, wblk), lambda b, h, t: (b, jnp.maximum(t * hb - 1, 0), col0 + h))

    def cprev(col0):
        return pl.BlockSpec((None, 8, wblk), lambda b, h, t: (b, 0, col0 + h))

    def cw(col0):
        return pl.BlockSpec((CONV_WIDTH, wblk), lambda b, h, t: (0, col0 + h))

    state = pl.BlockSpec((None, nh, HEAD_DIM, HEAD_DIM), lambda b, h, t: (b, h, 0, 0))
    return pl.pallas_call(
        functools.partial(_gdn_kernel, tb, nh, t_valid),
        grid=(B, hpg, nt),
        in_specs=[
            main(cq), main(ck), main(cv), halo(cq), halo(ck), halo(cv),
            cprev(0), cprev(hpg), cprev(2 * hpg), cw(0), cw(hpg), cw(2 * hpg),
            main(COL_ZA // wblk),
            pl.BlockSpec((None, tb, SMALL_WIDTH), lambda b, h, t: (b, t, 0)),
            pl.BlockSpec((8, SMALL_WIDTH), lambda b, h, t: (0, 0)),
            pl.BlockSpec((1, HEAD_DIM), lambda b, h, t: (0, 0)),
            state,
        ],
        out_specs=[pl.BlockSpec((None, tb, wblk), lambda b, h, t: (b, t, h)), state],
        out_shape=[jax.ShapeDtypeStruct((B, T, GDN_WIDTH), F32),
                   jax.ShapeDtypeStruct((B, GDN_HEADS, HEAD_DIM, HEAD_DIM), F32)],
        scratch_shapes=[pltpu.VMEM((tb + 8, wblk), F32),
                        pltpu.VMEM((tb // GDN_CHUNK, SMALL_WIDTH, GDN_CHUNK), F32),
                        pltpu.VMEM((nh, HEAD_DIM, HEAD_DIM), F32)],
        compiler_params=_cparams(("parallel", "parallel", "arbitrary")),
        name="gdn",
    )(z3, z3, z3, z3, z3, z3, conv_prev, conv_prev, conv_prev, conv_w, conv_w, conv_w,
      z3, zs3, hp, norm_w.reshape(1, HEAD_DIM), s0)


def _compress_kernel(x_ref, w_ref, o_ref):
    rows = x_ref.shape[0]
    x = x_ref[...].reshape(rows // CMP_BLOCK, CMP_BLOCK, 2 * NSA_KV_WIDTH)
    o_ref[...] = jnp.sum(x * w_ref[...][None], axis=1)


def _compress_prompt(z3, w_cmp2, tc=256):
    B, T, _ = z3.shape
    nc = T // CMP_BLOCK
    return pl.pallas_call(
        _compress_kernel,
        grid=(B, T // tc),
        in_specs=[pl.BlockSpec((None, tc, 2 * NSA_KV_WIDTH), lambda b, i: (b, i, COL_CMP // 1024)),
                  pl.BlockSpec((CMP_BLOCK, 2 * NSA_KV_WIDTH), lambda b, i: (0, 0))],
        out_specs=pl.BlockSpec((None, tc // CMP_BLOCK, 2 * NSA_KV_WIDTH), lambda b, i: (b, i, 0)),
        out_shape=jax.ShapeDtypeStruct((B, nc, 2 * NSA_KV_WIDTH), F32),
        compiler_params=_cparams(("parallel", "parallel")),
        name="compress_prompt",
    )(z3, w_cmp2)


def _topk_mask(sc, k_sel, axis):
    n = sc.shape[axis]
    idxf = lax.broadcasted_iota(jnp.int32, sc.shape, axis).astype(F32)
    sel = jnp.zeros(sc.shape, F32)
    for _ in range(k_sel):
        m = jnp.max(sc, axis=axis, keepdims=True)
        idx = jnp.min(jnp.where(sc == m, idxf, float(n)), axis=axis, keepdims=True)
        hit = idxf == idx
        sel = jnp.where(hit, 1.0, sel)
        sc = jnp.where(hit, -3.0, sc)
    return sel


def _cmp_attn_kernel(ta, nc, ns, ns_real, t_base, slopes_ref, q_ref, kvc_ref, zs_ref, oc_ref, sel_ref, sc_scr):
    t0 = t_base + pl.program_id(1) * ta
    tpos = t0 + lax.broadcasted_iota(jnp.int32, (ta, nc), 0)
    cend = lax.broadcasted_iota(jnp.int32, (ta, nc), 1) * CMP_BLOCK + (CMP_BLOCK - 1)
    dist = tpos - cend
    valid = dist >= 0
    distf = dist.astype(F32)
    gates = _sigmoid(zs_ref[...])
    blocks_on_rows = ta % 128 == 0
    tok_ax, blk_ax = (1, 0) if blocks_on_rows else (0, 1)
    sshape = (ns, ta) if blocks_on_rows else (ta, ns)
    pshape = (ns, nc) if blocks_on_rows else (nc, ns)
    pool = (lax.broadcasted_iota(jnp.int32, pshape, tok_ax) // (SEL_BLOCK // CMP_BLOCK)
            == lax.broadcasted_iota(jnp.int32, pshape, blk_ax)).astype(F32)
    tq = t0 + lax.broadcasted_iota(jnp.int32, sshape, tok_ax)
    blk = lax.broadcasted_iota(jnp.int32, sshape, blk_ax)
    cur = tq // SEL_BLOCK
    forced = (blk == 0) | (blk == cur) | (blk == cur - 1)
    avail = blk * SEL_BLOCK <= tq
    for g in range(NSA_KV_HEADS):
        kc = kvc_ref[:, g * HEAD_DIM:(g + 1) * HEAD_DIM]
        vc = kvc_ref[:, NSA_KV_WIDTH + g * HEAD_DIM:NSA_KV_WIDTH + (g + 1) * HEAD_DIM]
        imp = jnp.zeros((ta, nc), F32)
        for r in range(NSA_GROUP):
            hd = g * NSA_GROUP + r
            qh = q_ref[:, hd * HEAD_DIM:(hd + 1) * HEAD_DIM] * QK_SCALE
            s = _hdot_nt(qh, kc) - slopes_ref[hd] * distf
            s = jnp.where(valid, s, NEG_INF)
            e = jnp.exp(s - jnp.max(s, axis=1, keepdims=True))
            p = jnp.where(valid, e / jnp.sum(e, axis=1, keepdims=True), 0.0)
            imp = imp + p
            gi = SMALL_GN + hd * 3
            oc_ref[:, hd * HEAD_DIM:(hd + 1) * HEAD_DIM] = gates[:, gi:gi + 1] * _bdot(p, vc)
        imps = _hdot_nt(pool, imp) if blocks_on_rows else _hdot(imp, pool)
        score = jnp.where(forced, imps + FORCE_BONUS, jnp.where(avail, imps, -1.0))
        if ns_real < ns:
            score = jnp.where(blk < ns_real, score, -2.0)
        if blocks_on_rows:
            sc_scr[:, g * ta:(g + 1) * ta] = score
        else:
            sc_scr[g * ta:(g + 1) * ta, :] = score
    sel = _topk_mask(sc_scr[...], min(SEL_TOPK, ns_real), blk_ax)
    for g in range(NSA_KV_HEADS):
        if blocks_on_rows:
            sel_ref[g] = jnp.transpose(sel[:, g * ta:(g + 1) * ta])
        else:
            sel_ref[g] = sel[g * ta:(g + 1) * ta, :]


def _cmp_attn(z3, zs3, kvc, slopes, ta, n_tok, ns, ns_real, t_base, name):
    B = z3.shape[0]
    nc = kvc.shape[1]
    return pl.pallas_call(
        functools.partial(_cmp_attn_kernel, ta, nc, ns, ns_real, t_base),
        grid_spec=pltpu.PrefetchScalarGridSpec(
            num_scalar_prefetch=1,
            grid=(B, n_tok // ta),
            in_specs=[
                pl.BlockSpec((None, ta, NSA_HEADS * HEAD_DIM), lambda b, i, sl: (b, i, COL_QN // 2048)),
                pl.BlockSpec((None, nc, 2 * NSA_KV_WIDTH), lambda b, i, sl: (b, 0, 0)),
                pl.BlockSpec((None, ta, SMALL_WIDTH), lambda b, i, sl: (b, i, 0)),
            ],
            out_specs=[
                pl.BlockSpec((None, ta, NSA_HEADS * HEAD_DIM), lambda b, i, sl: (b, i, 0)),
                pl.BlockSpec((None, NSA_KV_HEADS, ta, ns), lambda b, i, sl: (b, 0, i, 0)),
            ],
            scratch_shapes=[pltpu.VMEM((ns, NSA_KV_HEADS * ta) if ta % 128 == 0 else (NSA_KV_HEADS * ta, ns), F32)],
        ),
        out_shape=[jax.ShapeDtypeStruct((B, n_tok, NSA_HEADS * HEAD_DIM), F32),
                   jax.ShapeDtypeStruct((B, NSA_KV_HEADS, n_tok, ns), F32)],
        compiler_params=_cparams(("parallel", "parallel")),
        name=name,
    )(slopes, z3, kvc, zs3)


def _sel_win_kernel(T, QB, ns, tkv, wspan, slopes_ref, q_ref, ks_ref, vs_ref, kw_ref, vw_ref, sel_ref,
                    zs_ref, oc_ref, o_ref):
    R = NSA_GROUP
    g = pl.program_id(1)
    qb = pl.program_id(2)
    t0 = qb * QB
    q = jnp.concatenate([(q_ref[:, r * HEAD_DIM:(r + 1) * HEAD_DIM] * (QK_SCALE * LOG2E)).astype(BF16)
                         for r in range(R)], axis=0)
    slope2 = [slopes_ref[g * R + r] * LOG2E for r in range(R)]
    selb = sel_ref[...].astype(BF16)

    def head_rows(x, r):
        return x[r * QB:(r + 1) * QB]

    def spread(cols):
        return jnp.concatenate([jnp.broadcast_to(c, (QB, 2 * HEAD_DIM)) for c in cols], axis=0)

    ti = lax.broadcasted_iota(jnp.int32, (QB, tkv), 0)
    kj = lax.broadcasted_iota(jnp.int32, (QB, tkv), 1)
    d0 = (ti - kj).astype(F32)
    eb = lax.broadcasted_iota(jnp.int32, (ns, tkv), 0)
    ek = lax.broadcasted_iota(jnp.int32, (ns, tkv), 1) // SEL_BLOCK

    def sel_scores(j):
        k0 = j * tkv
        expand = (eb == ek + j * (tkv // SEL_BLOCK)).astype(BF16)
        keymask = jnp.dot(selb, expand, preferred_element_type=F32)
        distf = d0 + (t0 - k0).astype(F32)
        base = jnp.where((distf >= 0.0) & (keymask > 0.5), distf, MASK_DIST)
        return _bdot_nt(q, ks_ref[k0:k0 + tkv, :]), base

    def with_ones(v):
        return jnp.concatenate([v.astype(BF16), jnp.ones(v.shape, BF16)], axis=1)

    def sel_tile(j, s, base, carry):
        ms, acc = carry
        k0 = j * tkv
        ps, ms2, alphas = [], [], []
        for r in range(R):
            sr = head_rows(s, r) - slope2[r] * base
            m_new = jnp.maximum(ms[r], jnp.max(sr, axis=1, keepdims=True))
            ps.append(jnp.exp2((sr - m_new).astype(BF16)))
            alphas.append(jnp.exp2(ms[r] - m_new))
            ms2.append(m_new)
        pv = jnp.dot(jnp.concatenate(ps, axis=0), with_ones(vs_ref[k0:k0 + tkv, :]),
                     preferred_element_type=F32)
        return ms2, spread(alphas) * acc + pv

    def window():
        kstart = pl.multiple_of(jnp.clip(t0 - WINDOW, 0, T - wspan), QB)
        kw = kw_ref[pl.ds(kstart, wspan), :]
        vw = vw_ref[pl.ds(kstart, wspan), :]
        tw = lax.broadcasted_iota(jnp.int32, (QB, wspan), 0)
        kwj = lax.broadcasted_iota(jnp.int32, (QB, wspan), 1)
        dist = (tw - kwj) + (t0 - kstart)
        base_w = jnp.where((dist >= 0) & (dist < WINDOW), dist.astype(F32), MASK_DIST)
        s = _bdot_nt(q, kw)
        pw = []
        for r in range(R):
            sr = head_rows(s, r) - slope2[r] * base_w
            pw.append(jnp.exp2((sr - jnp.max(sr, axis=1, keepdims=True)).astype(BF16)))
        return jnp.dot(jnp.concatenate(pw, axis=0), with_ones(vw), preferred_element_type=F32)

    def run(n_tiles):
        carry = ([jnp.full((QB, 1), NEG_INF, F32) for _ in range(R)],
                 jnp.zeros((R * QB, 2 * HEAD_DIM), F32))
        nxt = sel_scores(0)
        acc_w = window()
        for j in range(n_tiles):
            cur, nxt = nxt, (sel_scores(j + 1) if j + 1 < n_tiles else None)
            carry = sel_tile(j, cur[0], cur[1], carry)
        acc_s = carry[1]
        o_s = acc_s[:, :HEAD_DIM] / acc_s[:, HEAD_DIM:]
        o_w = acc_w[:, :HEAD_DIM] / acc_w[:, HEAD_DIM:]
        gates = _sigmoid(zs_ref[...])
        lane = lax.broadcasted_iota(jnp.int32, (QB, SMALL_WIDTH), 1)
        for r in range(R):
            gi = SMALL_GN + (g * R + r) * 3
            g_s = jnp.sum(jnp.where(lane == gi + 1, gates, 0.0), axis=1, keepdims=True)
            g_w = jnp.sum(jnp.where(lane == gi + 2, gates, 0.0), axis=1, keepdims=True)
            cs = slice(r * HEAD_DIM, (r + 1) * HEAD_DIM)
            o_ref[:, cs] = oc_ref[:, cs] + g_s * head_rows(o_s, r) + g_w * head_rows(o_w, r)

    need = (t0 + QB + tkv - 1) // tkv
    for n_tiles in range(1, T // tkv + 1):
        pl.when(need == n_tiles)(functools.partial(run, n_tiles))


def _sel_win_prompt(z3, zkv3, zs3, selmask, oc, slopes):
    B, T, _ = z3.shape
    ns = T // SEL_BLOCK
    tkv = min(1024, T)
    qb = 128
    wspan = min(WINDOW + 2 * qb, T)
    gw = NSA_GROUP * HEAD_DIM

    def kv(col0):
        return pl.BlockSpec((None, T, HEAD_DIM), lambda b, g, i, sl: (b, 0, (col0 - COL_SEL) // 128 + g))

    return pl.pallas_call(
        functools.partial(_sel_win_kernel, T, qb, ns, tkv, wspan),
        grid_spec=pltpu.PrefetchScalarGridSpec(
            num_scalar_prefetch=1,
            grid=(B, NSA_KV_HEADS, T // qb),
            in_specs=[
                pl.BlockSpec((None, qb, gw), lambda b, g, i, sl: (b, i, COL_QN // gw + g)),
                kv(COL_SEL), kv(COL_SEL + NSA_KV_WIDTH), kv(COL_WIN), kv(COL_WIN + NSA_KV_WIDTH),
                pl.BlockSpec((None, None, qb, ns), lambda b, g, i, sl: (b, g, i, 0)),
                pl.BlockSpec((None, qb, SMALL_WIDTH), lambda b, g, i, sl: (b, i, 0)),
                pl.BlockSpec((None, qb, gw), lambda b, g, i, sl: (b, i, g)),
            ],
            out_specs=pl.BlockSpec((None, qb, gw), lambda b, g, i, sl: (b, i, g)),
        ),
        out_shape=jax.ShapeDtypeStruct((B, T, NSA_HEADS * HEAD_DIM), F32),
        compiler_params=_cparams(("parallel", "parallel", "arbitrary")),
        name="sel_win_prompt",
    )(slopes, z3, zkv3, zkv3, zkv3, zkv3, selmask, zs3, oc)


def _merge_kernel(gma_ref, gmn_ref, oa_ref, on_ref, x_ref, w_ref, g_ref, o_ref):
    mixed = _sigmoid(gma_ref[...]) * oa_ref[...] + _sigmoid(gmn_ref[...]) * on_ref[...]
    y = jnp.dot(mixed.astype(BF16), w_ref[...], preferred_element_type=F32)
    o_ref[...] = x_ref[...] + (y * _rms_rows(y)) * g_ref[...]


def _merge_out(z2, o_a, o_n, x2, w_o, g_post1, tm):
    n = x2.shape[0]
    row = lambda c: pl.BlockSpec((tm, D_MODEL), lambda i: (i, c))
    return pl.pallas_call(
        _merge_kernel,
        grid=(n // tm,),
        in_specs=[row(COL_GMA // D_MODEL), row(COL_GMN // D_MODEL), row(0), row(0), row(0),
                  pl.BlockSpec((D_MODEL, D_MODEL), lambda i: (0, 0)),
                  pl.BlockSpec((1, D_MODEL), lambda i: (0, 0))],
        out_specs=row(0),
        out_shape=jax.ShapeDtypeStruct((n, D_MODEL), F32),
        compiler_params=_cparams(("parallel",)),
        name="merge_out",
    )(z2, z2, o_a, o_n, x2, w_o, g_post1.reshape(1, D_MODEL))


def _ffn_kernel(x_ref, g2_ref, wg_ref, wu_ref, wo_ref, gp_ref, p_ref, wpg_ref, wp_ref, o_ref, h_scr, acc_scr):
    j = pl.program_id(1)

    @pl.when(j == 0)
    def _():
        x = x_ref[...]
        h_scr[...] = ((x * _rms_rows(x)) * g2_ref[...]).astype(BF16)
        acc_scr[...] = jnp.zeros_like(acc_scr)

    h = h_scr[...]
    gt = jnp.dot(h, wg_ref[...], preferred_element_type=F32)
    up = jnp.dot(h, wu_ref[...], preferred_element_type=F32)
    acc_scr[...] += jnp.dot((_silu(gt) * up).astype(BF16), wo_ref[...], preferred_element_type=F32)

    @pl.when(j == pl.num_programs(1) - 1)
    def _():
        y = acc_scr[...]
        x = x_ref[...] + (y * _rms_rows(y)) * gp_ref[...]
        gate = _sigmoid(jnp.dot(x.astype(BF16), wpg_ref[...], preferred_element_type=F32))
        o_ref[...] = x + gate * jnp.dot(p_ref[...].astype(BF16), wp_ref[...], preferred_element_type=F32)


def _ffn_ple(x2, ple2, g_pre2, w_ffn_in, w_ffn_out, g_post2, w_ple_gate, w_ple, tm, th=512):
    n = x2.shape[0]
    nh = FFN_HIDDEN // th
    once = pl.Buffered(1)
    return pl.pallas_call(
        _ffn_kernel,
        grid=(n // tm, nh),
        in_specs=[
            pl.BlockSpec((tm, D_MODEL), lambda i, j: (i, 0)),
            pl.BlockSpec((1, D_MODEL), lambda i, j: (0, 0)),
            pl.BlockSpec((D_MODEL, th), lambda i, j: (0, j)),
            pl.BlockSpec((D_MODEL, th), lambda i, j: (0, nh + j)),
            pl.BlockSpec((th, D_MODEL), lambda i, j: (j, 0)),
            pl.BlockSpec((1, D_MODEL), lambda i, j: (0, 0)),
            pl.BlockSpec((tm, PLE_DIM), lambda i, j: (i, 0)),
            pl.BlockSpec((D_MODEL, D_MODEL), lambda i, j: (0, 0), pipeline_mode=once),
            pl.BlockSpec((PLE_DIM, D_MODEL), lambda i, j: (0, 0), pipeline_mode=once),
        ],
        out_specs=pl.BlockSpec((tm, D_MODEL), lambda i, j: (i, 0)),
        out_shape=jax.ShapeDtypeStruct((n, D_MODEL), F32),
        scratch_shapes=[pltpu.VMEM((tm, D_MODEL), BF16), pltpu.VMEM((tm, D_MODEL), F32)],
        compiler_params=_cparams(("parallel", "arbitrary")),
        name="ffn_ple",
    )(x2, g_pre2.reshape(1, D_MODEL), w_ffn_in, w_ffn_in, w_ffn_out, g_post2.reshape(1, D_MODEL),
      ple2, w_ple_gate, w_ple)


SAMPLE_ROWS = 8
KV_ROW = 2 * NSA_KV_HEADS


def _compress_pages_kernel(npg, pt_ref, *refs):
    w = refs[npg][...]
    o_ref = refs[npg + 1]
    per = PAGE_SIZE // CMP_BLOCK
    for p in range(npg):
        x = refs[p][...].reshape(per, CMP_BLOCK, KV_ROW, HEAD_DIM)
        o_ref[p * per:(p + 1) * per] = jnp.sum(x * w[None], axis=1)


def _compress_sample(cache_rows, page_table, w_cmp3, npg=16):
    Bs, n_pages = page_table.shape
    per = PAGE_SIZE // CMP_BLOCK

    def page(p):
        return pl.BlockSpec((None, PAGE_SIZE * KV_ROW, HEAD_DIM), lambda b, i, pt: (pt[b, i * npg + p], 0, 0))

    return pl.pallas_call(
        functools.partial(_compress_pages_kernel, npg),
        grid_spec=pltpu.PrefetchScalarGridSpec(
            num_scalar_prefetch=1,
            grid=(Bs, n_pages // npg),
            in_specs=[page(p) for p in range(npg)]
            + [pl.BlockSpec((CMP_BLOCK, KV_ROW, HEAD_DIM), lambda b, i, pt: (0, 0, 0))],
            out_specs=pl.BlockSpec((None, npg * per, KV_ROW, HEAD_DIM), lambda b, i, pt: (b, i, 0, 0)),
        ),
        out_shape=jax.ShapeDtypeStruct((Bs, n_pages * per, KV_ROW, HEAD_DIM), F32),
        compiler_params=_cparams(("parallel", "parallel")),
        name="compress_sample",
    )(page_table, *([cache_rows] * npg), w_cmp3)


def _sel_pages_kernel(npg, past_len, phys_ref, lst_ref, cnt_ref, slopes_ref, q_ref, sel_ref, *refs):
    pages = refs[:npg]
    m_ref, l_ref, acc_ref, q_scr, selrows_scr, slope_scr = refs[npg:]
    b = pl.program_id(0)
    i = pl.program_id(1)
    nrow = NSA_HEADS * SAMPLE_ROWS
    rg = NSA_GROUP * SAMPLE_ROWS
    ns_pad = sel_ref.shape[-1]

    @pl.when(i == 0)
    def _():
        for hd in range(NSA_HEADS):
            rs = slice(hd * SAMPLE_ROWS, (hd + 1) * SAMPLE_ROWS)
            q_scr[rs, :] = q_ref[:, hd * HEAD_DIM:(hd + 1) * HEAD_DIM] * (QK_SCALE * LOG2E)
            selrows_scr[rs, :] = sel_ref[hd // NSA_GROUP]
            slope_scr[rs, :] = jnp.full((SAMPLE_ROWS, HEAD_DIM), slopes_ref[hd] * LOG2E, F32)
        m_ref[...] = jnp.full(m_ref.shape, NEG_INF, F32)
        l_ref[...] = jnp.zeros(l_ref.shape, F32)
        acc_ref[...] = jnp.zeros(acc_ref.shape, F32)

    @pl.when(i * npg < cnt_ref[b])
    def _():
        qb = q_scr[...].astype(BF16)
        selb = selrows_scr[...].astype(BF16)
        slope = slope_scr[...]
        row = lax.broadcasted_iota(jnp.int32, (nrow, PAGE_SIZE), 0)
        pos = lax.broadcasted_iota(jnp.int32, (nrow, PAGE_SIZE), 1)
        d0 = (past_len + row % SAMPLE_ROWS - pos).astype(F32)
        ob = lax.broadcasted_iota(jnp.int32, (ns_pad, PAGE_SIZE), 0)
        ol = lax.broadcasted_iota(jnp.int32, (ns_pad, PAGE_SIZE), 1) // SEL_BLOCK
        scores = []
        for p in range(npg):
            k = i * npg + p
            pg = lst_ref[b, k]
            flags = jnp.dot(selb, (ob == ol + pg * (PAGE_SIZE // SEL_BLOCK)).astype(BF16),
                            preferred_element_type=F32)
            distf = d0 - (pg * PAGE_SIZE).astype(F32)
            base = jnp.where((flags > 0.5) & (distf >= 0.0), distf, MASK_DIST)
            base = jnp.where(k < cnt_ref[b], base, MASK_DIST)
            scores.append(jnp.concatenate(
                [_bdot_nt(qb[g * rg:(g + 1) * rg], pages[p][pl.ds(g, PAGE_SIZE, stride=KV_ROW), :])
                 for g in range(NSA_KV_HEADS)], axis=0) - slope * base)
        s = jnp.concatenate(scores, axis=1)
        m_old = m_ref[...]
        m_new = jnp.maximum(m_old, jnp.max(s, axis=1, keepdims=True))
        pr = jnp.exp2(s - m_new[:, 0:1])
        alpha = jnp.exp2(m_old - m_new)
        l_ref[...] = alpha * l_ref[...] + jnp.sum(pr, axis=1, keepdims=True)
        prb = pr.astype(BF16)
        pv = jnp.zeros((nrow, HEAD_DIM), F32)
        for p in range(npg):
            ps = prb[:, p * PAGE_SIZE:(p + 1) * PAGE_SIZE]
            pv = pv + jnp.concatenate(
                [_bdot(ps[g * rg:(g + 1) * rg], pages[p][pl.ds(NSA_KV_HEADS + g, PAGE_SIZE, stride=KV_ROW), :])
                 for g in range(NSA_KV_HEADS)], axis=0)
        acc_ref[...] = alpha * acc_ref[...] + pv
        m_ref[...] = m_new


def _sel_sample(z3s, selmask, cache_rows, page_table, slopes, past_len, npg=8):
    Bs, n_pages = page_table.shape
    nrow = NSA_HEADS * SAMPLE_ROWS
    ns_pad = selmask.shape[-1]
    per_page = PAGE_SIZE // SEL_BLOCK
    picked = selmask[..., :n_pages * per_page].reshape(Bs, -1, n_pages, per_page).max(axis=(1, 3)) > 0.5
    cnt = picked.sum(axis=1).astype(jnp.int32)
    order = jnp.argsort(jnp.logical_not(picked), axis=1, stable=True).astype(jnp.int32)
    last = jnp.take_along_axis(order, jnp.maximum(cnt - 1, 0)[:, None], axis=1)
    lst = jnp.where(lax.broadcasted_iota(jnp.int32, order.shape, 1) < cnt[:, None], order, last)
    phys = jnp.take_along_axis(page_table, lst, axis=1)

    def page(p):
        return pl.BlockSpec((None, PAGE_SIZE * KV_ROW, HEAD_DIM),
                            lambda b, i, ph, ls, ct, sl: (ph[b, i * npg + p], 0, 0))

    part = pl.BlockSpec((None, nrow, HEAD_DIM), lambda b, i, ph, ls, ct, sl: (b, 0, 0))
    return pl.pallas_call(
        functools.partial(_sel_pages_kernel, npg, past_len),
        grid_spec=pltpu.PrefetchScalarGridSpec(
            num_scalar_prefetch=4,
            grid=(Bs, n_pages // npg),
            in_specs=[
                pl.BlockSpec((None, SAMPLE_ROWS, NSA_HEADS * HEAD_DIM),
                             lambda b, i, ph, ls, ct, sl: (b, 0, COL_QN // 2048)),
                pl.BlockSpec((None, NSA_KV_HEADS, SAMPLE_ROWS, ns_pad), lambda b, i, ph, ls, ct, sl: (b, 0, 0, 0)),
            ] + [page(p) for p in range(npg)],
            out_specs=[part, part, part],
            scratch_shapes=[pltpu.VMEM((nrow, HEAD_DIM), F32), pltpu.VMEM((nrow, ns_pad), F32),
                            pltpu.VMEM((nrow, HEAD_DIM), F32)],
        ),
        out_shape=[jax.ShapeDtypeStruct((Bs, nrow, HEAD_DIM), F32)] * 3,
        compiler_params=_cparams(("parallel", "arbitrary")),
        name="sel_sample",
    )(phys, lst, cnt, slopes, z3s, selmask, *([cache_rows] * npg))


def _finish_sample_kernel(past_len, t_real, nnew, slopes_ref, q_ref, kst_ref, snew_ref, wnew_ref, sel_ref,
                          m_ref, l_ref, acc_ref, oc_ref, zs_ref, o_ref):
    nst = kst_ref.shape[0]
    cur = past_len // SEL_BLOCK
    gates = _sigmoid(zs_ref[...])
    t_new = lax.broadcasted_iota(jnp.int32, (SAMPLE_ROWS, nnew), 0)
    j_new = lax.broadcasted_iota(jnp.int32, (SAMPLE_ROWS, nnew), 1)
    dist_new = t_new - j_new
    ok_new = (dist_new >= 0) & (j_new < t_real)
    t_st = lax.broadcasted_iota(jnp.int32, (SAMPLE_ROWS, nst), 0)
    i_st = lax.broadcasted_iota(jnp.int32, (SAMPLE_ROWS, nst), 1)
    dist_st = t_st + nst - i_st
    ok_st = dist_st < WINDOW
    for hd in range(NSA_HEADS):
        g = hd // NSA_GROUP
        rs = slice(hd * SAMPLE_ROWS, (hd + 1) * SAMPLE_ROWS)
        kc = slice(g * HEAD_DIM, (g + 1) * HEAD_DIM)
        vc = slice(NSA_KV_WIDTH + g * HEAD_DIM, NSA_KV_WIDTH + (g + 1) * HEAD_DIM)
        sl = slopes_ref[hd] * LOG2E
        qh = (q_ref[:, hd * HEAD_DIM:(hd + 1) * HEAD_DIM] * (QK_SCALE * LOG2E)).astype(BF16)
        valid = ok_new & (sel_ref[g][:, cur:cur + 1] > 0.5)
        s = jnp.where(valid, _bdot_nt(qh, snew_ref[:, kc]) - sl * dist_new.astype(F32), NEG_INF)
        m_old = m_ref[rs, 0:1]
        m_new = jnp.maximum(m_old, jnp.max(s, axis=1, keepdims=True))
        pr = jnp.where(valid, jnp.exp2(s - m_new), 0.0)
        alpha = jnp.exp2(m_old - m_new)
        l = alpha * l_ref[rs, 0:1] + jnp.sum(pr, axis=1, keepdims=True)
        o_s = (alpha * acc_ref[rs, :] + _bdot(pr, snew_ref[:, vc])) / l
        s1 = jnp.where(ok_st, _bdot_nt(qh, kst_ref[:, kc]) - sl * dist_st.astype(F32), NEG_INF)
        s2 = jnp.where(ok_new, _bdot_nt(qh, wnew_ref[:, kc]) - sl * dist_new.astype(F32), NEG_INF)
        mw = jnp.maximum(jnp.max(s1, axis=1, keepdims=True), jnp.max(s2, axis=1, keepdims=True))
        e1 = jnp.where(ok_st, jnp.exp2(s1 - mw), 0.0)
        e2 = jnp.where(ok_new, jnp.exp2(s2 - mw), 0.0)
        den = jnp.sum(e1, axis=1, keepdims=True) + jnp.sum(e2, axis=1, keepdims=True)
        o_w = _bdot(e1 / den, kst_ref[:, vc]) + _bdot(e2 / den, wnew_ref[:, vc])
        gi = SMALL_GN + hd * 3
        cs = slice(hd * HEAD_DIM, (hd + 1) * HEAD_DIM)
        o_ref[:, cs] = oc_ref[:, cs] + gates[:, gi + 1:gi + 2] * o_s + gates[:, gi + 2:gi + 3] * o_w


def _finish_sample(z3s, zs3s, state_win2, selmask, m, l, acc, oc, slopes, past_len, t_real):
    Bs, tz, _ = z3s.shape
    nst = state_win2.shape[1]
    nrow = NSA_HEADS * SAMPLE_ROWS
    ns_pad = selmask.shape[-1]
    kvw = 2 * NSA_KV_WIDTH
    part = pl.BlockSpec((None, nrow, HEAD_DIM), lambda b, sl: (b, 0, 0))
    wide = pl.BlockSpec((None, SAMPLE_ROWS, NSA_HEADS * HEAD_DIM), lambda b, sl: (b, 0, 0))
    return pl.pallas_call(
        functools.partial(_finish_sample_kernel, past_len, t_real, tz),
        grid_spec=pltpu.PrefetchScalarGridSpec(
            num_scalar_prefetch=1,
            grid=(Bs,),
            in_specs=[
                pl.BlockSpec((None, SAMPLE_ROWS, NSA_HEADS * HEAD_DIM), lambda b, sl: (b, 0, COL_QN // 2048)),
                pl.BlockSpec((None, nst, kvw), lambda b, sl: (b, 0, 0)),
                pl.BlockSpec((None, tz, kvw), lambda b, sl: (b, 0, COL_SEL // kvw)),
                pl.BlockSpec((None, tz, kvw), lambda b, sl: (b, 0, COL_WIN // kvw)),
                pl.BlockSpec((None, NSA_KV_HEADS, SAMPLE_ROWS, ns_pad), lambda b, sl: (b, 0, 0, 0)),
                part, part, part, wide,
                pl.BlockSpec((None, SAMPLE_ROWS, SMALL_WIDTH), lambda b, sl: (b, 0, 0)),
            ],
            out_specs=wide,
        ),
        out_shape=jax.ShapeDtypeStruct((Bs, SAMPLE_ROWS, NSA_HEADS * HEAD_DIM), F32),
        compiler_params=_cparams(("parallel",)),
        name="finish_sample",
    )(slopes, z3s, state_win2, z3s, z3s, selmask, m, l, acc, oc, zs3s)


def _mix_and_ffn(z3, o_a, o_n, x3, ple3, wts, tm):
    B, T, _ = x3.shape
    n = B * T
    x1 = _merge_out(z3.reshape(n, MAIN_WIDTH), o_a.reshape(n, GDN_WIDTH), o_n.reshape(n, D_MODEL),
                    x3.reshape(n, D_MODEL), wts["w_o"], wts["g_post1"], min(tm, 256))
    x3o = _ffn_ple(x1, ple3.reshape(n, PLE_DIM), wts["g_pre2"], wts["w_ffn_in"], wts["w_ffn_out"], wts["g_post2"],
                   wts["w_ple_gate"], wts["w_ple"], tm)
    return x3o.reshape(B, T, D_MODEL)


def kernel(x_prompt, x_sample, cache_cmp_kv, cache_sel_kv, page_table, state_win_kv, state_gdn, state_conv, p_prompt, p_sample, g_pre1, w_in, conv_w, A_log, dt_bias, gdn_norm_w, w_cmp, w_o, g_post1, g_pre2, w_ffn_in, w_ffn_out, g_post2, w_ple, w_ple_gate):
    B, T, _ = x_prompt.shape
    Bs, Ts, _ = x_sample.shape
    n_pages = page_table.shape[1]
    past_len = n_pages * PAGE_SIZE
    win_buf = state_win_kv.shape[2]
    kvh = (2, NSA_KV_HEADS, HEAD_DIM)
    qkv_w = 3 * GDN_WIDTH

    wi = w_in[0]
    w_main = (wi[:, 0:8192].astype(BF16), wi[:, 13392:17488].astype(BF16), wi[:, 8224:13344].astype(BF16))
    w_small = jnp.concatenate([wi[:, 8192:8224], wi[:, 13344:13392],
                               jnp.zeros((D_MODEL, SMALL_WIDTH - 80), F32)], axis=1).astype(BF16)
    wts = dict(w_o=w_o[0].astype(BF16), g_post1=g_post1[0], g_pre2=g_pre2[0],
               w_ffn_in=w_ffn_in[0].astype(BF16), w_ffn_out=w_ffn_out[0].astype(BF16), g_post2=g_post2[0],
               w_ple=w_ple[0].astype(BF16), w_ple_gate=w_ple_gate[0].astype(BF16))
    hp = jnp.zeros((8, SMALL_WIDTH), F32).at[0, 0:GDN_HEADS].set(A_log[0]).at[1, 0:GDN_HEADS].set(dt_bias[0])
    w_cmp2 = w_cmp[0].reshape(CMP_BLOCK, 2 * NSA_KV_WIDTH)
    heads = jnp.arange(1, NSA_HEADS + 1, dtype=F32)
    slopes = jnp.exp2(-8.0 * heads / NSA_HEADS)

    z2, zs2, zkv2 = _inproj(x_prompt.reshape(B * T, D_MODEL), g_pre1[0], w_main, w_small, 1024)
    z3, zs3 = z2.reshape(B, T, MAIN_WIDTH), zs2.reshape(B, T, SMALL_WIDTH)
    o_a, s_new_p = _gdn(z3, zs3, jnp.zeros((B, 8, qkv_w), F32), conv_w[0], hp, gdn_norm_w[0],
                        jnp.zeros((B, GDN_HEADS, HEAD_DIM, HEAD_DIM), F32), 256, 256)
    kvc = _compress_prompt(z3, w_cmp2)
    ns = T // SEL_BLOCK
    oc, selmask = _cmp_attn(z3, zs3, kvc, slopes, 512, T, ns, ns, 0, "cmp_attn_prompt")
    o_n = _sel_win_prompt(z3, zkv2.reshape(B, T, KV_COPY_WIDTH), zs3, selmask, oc, slopes)
    y_prompt = _mix_and_ffn(z3, o_a, o_n, x_prompt, p_prompt[0], wts, 512)

    tz = GDN_CHUNK
    xs = jnp.pad(x_sample, ((0, 0), (0, tz - Ts), (0, 0)))
    zs2_, zss2, _ = _inproj(xs.reshape(Bs * tz, D_MODEL), g_pre1[0], w_main, w_small, Bs * tz)
    z3s, zs3s = zs2_.reshape(Bs, tz, MAIN_WIDTH), zss2.reshape(Bs, tz, SMALL_WIDTH)
    conv_prev = jnp.pad(state_conv[0], ((0, 0), (8 - (CONV_WIDTH - 1), 0), (0, 0)))
    o_a_s, s_new_s = _gdn(z3s, zs3s, conv_prev, conv_w[0], hp, gdn_norm_w[0], state_gdn[0], tz, Ts, nh=GDN_HEADS)
    n_pool = cache_cmp_kv.shape[1]
    kvc_s = _compress_sample(cache_cmp_kv[0].reshape(n_pool, PAGE_SIZE * KV_ROW, HEAD_DIM), page_table,
                             w_cmp[0].reshape(CMP_BLOCK, KV_ROW, HEAD_DIM))
    kvc_s = kvc_s.reshape(Bs, kvc_s.shape[1], 2 * NSA_KV_WIDTH)
    ns_real = -(-(past_len + Ts) // SEL_BLOCK)
    ns_pad = -(-ns_real // 128) * 128
    oc_s, selmask_s = _cmp_attn(z3s, zs3s, kvc_s, slopes, SAMPLE_ROWS, SAMPLE_ROWS, ns_pad, ns_real, past_len,
                                "cmp_attn_sample")
    m_s, l_s, acc_s = _sel_sample(z3s, selmask_s, cache_sel_kv[0].reshape(n_pool, PAGE_SIZE * KV_ROW, HEAD_DIM),
                                  page_table, slopes, past_len)
    state_win2 = state_win_kv[0].reshape(Bs, win_buf, 2 * NSA_KV_WIDTH)
    o_n_s = _finish_sample(z3s, zs3s, state_win2, selmask_s, m_s, l_s, acc_s, oc_s, slopes, past_len, Ts)
    o_n_s = jnp.pad(o_n_s, ((0, 0), (0, tz - SAMPLE_ROWS), (0, 0)))
    ps = jnp.pad(p_sample[0], ((0, 0), (0, tz - Ts), (0, 0)))
    y_sample = _mix_and_ffn(z3s, o_a_s, o_n_s, xs, ps, wts, Bs * tz)[:, :Ts]

    def kv_rows(z, col, lo, hi):
        return z[:, lo:hi, col:col + 2 * NSA_KV_WIDTH].reshape((z.shape[0], hi - lo) + kvh)

    new_win_s = jnp.concatenate([state_win_kv[0][:, Ts:], kv_rows(z3s, COL_WIN, 0, Ts)], axis=1)
    return (y_prompt, y_sample,
            kv_rows(z3, COL_CMP, 0, T)[None], kv_rows(z3, COL_SEL, 0, T)[None],
            kv_rows(z3, COL_WIN, T - win_buf, T)[None], s_new_p[None],
            z3[:, T - (CONV_WIDTH - 1):, 0:qkv_w][None],
            kv_rows(z3s, COL_CMP, 0, Ts)[None], kv_rows(z3s, COL_SEL, 0, Ts)[None],
            new_win_s[None], s_new_s[None],
            z3s[:, Ts - (CONV_WIDTH - 1):Ts, 0:qkv_w][None])
```

```python
import functools

import jax
import jax.numpy as jnp
from jax import lax
from jax.experimental import pallas as pl
from jax.experimental.pallas import tpu as pltpu

F32 = jnp.float32
BF16 = jnp.bfloat16
HI = lax.Precision.HIGHEST

D_MODEL = 2048
HEAD_DIM = 128
GDN_HEADS = 16
GDN_WIDTH = 2048
CONV_WIDTH = 4
GDN_CHUNK = 64
NSA_HEADS = 16
NSA_KV_HEADS = 4
NSA_GROUP = 4
NSA_KV_WIDTH = 512
CMP_BLOCK = 32
SEL_BLOCK = 64
SEL_TOPK = 16
WINDOW = 512
PLE_DIM = 256
FFN_HIDDEN = 5632
PAGE_SIZE = 128
RMS_EPS = 1e-6
NEG_INF = -1e30
FORCE_BONUS = float(NSA_GROUP + 1)
QK_SCALE = HEAD_DIM ** -0.5
LOG2E = 1.4426950408889634
MASK_DIST = 1e32

COL_QKV = 0
COL_ZA = 6144
COL_GMA = 8192
COL_GMN = 10240
COL_QN = 12288
COL_CMP = 14336
COL_SEL = 15360
COL_WIN = 16384
MAIN_WIDTH = 17408
KV_COPY_WIDTH = 2048
SMALL_WIDTH = 128
SMALL_A, SMALL_B, SMALL_GN = 0, 16, 32

VMEM_LIMIT = 56 * 1024 * 1024


def _cparams(sem):
    return pltpu.CompilerParams(dimension_semantics=sem, vmem_limit_bytes=VMEM_LIMIT)


def _bdot(a, b):
    return jnp.dot(a.astype(BF16), b.astype(BF16), preferred_element_type=F32)


def _bdot_nt(a, b):
    return lax.dot_general(a.astype(BF16), b.astype(BF16), (((1,), (1,)), ((), ())),
                           preferred_element_type=F32)


def _hdot(a, b):
    return jnp.dot(a, b, precision=HI, preferred_element_type=F32)


def _hdot_nt(a, b):
    return lax.dot_general(a, b, (((1,), (1,)), ((), ())), precision=HI, preferred_element_type=F32)


def _sigmoid(x):
    return 1.0 / (1.0 + jnp.exp(-x))


def _silu(x):
    return x * _sigmoid(x)


def _rms_rows(x):
    return lax.rsqrt(jnp.mean(x * x, axis=-1, keepdims=True) + RMS_EPS)


W_TILE = 512
W_SEGMENTS = ((0, 0), (COL_GMA // W_TILE, 13392), (COL_QN // W_TILE, 8224))


def _wprep_kernel(a_ref, b_ref, o_ref):
    j = pl.program_id(0)
    for s, (lo, src) in enumerate(W_SEGMENTS):
        hi = W_SEGMENTS[s + 1][0] if s + 1 < len(W_SEGMENTS) else MAIN_WIDTH // W_TILE
        off = src % W_TILE

        @pl.when((j >= lo) & (j < hi))
        def _(off=off):
            if off == 0:
                o_ref[...] = a_ref[...].astype(BF16)
            else:
                both = jnp.concatenate([a_ref[...], b_ref[...]], axis=1)
                o_ref[...] = pltpu.roll(both, 2 * W_TILE - off, axis=1)[:, :W_TILE].astype(BF16)


def _wprep(w_in2):
    def src_block(j, second):
        blk = j
        for lo, src in W_SEGMENTS[1:]:
            blk = jnp.where(j >= lo, j - lo + src // W_TILE, blk)
        if second:
            blk = jnp.where(j >= W_SEGMENTS[1][0], blk + 1, 0)
        return (0, blk)

    return pl.pallas_call(
        _wprep_kernel,
        grid=(MAIN_WIDTH // W_TILE,),
        in_specs=[pl.BlockSpec((D_MODEL, W_TILE), lambda j: src_block(j, False)),
                  pl.BlockSpec((D_MODEL, W_TILE), lambda j: src_block(j, True))],
        out_specs=pl.BlockSpec((D_MODEL, W_TILE), lambda j: (0, j)),
        out_shape=jax.ShapeDtypeStruct((D_MODEL, MAIN_WIDTH), BF16),
        compiler_params=_cparams(("parallel",)),
        name="wprep",
    )(w_in2, w_in2)


def _inproj_kernel(bounds, kv_lo, x_ref, g_ref, *refs):
    w_refs = refs[:len(bounds)]
    ws_ref, z_ref, zs_ref, zkv_ref, h_scr = refs[len(bounds):]
    j = pl.program_id(1)

    @pl.when(j == 0)
    def _():
        x = x_ref[...]
        h = ((x * _rms_rows(x)) * g_ref[...]).astype(BF16)
        h_scr[...] = h
        zs_ref[...] = jnp.dot(h, ws_ref[...], preferred_element_type=F32)

    for (lo, hi), w_ref in zip(bounds, w_refs):
        @pl.when((j >= lo) & (j < hi))
        def _(w_ref=w_ref, hi=hi):
            z = jnp.dot(h_scr[...], w_ref[...], preferred_element_type=F32)
            z_ref[...] = z
            if hi > kv_lo:
                @pl.when(j >= kv_lo)
                def _():
                    zkv_ref[...] = z.astype(BF16)


def _inproj(x2, g_pre1, w_slabs, w_small, tm, tn=1024):
    n = x2.shape[0]
    bounds, lo = [], 0
    for w in w_slabs:
        bounds.append((lo, lo + w.shape[1] // tn))
        lo = bounds[-1][1]
    assert lo * tn == MAIN_WIDTH

    def slab_spec(lo, hi):
        return pl.BlockSpec((D_MODEL, tn), lambda i, j: (0, jnp.clip(j - lo, 0, hi - lo - 1)))

    kv_lo, kv_n = COL_SEL // tn, KV_COPY_WIDTH // tn
    assert COL_SEL + KV_COPY_WIDTH == MAIN_WIDTH
    return pl.pallas_call(
        functools.partial(_inproj_kernel, tuple(bounds), kv_lo),
        grid=(n // tm, MAIN_WIDTH // tn),
        in_specs=[
            pl.BlockSpec((tm, D_MODEL), lambda i, j: (i, 0)),
            pl.BlockSpec((1, D_MODEL), lambda i, j: (0, 0)),
        ] + [slab_spec(lo, hi) for lo, hi in bounds] + [
            pl.BlockSpec((D_MODEL, SMALL_WIDTH), lambda i, j: (0, 0)),
        ],
        out_specs=[
            pl.BlockSpec((tm, tn), lambda i, j: (i, j)),
            pl.BlockSpec((tm, SMALL_WIDTH), lambda i, j: (i, 0)),
            pl.BlockSpec((tm, tn), lambda i, j: (i, jnp.clip(j - kv_lo, 0, kv_n - 1))),
        ],
        out_shape=[jax.ShapeDtypeStruct((n, MAIN_WIDTH), F32),
                   jax.ShapeDtypeStruct((n, SMALL_WIDTH), F32),
                   jax.ShapeDtypeStruct((n, KV_COPY_WIDTH), BF16)],
        scratch_shapes=[pltpu.VMEM((tm, D_MODEL), BF16)],
        compiler_params=_cparams(("parallel", "arbitrary")),
        name="inproj",
    )(x2, g_pre1.reshape(1, D_MODEL), *w_slabs, w_small)


def _level_mask(ii, jj, s):
    return ((ii // s) % 2 == 1) & (jj // s == ii // s - 1)


def _gdn_kernel(tb, nh, t_valid,
                q_ref, k_ref, v_ref, qh_ref, kh_ref, vh_ref, cpq_ref, cpk_ref, cpv_ref,
                cwq_ref, cwk_ref, cwv_ref, z_ref, gm_ref, zs_ref, hp_ref, nw_ref, s0_ref,
                o_ref, sn_ref, ext_scr, gt_scr, s_scr):
    C = GDN_CHUNK
    nch = tb // C
    hg = pl.program_id(1)
    t = pl.program_id(2)
    nt = pl.num_programs(2)

    @pl.when(t == 0)
    def _():
        s_scr[...] = s0_ref[...]

    def conv_silu(u_ref, halo_ref, cp_ref, cw_ref):
        prev = jnp.where(t == 0, cp_ref[...], halo_ref[...])
        ext_scr[0:8, :] = prev
        ext_scr[8:8 + tb, :] = u_ref[...]
        w = cw_ref[...]
        acc = ext_scr[5:5 + tb, :] * w[0:1, :]
        for j in range(1, CONV_WIDTH):
            acc = acc + ext_scr[5 + j:5 + j + tb, :] * w[j:j + 1, :]
        return _silu(acc)

    q_all = conv_silu(q_ref, qh_ref, cpq_ref, cwq_ref)
    k_all = conv_silu(k_ref, kh_ref, cpk_ref, cwk_ref)
    v_all = conv_silu(v_ref, vh_ref, cpv_ref, cwv_ref)

    zs = zs_ref[...]
    lane = lax.broadcasted_iota(jnp.int32, (GDN_CHUNK, SMALL_WIDTH), 1)
    xa = zs + hp_ref[1:2, :]
    softplus = jnp.maximum(xa, 0.0) + jnp.log1p(jnp.exp(-jnp.abs(xa)))
    g_all = -jnp.exp(hp_ref[0:1, :]) * softplus
    beta_all = _sigmoid(zs)
    if t_valid < tb:
        live = lax.broadcasted_iota(jnp.int32, (tb, 1), 0) < t_valid
        g_all = jnp.where(live, g_all, 0.0)
        beta_all = jnp.where(live, beta_all, 0.0)

    ii = lax.broadcasted_iota(jnp.int32, (C, C), 0)
    jj = lax.broadcasted_iota(jnp.int32, (C, C), 1)
    tril = (ii >= jj).astype(F32)
    eye = (ii == jj).astype(F32)
    gcum = []
    for c in range(nch):
        gc = _hdot(tril, g_all[c * C:(c + 1) * C])
        gcum.append(gc)
        gt_scr[c] = jnp.transpose(gc)

    Ls, rhss, qkds, kdts, qgs, egls = [], [], [], [], [], []
    for hl in range(nh):
        hglob = hg * nh + hl
        hs = slice(hl * HEAD_DIM, (hl + 1) * HEAD_DIM)
        q = q_all[:, hs]
        k = k_all[:, hs]
        q = (q * lax.rsqrt(jnp.sum(q * q, axis=-1, keepdims=True) + RMS_EPS)) * QK_SCALE
        k = k * lax.rsqrt(jnp.sum(k * k, axis=-1, keepdims=True) + RMS_EPS)
        for c in range(nch):
            sl = slice(c * C, (c + 1) * C)
            qc, kc, vc = q[sl], k[sl], v_all[sl, hs]
            bcol = jnp.sum(jnp.where(lane == SMALL_B + hglob, beta_all[sl], 0.0), axis=1, keepdims=True)
            gcol = jnp.sum(jnp.where(lane == SMALL_A + hglob, gcum[c], 0.0), axis=1, keepdims=True)
            grow = gt_scr[c, pl.ds(SMALL_A + hglob, 1), :]
            decay = jnp.exp(jnp.where(ii >= jj, gcol - grow, NEG_INF))
            qkk = _bdot_nt(jnp.concatenate([qc, kc], axis=0), kc)
            Ls.append(jnp.where(ii > jj, qkk[C:] * decay, 0.0) * bcol)
            eg = jnp.exp(gcol)
            g_last = grow[:, C - 1:C]
            rhss.append(jnp.concatenate([vc * bcol, kc * (bcol * eg)], axis=1))
            qkds.append(qkk[:C] * decay)
            kdts.append(jnp.transpose(kc * jnp.exp(g_last - gcol)))
            qgs.append(qc * eg)
            egls.append(jnp.exp(g_last))

    n_inst = nh * nch
    Xs = [eye - jnp.where(_level_mask(ii, jj, 1), L, 0.0) for L in Ls]
    s = 2
    while s < C:
        m = _level_mask(ii, jj, s)
        Ys = [_bdot(jnp.where(m, Ls[i], 0.0), Xs[i]) for i in range(n_inst)]
        Zs = [_bdot(Xs[i], Ys[i]) for i in range(n_inst)]
        Xs = [Xs[i] - Zs[i] for i in range(n_inst)]
        s *= 2
    sols = [_bdot(Xs[i], rhss[i]) for i in range(n_inst)]
    res = [rhss[i] - sols[i] - _hdot(Ls[i], sols[i]) for i in range(n_inst)]
    sols = [sols[i] + _bdot(Xs[i], res[i]) for i in range(n_inst)]
    NPs = [_bdot(kdts[i], sols[i]) for i in range(n_inst)]
    QOs = [_bdot(qkds[i], sols[i]) for i in range(n_inst)]

    Ss = [s_scr[hl] for hl in range(nh)]
    outs = [None] * n_inst
    for c in range(nch):
        for hl in range(nh):
            i = hl * nch + c
            S = Ss[hl]
            outs[i] = _bdot(qgs[i] - QOs[i][:, HEAD_DIM:], S) + QOs[i][:, :HEAD_DIM]
            Ss[hl] = S * egls[i] - _bdot(NPs[i][:, HEAD_DIM:], S) + NPs[i][:, :HEAD_DIM]
    nw = nw_ref[...]
    for hl in range(nh):
        hs = slice(hl * HEAD_DIM, (hl + 1) * HEAD_DIM)
        s_scr[hl] = Ss[hl]
        for c in range(nch):
            sl = slice(c * C, (c + 1) * C)
            o = outs[hl * nch + c]
            o_ref[sl, hs] = _sigmoid(gm_ref[sl, hs]) * (((o * _rms_rows(o)) * nw) * _silu(z_ref[sl, hs]))

    @pl.when(t == nt - 1)
    def _():
        sn_ref[...] = s_scr[...]


def _gdn(z3, zs3, conv_prev, conv_w, hp, norm_w, s0, tb, t_valid, nh=4):
    B, T, _ = z3.shape
    nt = T // tb
    hb = tb // 8
    wblk = nh * HEAD_DIM
    cq, ck, cv = COL_QKV // wblk, (COL_QKV + GDN_WIDTH) // wblk, (COL_QKV + 2 * GDN_WIDTH) // wblk
    hpg = GDN_HEADS // nh

    def main(col0):
        return pl.BlockSpec((None, tb, wblk), lambda b, h, t: (b, t, col0 + h))

    def halo(col0):
        return pl.BlockSpec((None, 8, wblk), lambda b, h, t: (b, jnp.maximum(t * hb - 1, 0), col0 + h))

    def cprev(col0):
        return pl.BlockSpec((None, 8, wblk), lambda b, h, t: (b, 0, col0 + h))

    def cw(col0):
        return pl.BlockSpec((CONV_WIDTH, wblk), lambda b, h, t: (0, col0 + h))

    state = pl.BlockSpec((None, nh, HEAD_DIM, HEAD_DIM), lambda b, h, t: (b, h, 0, 0))
    return pl.pallas_call(
        functools.partial(_gdn_kernel, tb, nh, t_valid),
        grid=(B, hpg, nt),
        in_specs=[
            main(cq), main(ck), main(cv), halo(cq), halo(ck), halo(cv),
            cprev(0), cprev(hpg), cprev(2 * hpg), cw(0), cw(hpg), cw(2 * hpg),
            main(COL_ZA // wblk), main(COL_GMA // wblk),
            pl.BlockSpec((None, tb, SMALL_WIDTH), lambda b, h, t: (b, t, 0)),
            pl.BlockSpec((8, SMALL_WIDTH), lambda b, h, t: (0, 0)),
            pl.BlockSpec((1, HEAD_DIM), lambda b, h, t: (0, 0)),
            state,
        ],
        out_specs=[pl.BlockSpec((None, tb, wblk), lambda b, h, t: (b, t, h)), state],
        out_shape=[jax.ShapeDtypeStruct((B, T, GDN_WIDTH), F32),
                   jax.ShapeDtypeStruct((B, GDN_HEADS, HEAD_DIM, HEAD_DIM), F32)],
        scratch_shapes=[pltpu.VMEM((tb + 8, wblk), F32),
                        pltpu.VMEM((tb // GDN_CHUNK, SMALL_WIDTH, GDN_CHUNK), F32),
                        pltpu.VMEM((nh, HEAD_DIM, HEAD_DIM), F32)],
        compiler_params=_cparams(("parallel", "parallel", "arbitrary")),
        name="gdn",
    )(z3, z3, z3, z3, z3, z3, conv_prev, conv_prev, conv_prev, conv_w, conv_w, conv_w,
      z3, z3, zs3, hp, norm_w.reshape(1, HEAD_DIM), s0)


def _compress_kernel(x_ref, w_ref, o_ref):
    rows = x_ref.shape[0]
    x = x_ref[...].reshape(rows // CMP_BLOCK, CMP_BLOCK, 2 * NSA_KV_WIDTH)
    o_ref[...] = jnp.sum(x * w_ref[...][None], axis=1)


def _compress_prompt(z3, w_cmp2, tc=256):
    B, T, _ = z3.shape
    nc = T // CMP_BLOCK
    return pl.pallas_call(
        _compress_kernel,
        grid=(B, T // tc),
        in_specs=[pl.BlockSpec((None, tc, 2 * NSA_KV_WIDTH), lambda b, i: (b, i, COL_CMP // 1024)),
                  pl.BlockSpec((CMP_BLOCK, 2 * NSA_KV_WIDTH), lambda b, i: (0, 0))],
        out_specs=pl.BlockSpec((None, tc // CMP_BLOCK, 2 * NSA_KV_WIDTH), lambda b, i: (b, i, 0)),
        out_shape=jax.ShapeDtypeStruct((B, nc, 2 * NSA_KV_WIDTH), F32),
        compiler_params=_cparams(("parallel", "parallel")),
        name="compress_prompt",
    )(z3, w_cmp2)


def _topk_mask(sc, k_sel, axis):
    n = sc.shape[axis]
    idxf = lax.broadcasted_iota(jnp.int32, sc.shape, axis).astype(F32)
    sel = jnp.zeros(sc.shape, F32)
    for _ in range(k_sel):
        m = jnp.max(sc, axis=axis, keepdims=True)
        idx = jnp.min(jnp.where(sc == m, idxf, float(n)), axis=axis, keepdims=True)
        hit = idxf == idx
        sel = jnp.where(hit, 1.0, sel)
        sc = jnp.where(hit, -3.0, sc)
    return sel


def _cmp_attn_kernel(ta, nc, ns, ns_real, t_base, slopes_ref, q_ref, kvc_ref, zs_ref, oc_ref, sel_ref, sc_scr):
    t0 = t_base + pl.program_id(1) * ta
    tpos = t0 + lax.broadcasted_iota(jnp.int32, (ta, nc), 0)
    cend = lax.broadcasted_iota(jnp.int32, (ta, nc), 1) * CMP_BLOCK + (CMP_BLOCK - 1)
    dist = tpos - cend
    valid = dist >= 0
    distf = dist.astype(F32)
    gates = _sigmoid(zs_ref[...])
    blocks_on_rows = ta % 128 == 0
    tok_ax, blk_ax = (1, 0) if blocks_on_rows else (0, 1)
    sshape = (ns, ta) if blocks_on_rows else (ta, ns)
    pshape = (ns, nc) if blocks_on_rows else (nc, ns)
    pool = (lax.broadcasted_iota(jnp.int32, pshape, tok_ax) // (SEL_BLOCK // CMP_BLOCK)
            == lax.broadcasted_iota(jnp.int32, pshape, blk_ax)).astype(F32)
    tq = t0 + lax.broadcasted_iota(jnp.int32, sshape, tok_ax)
    blk = lax.broadcasted_iota(jnp.int32, sshape, blk_ax)
    cur = tq // SEL_BLOCK
    forced = (blk == 0) | (blk == cur) | (blk == cur - 1)
    avail = blk * SEL_BLOCK <= tq
    for g in range(NSA_KV_HEADS):
        kc = kvc_ref[:, g * HEAD_DIM:(g + 1) * HEAD_DIM]
        vc = kvc_ref[:, NSA_KV_WIDTH + g * HEAD_DIM:NSA_KV_WIDTH + (g + 1) * HEAD_DIM]
        imp = jnp.zeros((ta, nc), F32)
        for r in range(NSA_GROUP):
            hd = g * NSA_GROUP + r
            qh = q_ref[:, hd * HEAD_DIM:(hd + 1) * HEAD_DIM] * QK_SCALE
            s = _hdot_nt(qh, kc) - slopes_ref[hd] * distf
            s = jnp.where(valid, s, NEG_INF)
            e = jnp.exp(s - jnp.max(s, axis=1, keepdims=True))
            p = jnp.where(valid, e / jnp.sum(e, axis=1, keepdims=True), 0.0)
            imp = imp + p
            gi = SMALL_GN + hd * 3
            oc_ref[:, hd * HEAD_DIM:(hd + 1) * HEAD_DIM] = gates[:, gi:gi + 1] * _bdot(p, vc)
        imps = _hdot_nt(pool, imp) if blocks_on_rows else _hdot(imp, pool)
        score = jnp.where(forced, imps + FORCE_BONUS, jnp.where(avail, imps, -1.0))
        if ns_real < ns:
            score = jnp.where(blk < ns_real, score, -2.0)
        if blocks_on_rows:
            sc_scr[:, g * ta:(g + 1) * ta] = score
        else:
            sc_scr[g * ta:(g + 1) * ta, :] = score
    sel = _topk_mask(sc_scr[...], min(SEL_TOPK, ns_real), blk_ax)
    for g in range(NSA_KV_HEADS):
        if blocks_on_rows:
            sel_ref[g] = jnp.transpose(sel[:, g * ta:(g + 1) * ta])
        else:
            sel_ref[g] = sel[g * ta:(g + 1) * ta, :]


def _cmp_attn(z3, zs3, kvc, slopes, ta, n_tok, ns, ns_real, t_base, name):
    B = z3.shape[0]
    nc = kvc.shape[1]
    return pl.pallas_call(
        functools.partial(_cmp_attn_kernel, ta, nc, ns, ns_real, t_base),
        grid_spec=pltpu.PrefetchScalarGridSpec(
            num_scalar_prefetch=1,
            grid=(B, n_tok // ta),
            in_specs=[
                pl.BlockSpec((None, ta, NSA_HEADS * HEAD_DIM), lambda b, i, sl: (b, i, COL_QN // 2048)),
                pl.BlockSpec((None, nc, 2 * NSA_KV_WIDTH), lambda b, i, sl: (b, 0, 0)),
                pl.BlockSpec((None, ta, SMALL_WIDTH), lambda b, i, sl: (b, i, 0)),
            ],
            out_specs=[
                pl.BlockSpec((None, ta, NSA_HEADS * HEAD_DIM), lambda b, i, sl: (b, i, 0)),
                pl.BlockSpec((None, NSA_KV_HEADS, ta, ns), lambda b, i, sl: (b, 0, i, 0)),
            ],
            scratch_shapes=[pltpu.VMEM((ns, NSA_KV_HEADS * ta) if ta % 128 == 0 else (NSA_KV_HEADS * ta, ns), F32)],
        ),
        out_shape=[jax.ShapeDtypeStruct((B, n_tok, NSA_HEADS * HEAD_DIM), F32),
                   jax.ShapeDtypeStruct((B, NSA_KV_HEADS, n_tok, ns), F32)],
        compiler_params=_cparams(("parallel", "parallel")),
        name=name,
    )(slopes, z3, kvc, zs3)


def _sel_win_kernel(T, QB, ns, tkv, wspan, slopes_ref, q_ref, ks_ref, vs_ref, kw_ref, vw_ref, sel_ref,
                    zs_ref, oc_ref, ga_ref, gmn_ref, o_ref):
    R = NSA_GROUP
    g = pl.program_id(1)
    qb = pl.program_id(2)
    t0 = qb * QB
    q = jnp.concatenate([(q_ref[:, r * HEAD_DIM:(r + 1) * HEAD_DIM] * (QK_SCALE * LOG2E)).astype(BF16)
                         for r in range(R)], axis=0)
    slope2 = [slopes_ref[g * R + r] * LOG2E for r in range(R)]
    selb = sel_ref[...].astype(BF16)

    def head_rows(x, r):
        return x[r * QB:(r + 1) * QB]

    def spread(cols):
        return jnp.concatenate([jnp.broadcast_to(c, (QB, 2 * HEAD_DIM)) for c in cols], axis=0)

    ti = lax.broadcasted_iota(jnp.int32, (QB, tkv), 0)
    kj = lax.broadcasted_iota(jnp.int32, (QB, tkv), 1)
    d0 = (ti - kj).astype(F32)
    eb = lax.broadcasted_iota(jnp.int32, (ns, tkv), 0)
    ek = lax.broadcasted_iota(jnp.int32, (ns, tkv), 1) // SEL_BLOCK

    def sel_scores(j):
        k0 = j * tkv
        expand = (eb == ek + j * (tkv // SEL_BLOCK)).astype(BF16)
        keymask = jnp.dot(selb, expand, preferred_element_type=F32)
        distf = d0 + (t0 - k0).astype(F32)
        base = jnp.where((distf >= 0.0) & (keymask > 0.5), distf, MASK_DIST)
        return _bdot_nt(q, ks_ref[k0:k0 + tkv, :]), base

    def with_ones(v):
        return jnp.concatenate([v.astype(BF16), jnp.ones(v.shape, BF16)], axis=1)

    def sel_tile(j, s, base, carry):
        ms, acc = carry
        k0 = j * tkv
        ps, ms2, alphas = [], [], []
        for r in range(R):
            sr = head_rows(s, r) - slope2[r] * base
            m_new = jnp.maximum(ms[r], jnp.max(sr, axis=1, keepdims=True))
            ps.append(jnp.exp2((sr - m_new).astype(BF16)))
            alphas.append(jnp.exp2(ms[r] - m_new))
            ms2.append(m_new)
        pv = jnp.dot(jnp.concatenate(ps, axis=0), with_ones(vs_ref[k0:k0 + tkv, :]),
                     preferred_element_type=F32)
        return ms2, spread(alphas) * acc + pv

    def window():
        kstart = pl.multiple_of(jnp.clip(t0 - WINDOW, 0, T - wspan), QB)
        kw = kw_ref[pl.ds(kstart, wspan), :]
        vw = vw_ref[pl.ds(kstart, wspan), :]
        tw = lax.broadcasted_iota(jnp.int32, (QB, wspan), 0)
        kwj = lax.broadcasted_iota(jnp.int32, (QB, wspan), 1)
        dist = (tw - kwj) + (t0 - kstart)
        base_w = jnp.where((dist >= 0) & (dist < WINDOW), dist.astype(F32), MASK_DIST)
        s = _bdot_nt(q, kw)
        pw = []
        for r in range(R):
            sr = head_rows(s, r) - slope2[r] * base_w
            pw.append(jnp.exp2((sr - jnp.max(sr, axis=1, keepdims=True)).astype(BF16)))
        return jnp.dot(jnp.concatenate(pw, axis=0), with_ones(vw), preferred_element_type=F32)

    def run(n_tiles):
        carry = ([jnp.full((QB, 1), NEG_INF, F32) for _ in range(R)],
                 jnp.zeros((R * QB, 2 * HEAD_DIM), F32))
        nxt = sel_scores(0)
        acc_w = window()
        for j in range(n_tiles):
            cur, nxt = nxt, (sel_scores(j + 1) if j + 1 < n_tiles else None)
            carry = sel_tile(j, cur[0], cur[1], carry)
        acc_s = carry[1]
        o_s = acc_s[:, :HEAD_DIM] / acc_s[:, HEAD_DIM:]
        o_w = acc_w[:, :HEAD_DIM] / acc_w[:, HEAD_DIM:]
        gates = _sigmoid(zs_ref[...])
        lane = lax.broadcasted_iota(jnp.int32, (QB, SMALL_WIDTH), 1)
        for r in range(R):
            gi = SMALL_GN + (g * R + r) * 3
            g_s = jnp.sum(jnp.where(lane == gi + 1, gates, 0.0), axis=1, keepdims=True)
            g_w = jnp.sum(jnp.where(lane == gi + 2, gates, 0.0), axis=1, keepdims=True)
            cs = slice(r * HEAD_DIM, (r + 1) * HEAD_DIM)
            o_n = oc_ref[:, cs] + g_s * head_rows(o_s, r) + g_w * head_rows(o_w, r)
            o_ref[:, cs] = (ga_ref[:, cs] + _sigmoid(gmn_ref[:, cs]) * o_n).astype(BF16)

    need = (t0 + QB + tkv - 1) // tkv
    for n_tiles in range(1, T // tkv + 1):
        pl.when(need == n_tiles)(functools.partial(run, n_tiles))


def _sel_win_prompt(z3, zkv3, zs3, selmask, oc, ga, slopes):
    B, T, _ = z3.shape
    ns = T // SEL_BLOCK
    tkv = min(1024, T)
    qb = 128
    wspan = min(WINDOW + 2 * qb, T)
    gw = NSA_GROUP * HEAD_DIM

    def kv(col0):
        return pl.BlockSpec((None, T, HEAD_DIM), lambda b, g, i, sl: (b, 0, (col0 - COL_SEL) // 128 + g))

    return pl.pallas_call(
        functools.partial(_sel_win_kernel, T, qb, ns, tkv, wspan),
        grid_spec=pltpu.PrefetchScalarGridSpec(
            num_scalar_prefetch=1,
            grid=(B, NSA_KV_HEADS, T // qb),
            in_specs=[
                pl.BlockSpec((None, qb, gw), lambda b, g, i, sl: (b, i, COL_QN // gw + g)),
                kv(COL_SEL), kv(COL_SEL + NSA_KV_WIDTH), kv(COL_WIN), kv(COL_WIN + NSA_KV_WIDTH),
                pl.BlockSpec((None, None, qb, ns), lambda b, g, i, sl: (b, g, i, 0)),
                pl.BlockSpec((None, qb, SMALL_WIDTH), lambda b, g, i, sl: (b, i, 0)),
                pl.BlockSpec((None, qb, gw), lambda b, g, i, sl: (b, i, g)),
                pl.BlockSpec((None, qb, gw), lambda b, g, i, sl: (b, i, g)),
                pl.BlockSpec((None, qb, gw), lambda b, g, i, sl: (b, i, COL_GMN // gw + g)),
            ],
            out_specs=pl.BlockSpec((None, qb, gw), lambda b, g, i, sl: (b, i, g)),
        ),
        out_shape=jax.ShapeDtypeStruct((B, T, D_MODEL), BF16),
        compiler_params=_cparams(("parallel", "parallel", "arbitrary")),
        name="sel_win_prompt",
    )(slopes, z3, zkv3, zkv3, zkv3, zkv3, selmask, zs3, oc, ga, z3)


def _merge_kernel(mixed_ref, x_ref, w_ref, g_ref, o_ref):
    y = jnp.dot(mixed_ref[...].astype(BF16), w_ref[...], preferred_element_type=F32)
    o_ref[...] = x_ref[...] + (y * _rms_rows(y)) * g_ref[...]


def _merge_out(mixed2, x2, w_o, g_post1, tm):
    n = x2.shape[0]
    row = pl.BlockSpec((tm, D_MODEL), lambda i: (i, 0))
    return pl.pallas_call(
        _merge_kernel,
        grid=(n // tm,),
        in_specs=[row, row,
                  pl.BlockSpec((D_MODEL, D_MODEL), lambda i: (0, 0)),
                  pl.BlockSpec((1, D_MODEL), lambda i: (0, 0))],
        out_specs=row,
        out_shape=jax.ShapeDtypeStruct((n, D_MODEL), F32),
        compiler_params=_cparams(("parallel",)),
        name="merge_out",
    )(mixed2, x2, w_o, g_post1.reshape(1, D_MODEL))


def _ffn_kernel(x_ref, g2_ref, wg_ref, wu_ref, wo_ref, gp_ref, p_ref, wpg_ref, wp_ref, o_ref, h_scr, acc_scr):
    j = pl.program_id(1)

    @pl.when(j == 0)
    def _():
        x = x_ref[...]
        h_scr[...] = ((x * _rms_rows(x)) * g2_ref[...]).astype(BF16)
        acc_scr[...] = jnp.zeros_like(acc_scr)

    h = h_scr[...]
    gt = jnp.dot(h, wg_ref[...], preferred_element_type=F32)
    up = jnp.dot(h, wu_ref[...], preferred_element_type=F32)
    acc_scr[...] += jnp.dot((_silu(gt) * up).astype(BF16), wo_ref[...], preferred_element_type=F32)

    @pl.when(j == pl.num_programs(1) - 1)
    def _():
        y = acc_scr[...]
        x = x_ref[...] + (y * _rms_rows(y)) * gp_ref[...]
        gate = _sigmoid(jnp.dot(x.astype(BF16), wpg_ref[...], preferred_element_type=F32))
        o_ref[...] = x + gate * jnp.dot(p_ref[...].astype(BF16), wp_ref[...], preferred_element_type=F32)


def _ffn_ple(x2, ple2, g_pre2, w_ffn_in, w_ffn_out, g_post2, w_ple_gate, w_ple, tm, th=512):
    n = x2.shape[0]
    nh = FFN_HIDDEN // th
    once = pl.Buffered(1)
    return pl.pallas_call(
        _ffn_kernel,
        grid=(n // tm, nh),
        in_specs=[
            pl.BlockSpec((tm, D_MODEL), lambda i, j: (i, 0)),
            pl.BlockSpec((1, D_MODEL), lambda i, j: (0, 0)),
            pl.BlockSpec((D_MODEL, th), lambda i, j: (0, j)),
            pl.BlockSpec((D_MODEL, th), lambda i, j: (0, nh + j)),
            pl.BlockSpec((th, D_MODEL), lambda i, j: (j, 0)),
            pl.BlockSpec((1, D_MODEL), lambda i, j: (0, 0)),
            pl.BlockSpec((tm, PLE_DIM), lambda i, j: (i, 0)),
            pl.BlockSpec((D_MODEL, D_MODEL), lambda i, j: (0, 0), pipeline_mode=once),
            pl.BlockSpec((PLE_DIM, D_MODEL), lambda i, j: (0, 0), pipeline_mode=once),
        ],
        out_specs=pl.BlockSpec((tm, D_MODEL), lambda i, j: (i, 0)),
        out_shape=jax.ShapeDtypeStruct((n, D_MODEL), F32),
        scratch_shapes=[pltpu.VMEM((tm, D_MODEL), BF16), pltpu.VMEM((tm, D_MODEL), F32)],
        compiler_params=_cparams(("parallel", "arbitrary")),
        name="ffn_ple",
    )(x2, g_pre2.reshape(1, D_MODEL), w_ffn_in, w_ffn_in, w_ffn_out, g_post2.reshape(1, D_MODEL),
      ple2, w_ple_gate, w_ple)


SAMPLE_ROWS = 8
KV_ROW = 2 * NSA_KV_HEADS


def _compress_pages_kernel(npg, pt_ref, *refs):
    w = refs[npg][...]
    o_ref = refs[npg + 1]
    per = PAGE_SIZE // CMP_BLOCK
    for p in range(npg):
        x = refs[p][...].reshape(per, CMP_BLOCK, KV_ROW, HEAD_DIM)
        o_ref[p * per:(p + 1) * per] = jnp.sum(x * w[None], axis=1)


def _compress_sample(cache_rows, page_table, w_cmp3, npg=16):
    Bs, n_pages = page_table.shape
    per = PAGE_SIZE // CMP_BLOCK

    def page(p):
        return pl.BlockSpec((None, PAGE_SIZE * KV_ROW, HEAD_DIM), lambda b, i, pt: (pt[b, i * npg + p], 0, 0))

    return pl.pallas_call(
        functools.partial(_compress_pages_kernel, npg),
        grid_spec=pltpu.PrefetchScalarGridSpec(
            num_scalar_prefetch=1,
            grid=(Bs, n_pages // npg),
            in_specs=[page(p) for p in range(npg)]
            + [pl.BlockSpec((CMP_BLOCK, KV_ROW, HEAD_DIM), lambda b, i, pt: (0, 0, 0))],
            out_specs=pl.BlockSpec((None, npg * per, KV_ROW, HEAD_DIM), lambda b, i, pt: (b, i, 0, 0)),
        ),
        out_shape=jax.ShapeDtypeStruct((Bs, n_pages * per, KV_ROW, HEAD_DIM), F32),
        compiler_params=_cparams(("parallel", "parallel")),
        name="compress_sample",
    )(page_table, *([cache_rows] * npg), w_cmp3)


def _sel_pages_kernel(npg, past_len, phys_ref, lst_ref, cnt_ref, slopes_ref, q_ref, sel_ref, *refs):
    pages = refs[:npg]
    m_ref, l_ref, acc_ref, q_scr, selrows_scr, slope_scr = refs[npg:]
    b = pl.program_id(0)
    i = pl.program_id(1)
    nrow = NSA_HEADS * SAMPLE_ROWS
    rg = NSA_GROUP * SAMPLE_ROWS
    ns_pad = sel_ref.shape[-1]

    @pl.when(i == 0)
    def _():
        for hd in range(NSA_HEADS):
            rs = slice(hd * SAMPLE_ROWS, (hd + 1) * SAMPLE_ROWS)
            q_scr[rs, :] = q_ref[:, hd * HEAD_DIM:(hd + 1) * HEAD_DIM] * (QK_SCALE * LOG2E)
            selrows_scr[rs, :] = sel_ref[hd // NSA_GROUP]
            slope_scr[rs, :] = jnp.full((SAMPLE_ROWS, HEAD_DIM), slopes_ref[hd] * LOG2E, F32)
        m_ref[...] = jnp.full(m_ref.shape, NEG_INF, F32)
        l_ref[...] = jnp.zeros(l_ref.shape, F32)
        acc_ref[...] = jnp.zeros(acc_ref.shape, F32)

    @pl.when(i * npg < cnt_ref[b])
    def _():
        qb = q_scr[...].astype(BF16)
        selb = selrows_scr[...].astype(BF16)
        slope = slope_scr[...]
        row = lax.broadcasted_iota(jnp.int32, (nrow, PAGE_SIZE), 0)
        pos = lax.broadcasted_iota(jnp.int32, (nrow, PAGE_SIZE), 1)
        d0 = (past_len + row % SAMPLE_ROWS - pos).astype(F32)
        ob = lax.broadcasted_iota(jnp.int32, (ns_pad, PAGE_SIZE), 0)
        ol = lax.broadcasted_iota(jnp.int32, (ns_pad, PAGE_SIZE), 1) // SEL_BLOCK
        scores = []
        for p in range(npg):
            k = i * npg + p
            pg = lst_ref[b, k]
            flags = jnp.dot(selb, (ob == ol + pg * (PAGE_SIZE // SEL_BLOCK)).astype(BF16),
                            preferred_element_type=F32)
            distf = d0 - (pg * PAGE_SIZE).astype(F32)
            base = jnp.where((flags > 0.5) & (distf >= 0.0), distf, MASK_DIST)
            base = jnp.where(k < cnt_ref[b], base, MASK_DIST)
            scores.append(jnp.concatenate(
                [_bdot_nt(qb[g * rg:(g + 1) * rg], pages[p][pl.ds(g, PAGE_SIZE, stride=KV_ROW), :])
                 for g in range(NSA_KV_HEADS)], axis=0) - slope * base)
        s = jnp.concatenate(scores, axis=1)
        m_old = m_ref[...]
        m_new = jnp.maximum(m_old, jnp.max(s, axis=1, keepdims=True))
        pr = jnp.exp2(s - m_new[:, 0:1])
        alpha = jnp.exp2(m_old - m_new)
        l_ref[...] = alpha * l_ref[...] + jnp.sum(pr, axis=1, keepdims=True)
        prb = pr.astype(BF16)
        pv = jnp.zeros((nrow, HEAD_DIM), F32)
        for p in range(npg):
            ps = prb[:, p * PAGE_SIZE:(p + 1) * PAGE_SIZE]
            pv = pv + jnp.concatenate(
                [_bdot(ps[g * rg:(g + 1) * rg], pages[p][pl.ds(NSA_KV_HEADS + g, PAGE_SIZE, stride=KV_ROW), :])
                 for g in range(NSA_KV_HEADS)], axis=0)
        acc_ref[...] = alpha * acc_ref[...] + pv
        m_ref[...] = m_new


def _sel_sample(z3s, selmask, cache_rows, page_table, slopes, past_len, npg=8):
    Bs, n_pages = page_table.shape
    nrow = NSA_HEADS * SAMPLE_ROWS
    ns_pad = selmask.shape[-1]
    per_page = PAGE_SIZE // SEL_BLOCK
    picked = selmask[..., :n_pages * per_page].reshape(Bs, -1, n_pages, per_page).max(axis=(1, 3)) > 0.5
    cnt = picked.sum(axis=1).astype(jnp.int32)
    order = jnp.argsort(jnp.logical_not(picked), axis=1, stable=True).astype(jnp.int32)
    last = jnp.take_along_axis(order, jnp.maximum(cnt - 1, 0)[:, None], axis=1)
    lst = jnp.where(lax.broadcasted_iota(jnp.int32, order.shape, 1) < cnt[:, None], order, last)
    phys = jnp.take_along_axis(page_table, lst, axis=1)

    def page(p):
        return pl.BlockSpec((None, PAGE_SIZE * KV_ROW, HEAD_DIM),
                            lambda b, i, ph, ls, ct, sl: (ph[b, i * npg + p], 0, 0))

    part = pl.BlockSpec((None, nrow, HEAD_DIM), lambda b, i, ph, ls, ct, sl: (b, 0, 0))
    return pl.pallas_call(
        functools.partial(_sel_pages_kernel, npg, past_len),
        grid_spec=pltpu.PrefetchScalarGridSpec(
            num_scalar_prefetch=4,
            grid=(Bs, n_pages // npg),
            in_specs=[
                pl.BlockSpec((None, SAMPLE_ROWS, NSA_HEADS * HEAD_DIM),
                             lambda b, i, ph, ls, ct, sl: (b, 0, COL_QN // 2048)),
                pl.BlockSpec((None, NSA_KV_HEADS, SAMPLE_ROWS, ns_pad), lambda b, i, ph, ls, ct, sl: (b, 0, 0, 0)),
            ] + [page(p) for p in range(npg)],
            out_specs=[part, part, part],
            scratch_shapes=[pltpu.VMEM((nrow, HEAD_DIM), F32), pltpu.VMEM((nrow, ns_pad), F32),
                            pltpu.VMEM((nrow, HEAD_DIM), F32)],
        ),
        out_shape=[jax.ShapeDtypeStruct((Bs, nrow, HEAD_DIM), F32)] * 3,
        compiler_params=_cparams(("parallel", "arbitrary")),
        name="sel_sample",
    )(phys, lst, cnt, slopes, z3s, selmask, *([cache_rows] * npg))


def _finish_sample_kernel(past_len, t_real, nnew, slopes_ref, q_ref, kst_ref, snew_ref, wnew_ref, sel_ref,
                          m_ref, l_ref, acc_ref, oc_ref, zs_ref, ga_ref, gmn_ref, o_ref):
    nst = kst_ref.shape[0]
    cur = past_len // SEL_BLOCK
    gates = _sigmoid(zs_ref[...])
    t_new = lax.broadcasted_iota(jnp.int32, (SAMPLE_ROWS, nnew), 0)
    j_new = lax.broadcasted_iota(jnp.int32, (SAMPLE_ROWS, nnew), 1)
    dist_new = t_new - j_new
    ok_new = (dist_new >= 0) & (j_new < t_real)
    t_st = lax.broadcasted_iota(jnp.int32, (SAMPLE_ROWS, nst), 0)
    i_st = lax.broadcasted_iota(jnp.int32, (SAMPLE_ROWS, nst), 1)
    dist_st = t_st + nst - i_st
    ok_st = dist_st < WINDOW
    for hd in range(NSA_HEADS):
        g = hd // NSA_GROUP
        rs = slice(hd * SAMPLE_ROWS, (hd + 1) * SAMPLE_ROWS)
        kc = slice(g * HEAD_DIM, (g + 1) * HEAD_DIM)
        vc = slice(NSA_KV_WIDTH + g * HEAD_DIM, NSA_KV_WIDTH + (g + 1) * HEAD_DIM)
        sl = slopes_ref[hd] * LOG2E
        qh = (q_ref[:, hd * HEAD_DIM:(hd + 1) * HEAD_DIM] * (QK_SCALE * LOG2E)).astype(BF16)
        valid = ok_new & (sel_ref[g][:, cur:cur + 1] > 0.5)
        s = jnp.where(valid, _bdot_nt(qh, snew_ref[:, kc]) - sl * dist_new.astype(F32), NEG_INF)
        m_old = m_ref[rs, 0:1]
        m_new = jnp.maximum(m_old, jnp.max(s, axis=1, keepdims=True))
        pr = jnp.where(valid, jnp.exp2(s - m_new), 0.0)
        alpha = jnp.exp2(m_old - m_new)
        l = alpha * l_ref[rs, 0:1] + jnp.sum(pr, axis=1, keepdims=True)
        o_s = (alpha * acc_ref[rs, :] + _bdot(pr, snew_ref[:, vc])) / l
        s1 = jnp.where(ok_st, _bdot_nt(qh, kst_ref[:, kc]) - sl * dist_st.astype(F32), NEG_INF)
        s2 = jnp.where(ok_new, _bdot_nt(qh, wnew_ref[:, kc]) - sl * dist_new.astype(F32), NEG_INF)
        mw = jnp.maximum(jnp.max(s1, axis=1, keepdims=True), jnp.max(s2, axis=1, keepdims=True))
        e1 = jnp.where(ok_st, jnp.exp2(s1 - mw), 0.0)
        e2 = jnp.where(ok_new, jnp.exp2(s2 - mw), 0.0)
        den = jnp.sum(e1, axis=1, keepdims=True) + jnp.sum(e2, axis=1, keepdims=True)
        o_w = _bdot(e1 / den, kst_ref[:, vc]) + _bdot(e2 / den, wnew_ref[:, vc])
        gi = SMALL_GN + hd * 3
        cs = slice(hd * HEAD_DIM, (hd + 1) * HEAD_DIM)
        o_n = oc_ref[:, cs] + gates[:, gi + 1:gi + 2] * o_s + gates[:, gi + 2:gi + 3] * o_w
        o_ref[:, cs] = ga_ref[:, cs] + _sigmoid(gmn_ref[:, cs]) * o_n


def _finish_sample(z3s, zs3s, state_win2, selmask, m, l, acc, oc, ga, slopes, past_len, t_real):
    Bs, tz, _ = z3s.shape
    nst = state_win2.shape[1]
    nrow = NSA_HEADS * SAMPLE_ROWS
    ns_pad = selmask.shape[-1]
    kvw = 2 * NSA_KV_WIDTH
    part = pl.BlockSpec((None, nrow, HEAD_DIM), lambda b, sl: (b, 0, 0))
    wide = pl.BlockSpec((None, SAMPLE_ROWS, NSA_HEADS * HEAD_DIM), lambda b, sl: (b, 0, 0))
    return pl.pallas_call(
        functools.partial(_finish_sample_kernel, past_len, t_real, tz),
        grid_spec=pltpu.PrefetchScalarGridSpec(
            num_scalar_prefetch=1,
            grid=(Bs,),
            in_specs=[
                pl.BlockSpec((None, SAMPLE_ROWS, NSA_HEADS * HEAD_DIM), lambda b, sl: (b, 0, COL_QN // 2048)),
                pl.BlockSpec((None, nst, kvw), lambda b, sl: (b, 0, 0)),
                pl.BlockSpec((None, tz, kvw), lambda b, sl: (b, 0, COL_SEL // kvw)),
                pl.BlockSpec((None, tz, kvw), lambda b, sl: (b, 0, COL_WIN // kvw)),
                pl.BlockSpec((None, NSA_KV_HEADS, SAMPLE_ROWS, ns_pad), lambda b, sl: (b, 0, 0, 0)),
                part, part, part, wide,
                pl.BlockSpec((None, SAMPLE_ROWS, SMALL_WIDTH), lambda b, sl: (b, 0, 0)),
                wide,
                pl.BlockSpec((None, SAMPLE_ROWS, D_MODEL), lambda b, sl: (b, 0, COL_GMN // D_MODEL)),
            ],
            out_specs=wide,
        ),
        out_shape=jax.ShapeDtypeStruct((Bs, SAMPLE_ROWS, NSA_HEADS * HEAD_DIM), F32),
        compiler_params=_cparams(("parallel",)),
        name="finish_sample",
    )(slopes, z3s, state_win2, z3s, z3s, selmask, m, l, acc, oc, zs3s, ga, z3s)


def _mix_and_ffn(mixed, x3, ple3, wts, tm):
    B, T, _ = x3.shape
    n = B * T
    x1 = _merge_out(mixed.reshape(n, D_MODEL), x3.reshape(n, D_MODEL), wts["w_o"], wts["g_post1"], min(tm, 256))
    x3o = _ffn_ple(x1, ple3.reshape(n, PLE_DIM), wts["g_pre2"], wts["w_ffn_in"], wts["w_ffn_out"], wts["g_post2"],
                   wts["w_ple_gate"], wts["w_ple"], tm)
    return x3o.reshape(B, T, D_MODEL)


def kernel(x_prompt, x_sample, cache_cmp_kv, cache_sel_kv, page_table, state_win_kv, state_gdn, state_conv, p_prompt, p_sample, g_pre1, w_in, conv_w, A_log, dt_bias, gdn_norm_w, w_cmp, w_o, g_post1, g_pre2, w_ffn_in, w_ffn_out, g_post2, w_ple, w_ple_gate):
    B, T, _ = x_prompt.shape
    Bs, Ts, _ = x_sample.shape
    n_pages = page_table.shape[1]
    past_len = n_pages * PAGE_SIZE
    win_buf = state_win_kv.shape[2]
    kvh = (2, NSA_KV_HEADS, HEAD_DIM)
    qkv_w = 3 * GDN_WIDTH

    wi = w_in[0]
    w_main = (_wprep(wi),)
    w_small = jnp.concatenate([wi[:, 8192:8224], wi[:, 13344:13392],
                               jnp.zeros((D_MODEL, SMALL_WIDTH - 80), F32)], axis=1).astype(BF16)
    wts = dict(w_o=w_o[0].astype(BF16), g_post1=g_post1[0], g_pre2=g_pre2[0],
               w_ffn_in=w_ffn_in[0].astype(BF16), w_ffn_out=w_ffn_out[0].astype(BF16), g_post2=g_post2[0],
               w_ple=w_ple[0].astype(BF16), w_ple_gate=w_ple_gate[0].astype(BF16))
    hp = jnp.zeros((8, SMALL_WIDTH), F32).at[0, 0:GDN_HEADS].set(A_log[0]).at[1, 0:GDN_HEADS].set(dt_bias[0])
    w_cmp2 = w_cmp[0].reshape(CMP_BLOCK, 2 * NSA_KV_WIDTH)
    heads = jnp.arange(1, NSA_HEADS + 1, dtype=F32)
    slopes = jnp.exp2(-8.0 * heads / NSA_HEADS)

    z2, zs2, zkv2 = _inproj(x_prompt.reshape(B * T, D_MODEL), g_pre1[0], w_main, w_small, 1024)
    z3, zs3 = z2.reshape(B, T, MAIN_WIDTH), zs2.reshape(B, T, SMALL_WIDTH)
    o_a, s_new_p = _gdn(z3, zs3, jnp.zeros((B, 8, qkv_w), F32), conv_w[0], hp, gdn_norm_w[0],
                        jnp.zeros((B, GDN_HEADS, HEAD_DIM, HEAD_DIM), F32), 256, 256)
    kvc = _compress_prompt(z3, w_cmp2)
    ns = T // SEL_BLOCK
    oc, selmask = _cmp_attn(z3, zs3, kvc, slopes, 512, T, ns, ns, 0, "cmp_attn_prompt")
    mixed = _sel_win_prompt(z3, zkv2.reshape(B, T, KV_COPY_WIDTH), zs3, selmask, oc, o_a, slopes)
    y_prompt = _mix_and_ffn(mixed, x_prompt, p_prompt[0], wts, 512)

    tz = GDN_CHUNK
    xs = jnp.pad(x_sample, ((0, 0), (0, tz - Ts), (0, 0)))
    zs2_, zss2, _ = _inproj(xs.reshape(Bs * tz, D_MODEL), g_pre1[0], w_main, w_small, Bs * tz)
    z3s, zs3s = zs2_.reshape(Bs, tz, MAIN_WIDTH), zss2.reshape(Bs, tz, SMALL_WIDTH)
    conv_prev = jnp.pad(state_conv[0], ((0, 0), (8 - (CONV_WIDTH - 1), 0), (0, 0)))
    o_a_s, s_new_s = _gdn(z3s, zs3s, conv_prev, conv_w[0], hp, gdn_norm_w[0], state_gdn[0], tz, Ts, nh=GDN_HEADS)
    n_pool = cache_cmp_kv.shape[1]
    kvc_s = _compress_sample(cache_cmp_kv[0].reshape(n_pool, PAGE_SIZE * KV_ROW, HEAD_DIM), page_table,
                             w_cmp[0].reshape(CMP_BLOCK, KV_ROW, HEAD_DIM))
    kvc_s = kvc_s.reshape(Bs, kvc_s.shape[1], 2 * NSA_KV_WIDTH)
    ns_real = -(-(past_len + Ts) // SEL_BLOCK)
    ns_pad = -(-ns_real // 128) * 128
    oc_s, selmask_s = _cmp_attn(z3s, zs3s, kvc_s, slopes, SAMPLE_ROWS, SAMPLE_ROWS, ns_pad, ns_real, past_len,
                                "cmp_attn_sample")
    m_s, l_s, acc_s = _sel_sample(z3s, selmask_s, cache_sel_kv[0].reshape(n_pool, PAGE_SIZE * KV_ROW, HEAD_DIM),
                                  page_table, slopes, past_len)
    state_win2 = state_win_kv[0].reshape(Bs, win_buf, 2 * NSA_KV_WIDTH)
    mixed_s = _finish_sample(z3s, zs3s, state_win2, selmask_s, m_s, l_s, acc_s, oc_s, o_a_s, slopes, past_len, Ts)
    mixed_s = jnp.pad(mixed_s, ((0, 0), (0, tz - SAMPLE_ROWS), (0, 0)))
    ps = jnp.pad(p_sample[0], ((0, 0), (0, tz - Ts), (0, 0)))
    y_sample = _mix_and_ffn(mixed_s, xs, ps, wts, Bs * tz)[:, :Ts]

    def kv_rows(z, col, lo, hi):
        return z[:, lo:hi, col:col + 2 * NSA_KV_WIDTH].reshape((z.shape[0], hi - lo) + kvh)

    new_win_s = jnp.concatenate([state_win_kv[0][:, Ts:], kv_rows(z3s, COL_WIN, 0, Ts)], axis=1)
    return (y_prompt, y_sample,
            kv_rows(z3, COL_CMP, 0, T)[None], kv_rows(z3, COL_SEL, 0, T)[None],
            kv_rows(z3, COL_WIN, T - win_buf, T)[None], s_new_p[None],
            z3[:, T - (CONV_WIDTH - 1):, 0:qkv_w][None],
            kv_rows(z3s, COL_CMP, 0, Ts)[None], kv_rows(z3s, COL_SEL, 0, Ts)[None],
            new_win_s[None], s_new_s[None],
            z3s[:, Ts - (CONV_WIDTH - 1):Ts, 0:qkv_w][None])
```

```python
import functools

import jax
import jax.numpy as jnp
from jax import lax
from jax.experimental import pallas as pl
from jax.experimental.pallas import tpu as pltpu

F32 = jnp.float32
BF16 = jnp.bfloat16
HI = lax.Precision.HIGHEST

D_MODEL = 2048
HEAD_DIM = 128
GDN_HEADS = 16
GDN_WIDTH = 2048
CONV_WIDTH = 4
GDN_CHUNK = 64
NSA_HEADS = 16
NSA_KV_HEADS = 4
NSA_GROUP = 4
NSA_KV_WIDTH = 512
CMP_BLOCK = 32
SEL_BLOCK = 64
SEL_TOPK = 16
WINDOW = 512
PLE_DIM = 256
FFN_HIDDEN = 5632
PAGE_SIZE = 128
RMS_EPS = 1e-6
NEG_INF = -1e30
FORCE_BONUS = float(NSA_GROUP + 1)
QK_SCALE = HEAD_DIM ** -0.5
LOG2E = 1.4426950408889634
MASK_DIST = 1e32

COL_QKV = 0
COL_ZA = 6144
COL_GMA = 8192
COL_GMN = 10240
COL_QN = 12288
COL_CMP = 14336
COL_SEL = 15360
COL_WIN = 16384
MAIN_WIDTH = 17408
KV_COPY_WIDTH = 2048
SMALL_WIDTH = 128
SMALL_A, SMALL_B, SMALL_GN = 0, 16, 32

VMEM_LIMIT = 56 * 1024 * 1024


def _cparams(sem):
    return pltpu.CompilerParams(dimension_semantics=sem, vmem_limit_bytes=VMEM_LIMIT)


def _bdot(a, b):
    return jnp.dot(a.astype(BF16), b.astype(BF16), preferred_element_type=F32)


def _bdot_nt(a, b):
    return lax.dot_general(a.astype(BF16), b.astype(BF16), (((1,), (1,)), ((), ())),
                           preferred_element_type=F32)


def _hdot(a, b):
    return jnp.dot(a, b, precision=HI, preferred_element_type=F32)


def _hdot_nt(a, b):
    return lax.dot_general(a, b, (((1,), (1,)), ((), ())), precision=HI, preferred_element_type=F32)


def _sigmoid(x):
    return 1.0 / (1.0 + jnp.exp(-x))


def _silu(x):
    return x * _sigmoid(x)


def _rms_rows(x):
    return lax.rsqrt(jnp.mean(x * x, axis=-1, keepdims=True) + RMS_EPS)


def _inproj_kernel(bounds, kv_lo, x_ref, g_ref, *refs):
    w_refs = refs[:len(bounds)]
    ws_ref, z_ref, zs_ref, zkv_ref, h_scr = refs[len(bounds):]
    j = pl.program_id(1)

    @pl.when(j == 0)
    def _():
        x = x_ref[...]
        h = ((x * _rms_rows(x)) * g_ref[...]).astype(BF16)
        h_scr[...] = h
        zs_ref[...] = jnp.dot(h, ws_ref[...], preferred_element_type=F32)

    for (lo, hi), w_ref in zip(bounds, w_refs):
        @pl.when((j >= lo) & (j < hi))
        def _(w_ref=w_ref, hi=hi):
            z = jnp.dot(h_scr[...], w_ref[...], preferred_element_type=F32)
            z_ref[...] = z
            if hi > kv_lo:
                @pl.when(j >= kv_lo)
                def _():
                    zkv_ref[...] = z.astype(BF16)


def _inproj(x2, g_pre1, w_slabs, w_small, tm, tn=1024):
    n = x2.shape[0]
    bounds, lo = [], 0
    for w in w_slabs:
        bounds.append((lo, lo + w.shape[1] // tn))
        lo = bounds[-1][1]
    assert lo * tn == MAIN_WIDTH

    def slab_spec(lo, hi):
        return pl.BlockSpec((D_MODEL, tn), lambda i, j: (0, jnp.clip(j - lo, 0, hi - lo - 1)))

    kv_lo, kv_n = COL_SEL // tn, KV_COPY_WIDTH // tn
    assert COL_SEL + KV_COPY_WIDTH == MAIN_WIDTH
    return pl.pallas_call(
        functools.partial(_inproj_kernel, tuple(bounds), kv_lo),
        grid=(n // tm, MAIN_WIDTH // tn),
        in_specs=[
            pl.BlockSpec((tm, D_MODEL), lambda i, j: (i, 0)),
            pl.BlockSpec((1, D_MODEL), lambda i, j: (0, 0)),
        ] + [slab_spec(lo, hi) for lo, hi in bounds] + [
            pl.BlockSpec((D_MODEL, SMALL_WIDTH), lambda i, j: (0, 0)),
        ],
        out_specs=[
            pl.BlockSpec((tm, tn), lambda i, j: (i, j)),
            pl.BlockSpec((tm, SMALL_WIDTH), lambda i, j: (i, 0)),
            pl.BlockSpec((tm, tn), lambda i, j: (i, jnp.clip(j - kv_lo, 0, kv_n - 1))),
        ],
        out_shape=[jax.ShapeDtypeStruct((n, MAIN_WIDTH), F32),
                   jax.ShapeDtypeStruct((n, SMALL_WIDTH), F32),
                   jax.ShapeDtypeStruct((n, KV_COPY_WIDTH), BF16)],
        scratch_shapes=[pltpu.VMEM((tm, D_MODEL), BF16)],
        compiler_params=_cparams(("parallel", "arbitrary")),
        name="inproj",
    )(x2, g_pre1.reshape(1, D_MODEL), *w_slabs, w_small)


def _level_mask(ii, jj, s):
    return ((ii // s) % 2 == 1) & (jj // s == ii // s - 1)


def _gdn_kernel(tb, nh, t_valid,
                q_ref, k_ref, v_ref, qh_ref, kh_ref, vh_ref, cpq_ref, cpk_ref, cpv_ref,
                cwq_ref, cwk_ref, cwv_ref, z_ref, gm_ref, zs_ref, hp_ref, nw_ref, s0_ref,
                o_ref, sn_ref, ext_scr, gt_scr, s_scr):
    C = GDN_CHUNK
    nch = tb // C
    hg = pl.program_id(1)
    t = pl.program_id(2)
    nt = pl.num_programs(2)

    @pl.when(t == 0)
    def _():
        s_scr[...] = s0_ref[...]

    def conv_silu(u_ref, halo_ref, cp_ref, cw_ref):
        prev = jnp.where(t == 0, cp_ref[...], halo_ref[...])
        ext_scr[0:8, :] = prev
        ext_scr[8:8 + tb, :] = u_ref[...]
        w = cw_ref[...]
        acc = ext_scr[5:5 + tb, :] * w[0:1, :]
        for j in range(1, CONV_WIDTH):
            acc = acc + ext_scr[5 + j:5 + j + tb, :] * w[j:j + 1, :]
        return _silu(acc)

    q_all = conv_silu(q_ref, qh_ref, cpq_ref, cwq_ref)
    k_all = conv_silu(k_ref, kh_ref, cpk_ref, cwk_ref)
    v_all = conv_silu(v_ref, vh_ref, cpv_ref, cwv_ref)

    zs = zs_ref[...]
    lane = lax.broadcasted_iota(jnp.int32, (GDN_CHUNK, SMALL_WIDTH), 1)
    xa = zs + hp_ref[1:2, :]
    softplus = jnp.maximum(xa, 0.0) + jnp.log1p(jnp.exp(-jnp.abs(xa)))
    g_all = -jnp.exp(hp_ref[0:1, :]) * softplus
    beta_all = _sigmoid(zs)
    if t_valid < tb:
        live = lax.broadcasted_iota(jnp.int32, (tb, 1), 0) < t_valid
        g_all = jnp.where(live, g_all, 0.0)
        beta_all = jnp.where(live, beta_all, 0.0)

    ii = lax.broadcasted_iota(jnp.int32, (C, C), 0)
    jj = lax.broadcasted_iota(jnp.int32, (C, C), 1)
    tril = (ii >= jj).astype(F32)
    eye = (ii == jj).astype(F32)
    gcum = []
    for c in range(nch):
        gc = _hdot(tril, g_all[c * C:(c + 1) * C])
        gcum.append(gc)
        gt_scr[c] = jnp.transpose(gc)

    Ls, rhss, qkds, kdts, qgs, egls = [], [], [], [], [], []
    for hl in range(nh):
        hglob = hg * nh + hl
        hs = slice(hl * HEAD_DIM, (hl + 1) * HEAD_DIM)
        q = q_all[:, hs]
        k = k_all[:, hs]
        q = (q * lax.rsqrt(jnp.sum(q * q, axis=-1, keepdims=True) + RMS_EPS)) * QK_SCALE
        k = k * lax.rsqrt(jnp.sum(k * k, axis=-1, keepdims=True) + RMS_EPS)
        for c in range(nch):
            sl = slice(c * C, (c + 1) * C)
            qc, kc, vc = q[sl], k[sl], v_all[sl, hs]
            bcol = jnp.sum(jnp.where(lane == SMALL_B + hglob, beta_all[sl], 0.0), axis=1, keepdims=True)
            gcol = jnp.sum(jnp.where(lane == SMALL_A + hglob, gcum[c], 0.0), axis=1, keepdims=True)
            grow = gt_scr[c, pl.ds(SMALL_A + hglob, 1), :]
            decay = jnp.exp(jnp.where(ii >= jj, gcol - grow, NEG_INF))
            qkk = _bdot_nt(jnp.concatenate([qc, kc], axis=0), kc)
            Ls.append(jnp.where(ii > jj, qkk[C:] * decay, 0.0) * bcol)
            eg = jnp.exp(gcol)
            g_last = grow[:, C - 1:C]
            rhss.append(jnp.concatenate([vc * bcol, kc * (bcol * eg)], axis=1))
            qkds.append(qkk[:C] * decay)
            kdts.append(jnp.transpose(kc * jnp.exp(g_last - gcol)))
            qgs.append(qc * eg)
            egls.append(jnp.exp(g_last))

    n_inst = nh * nch
    Xs = [eye - jnp.where(_level_mask(ii, jj, 1), L, 0.0) for L in Ls]
    s = 2
    while s < C:
        m = _level_mask(ii, jj, s)
        Ys = [_bdot(jnp.where(m, Ls[i], 0.0), Xs[i]) for i in range(n_inst)]
        Zs = [_bdot(Xs[i], Ys[i]) for i in range(n_inst)]
        Xs = [Xs[i] - Zs[i] for i in range(n_inst)]
        s *= 2
    sols = [_bdot(Xs[i], rhss[i]) for i in range(n_inst)]
    res = [rhss[i] - sols[i] - _hdot(Ls[i], sols[i]) for i in range(n_inst)]
    sols = [sols[i] + _bdot(Xs[i], res[i]) for i in range(n_inst)]
    NPs = [_bdot(kdts[i], sols[i]) for i in range(n_inst)]
    QOs = [_bdot(qkds[i], sols[i]) for i in range(n_inst)]

    Ss = [s_scr[hl] for hl in range(nh)]
    outs = [None] * n_inst
    for c in range(nch):
        for hl in range(nh):
            i = hl * nch + c
            S = Ss[hl]
            outs[i] = _bdot(qgs[i] - QOs[i][:, HEAD_DIM:], S) + QOs[i][:, :HEAD_DIM]
            Ss[hl] = S * egls[i] - _bdot(NPs[i][:, HEAD_DIM:], S) + NPs[i][:, :HEAD_DIM]
    nw = nw_ref[...]
    for hl in range(nh):
        hs = slice(hl * HEAD_DIM, (hl + 1) * HEAD_DIM)
        s_scr[hl] = Ss[hl]
        for c in range(nch):
            sl = slice(c * C, (c + 1) * C)
            o = outs[hl * nch + c]
            o_ref[sl, hs] = _sigmoid(gm_ref[sl, hs]) * (((o * _rms_rows(o)) * nw) * _silu(z_ref[sl, hs]))

    @pl.when(t == nt - 1)
    def _():
        sn_ref[...] = s_scr[...]


def _gdn(z3, zs3, conv_prev, conv_w, hp, norm_w, s0, tb, t_valid, nh=4):
    B, T, _ = z3.shape
    nt = T // tb
    hb = tb // 8
    wblk = nh * HEAD_DIM
    cq, ck, cv = COL_QKV // wblk, (COL_QKV + GDN_WIDTH) // wblk, (COL_QKV + 2 * GDN_WIDTH) // wblk
    hpg = GDN_HEADS // nh

    def main(col0):
        return pl.BlockSpec((None, tb, wblk), lambda b, h, t: (b, t, col0 + h))

    def halo(col0):
        return pl.BlockSpec((None, 8, wblk), lambda b, h, t: (b, jnp.maximum(t * hb - 1, 0), col0 + h))

    def cprev(col0):
        return pl.BlockSpec((None, 8, wblk), lambda b, h, t: (b, 0, col0 + h))

    def cw(col0):
        return pl.BlockSpec((CONV_WIDTH, wblk), lambda b, h, t: (0, col0 + h))

    state = pl.BlockSpec((None, nh, HEAD_DIM, HEAD_DIM), lambda b, h, t: (b, h, 0, 0))
    return pl.pallas_call(
        functools.partial(_gdn_kernel, tb, nh, t_valid),
        grid=(B, hpg, nt),
        in_specs=[
            main(cq), main(ck), main(cv), halo(cq), halo(ck), halo(cv),
            cprev(0), cprev(hpg), cprev(2 * hpg), cw(0), cw(hpg), cw(2 * hpg),
            main(COL_ZA // wblk), main(COL_GMA // wblk),
            pl.BlockSpec((None, tb, SMALL_WIDTH), lambda b, h, t: (b, t, 0)),
            pl.BlockSpec((8, SMALL_WIDTH), lambda b, h, t: (0, 0)),
            pl.BlockSpec((1, HEAD_DIM), lambda b, h, t: (0, 0)),
            state,
        ],
        out_specs=[pl.BlockSpec((None, tb, wblk), lambda b, h, t: (b, t, h)), state],
        out_shape=[jax.ShapeDtypeStruct((B, T, GDN_WIDTH), F32),
                   jax.ShapeDtypeStruct((B, GDN_HEADS, HEAD_DIM, HEAD_DIM), F32)],
        scratch_shapes=[pltpu.VMEM((tb + 8, wblk), F32),
                        pltpu.VMEM((tb // GDN_CHUNK, SMALL_WIDTH, GDN_CHUNK), F32),
                        pltpu.VMEM((nh, HEAD_DIM, HEAD_DIM), F32)],
        compiler_params=_cparams(("parallel", "parallel", "arbitrary")),
        name="gdn",
    )(z3, z3, z3, z3, z3, z3, conv_prev, conv_prev, conv_prev, conv_w, conv_w, conv_w,
      z3, z3, zs3, hp, norm_w.reshape(1, HEAD_DIM), s0)


def _compress_kernel(x_ref, y_ref, w_ref, o_ref, xr_ref, yr_ref):
    rows = x_ref.shape[0]
    x = x_ref[...]
    o_ref[...] = jnp.sum(x.reshape(rows // CMP_BLOCK, CMP_BLOCK, 2 * NSA_KV_WIDTH) * w_ref[...][None], axis=1)
    for k in range(KV_ROW):
        cols = slice(k * HEAD_DIM, (k + 1) * HEAD_DIM)
        xr_ref[pl.ds(k, rows, stride=KV_ROW), :] = x[:, cols]
        yr_ref[pl.ds(k, rows, stride=KV_ROW), :] = y_ref[:, cols]


def _compress_prompt(z3, w_cmp2, tc=256):
    B, T, _ = z3.shape
    nc = T // CMP_BLOCK
    kvw = 2 * NSA_KV_WIDTH
    rows_spec = pl.BlockSpec((None, tc * KV_ROW, HEAD_DIM), lambda b, i: (b, i, 0))
    return pl.pallas_call(
        _compress_kernel,
        grid=(B, T // tc),
        in_specs=[pl.BlockSpec((None, tc, kvw), lambda b, i: (b, i, COL_CMP // kvw)),
                  pl.BlockSpec((None, tc, kvw), lambda b, i: (b, i, COL_SEL // kvw)),
                  pl.BlockSpec((CMP_BLOCK, kvw), lambda b, i: (0, 0))],
        out_specs=[pl.BlockSpec((None, tc // CMP_BLOCK, kvw), lambda b, i: (b, i, 0)), rows_spec, rows_spec],
        out_shape=[jax.ShapeDtypeStruct((B, nc, kvw), F32),
                   jax.ShapeDtypeStruct((B, T * KV_ROW, HEAD_DIM), F32),
                   jax.ShapeDtypeStruct((B, T * KV_ROW, HEAD_DIM), F32)],
        compiler_params=_cparams(("parallel", "parallel")),
        name="compress_prompt",
    )(z3, z3, w_cmp2)


def _topk_mask(sc, k_sel, axis):
    n = sc.shape[axis]
    idxf = lax.broadcasted_iota(jnp.int32, sc.shape, axis).astype(F32)
    sel = jnp.zeros(sc.shape, F32)
    for _ in range(k_sel):
        m = jnp.max(sc, axis=axis, keepdims=True)
        idx = jnp.min(jnp.where(sc == m, idxf, float(n)), axis=axis, keepdims=True)
        hit = idxf == idx
        sel = jnp.where(hit, 1.0, sel)
        sc = jnp.where(hit, -3.0, sc)
    return sel


def _cmp_attn_kernel(ta, nc, ns, ns_real, t_base, slopes_ref, q_ref, kvc_ref, zs_ref, oc_ref, sel_ref, sc_scr):
    t0 = t_base + pl.program_id(1) * ta
    tpos = t0 + lax.broadcasted_iota(jnp.int32, (ta, nc), 0)
    cend = lax.broadcasted_iota(jnp.int32, (ta, nc), 1) * CMP_BLOCK + (CMP_BLOCK - 1)
    dist = tpos - cend
    valid = dist >= 0
    distf = dist.astype(F32)
    gates = _sigmoid(zs_ref[...])
    blocks_on_rows = ta % 128 == 0
    tok_ax, blk_ax = (1, 0) if blocks_on_rows else (0, 1)
    sshape = (ns, ta) if blocks_on_rows else (ta, ns)
    pshape = (ns, nc) if blocks_on_rows else (nc, ns)
    pool = (lax.broadcasted_iota(jnp.int32, pshape, tok_ax) // (SEL_BLOCK // CMP_BLOCK)
            == lax.broadcasted_iota(jnp.int32, pshape, blk_ax)).astype(F32)
    tq = t0 + lax.broadcasted_iota(jnp.int32, sshape, tok_ax)
    blk = lax.broadcasted_iota(jnp.int32, sshape, blk_ax)
    cur = tq // SEL_BLOCK
    forced = (blk == 0) | (blk == cur) | (blk == cur - 1)
    avail = blk * SEL_BLOCK <= tq
    for g in range(NSA_KV_HEADS):
        kc = kvc_ref[:, g * HEAD_DIM:(g + 1) * HEAD_DIM]
        vc = kvc_ref[:, NSA_KV_WIDTH + g * HEAD_DIM:NSA_KV_WIDTH + (g + 1) * HEAD_DIM]
        imp = jnp.zeros((ta, nc), F32)
        for r in range(NSA_GROUP):
            hd = g * NSA_GROUP + r
            qh = q_ref[:, hd * HEAD_DIM:(hd + 1) * HEAD_DIM] * QK_SCALE
            s = _hdot_nt(qh, kc) - slopes_ref[hd] * distf
            s = jnp.where(valid, s, NEG_INF)
            e = jnp.exp(s - jnp.max(s, axis=1, keepdims=True))
            p = jnp.where(valid, e / jnp.sum(e, axis=1, keepdims=True), 0.0)
            imp = imp + p
            gi = SMALL_GN + hd * 3
            oc_ref[:, hd * HEAD_DIM:(hd + 1) * HEAD_DIM] = gates[:, gi:gi + 1] * _bdot(p, vc)
        imps = _hdot_nt(pool, imp) if blocks_on_rows else _hdot(imp, pool)
        score = jnp.where(forced, imps + FORCE_BONUS, jnp.where(avail, imps, -1.0))
        if ns_real < ns:
            score = jnp.where(blk < ns_real, score, -2.0)
        if blocks_on_rows:
            sc_scr[:, g * ta:(g + 1) * ta] = score
        else:
            sc_scr[g * ta:(g + 1) * ta, :] = score
    sel = _topk_mask(sc_scr[...], min(SEL_TOPK, ns_real), blk_ax)
    for g in range(NSA_KV_HEADS):
        if blocks_on_rows:
            sel_ref[g] = jnp.transpose(sel[:, g * ta:(g + 1) * ta])
        else:
            sel_ref[g] = sel[g * ta:(g + 1) * ta, :]


def _cmp_attn(z3, zs3, kvc, slopes, ta, n_tok, ns, ns_real, t_base, name):
    B = z3.shape[0]
    nc = kvc.shape[1]
    return pl.pallas_call(
        functools.partial(_cmp_attn_kernel, ta, nc, ns, ns_real, t_base),
        grid_spec=pltpu.PrefetchScalarGridSpec(
            num_scalar_prefetch=1,
            grid=(B, n_tok // ta),
            in_specs=[
                pl.BlockSpec((None, ta, NSA_HEADS * HEAD_DIM), lambda b, i, sl: (b, i, COL_QN // 2048)),
                pl.BlockSpec((None, nc, 2 * NSA_KV_WIDTH), lambda b, i, sl: (b, 0, 0)),
                pl.BlockSpec((None, ta, SMALL_WIDTH), lambda b, i, sl: (b, i, 0)),
            ],
            out_specs=[
                pl.BlockSpec((None, ta, NSA_HEADS * HEAD_DIM), lambda b, i, sl: (b, i, 0)),
                pl.BlockSpec((None, NSA_KV_HEADS, ta, ns), lambda b, i, sl: (b, 0, i, 0)),
            ],
            scratch_shapes=[pltpu.VMEM((ns, NSA_KV_HEADS * ta) if ta % 128 == 0 else (NSA_KV_HEADS * ta, ns), F32)],
        ),
        out_shape=[jax.ShapeDtypeStruct((B, n_tok, NSA_HEADS * HEAD_DIM), F32),
                   jax.ShapeDtypeStruct((B, NSA_KV_HEADS, n_tok, ns), F32)],
        compiler_params=_cparams(("parallel", "parallel")),
        name=name,
    )(slopes, z3, kvc, zs3)


def _sel_win_kernel(T, QB, ns, tkv, wspan, slopes_ref, q_ref, ks_ref, vs_ref, kw_ref, vw_ref, sel_ref,
                    zs_ref, oc_ref, ga_ref, gmn_ref, o_ref):
    R = NSA_GROUP
    g = pl.program_id(1)
    qb = pl.program_id(2)
    t0 = qb * QB
    q = jnp.concatenate([(q_ref[:, r * HEAD_DIM:(r + 1) * HEAD_DIM] * (QK_SCALE * LOG2E)).astype(BF16)
                         for r in range(R)], axis=0)
    slope2 = [slopes_ref[g * R + r] * LOG2E for r in range(R)]
    selb = sel_ref[...].astype(BF16)

    def head_rows(x, r):
        return x[r * QB:(r + 1) * QB]

    def spread(cols):
        return jnp.concatenate([jnp.broadcast_to(c, (QB, 2 * HEAD_DIM)) for c in cols], axis=0)

    ti = lax.broadcasted_iota(jnp.int32, (QB, tkv), 0)
    kj = lax.broadcasted_iota(jnp.int32, (QB, tkv), 1)
    d0 = (ti - kj).astype(F32)
    eb = lax.broadcasted_iota(jnp.int32, (ns, tkv), 0)
    ek = lax.broadcasted_iota(jnp.int32, (ns, tkv), 1) // SEL_BLOCK

    def sel_scores(j):
        k0 = j * tkv
        expand = (eb == ek + j * (tkv // SEL_BLOCK)).astype(BF16)
        keymask = jnp.dot(selb, expand, preferred_element_type=F32)
        distf = d0 + (t0 - k0).astype(F32)
        base = jnp.where((distf >= 0.0) & (keymask > 0.5), distf, MASK_DIST)
        return _bdot_nt(q, ks_ref[k0:k0 + tkv, :]), base

    def with_ones(v):
        return jnp.concatenate([v.astype(BF16), jnp.ones(v.shape, BF16)], axis=1)

    def sel_tile(j, s, base, carry):
        ms, acc = carry
        k0 = j * tkv
        ps, ms2, alphas = [], [], []
        for r in range(R):
            sr = head_rows(s, r) - slope2[r] * base
            m_new = jnp.maximum(ms[r], jnp.max(sr, axis=1, keepdims=True))
            ps.append(jnp.exp2((sr - m_new).astype(BF16)))
            alphas.append(jnp.exp2(ms[r] - m_new))
            ms2.append(m_new)
        pv = jnp.dot(jnp.concatenate(ps, axis=0), with_ones(vs_ref[k0:k0 + tkv, :]),
                     preferred_element_type=F32)
        return ms2, spread(alphas) * acc + pv

    def window():
        kstart = pl.multiple_of(jnp.clip(t0 - WINDOW, 0, T - wspan), QB)
        kw = kw_ref[pl.ds(kstart, wspan), :]
        vw = vw_ref[pl.ds(kstart, wspan), :]
        tw = lax.broadcasted_iota(jnp.int32, (QB, wspan), 0)
        kwj = lax.broadcasted_iota(jnp.int32, (QB, wspan), 1)
        dist = (tw - kwj) + (t0 - kstart)
        base_w = jnp.where((dist >= 0) & (dist < WINDOW), dist.astype(F32), MASK_DIST)
        s = _bdot_nt(q, kw)
        pw = []
        for r in range(R):
            sr = head_rows(s, r) - slope2[r] * base_w
            pw.append(jnp.exp2((sr - jnp.max(sr, axis=1, keepdims=True)).astype(BF16)))
        return jnp.dot(jnp.concatenate(pw, axis=0), with_ones(vw), preferred_element_type=F32)

    def run(n_tiles):
        carry = ([jnp.full((QB, 1), NEG_INF, F32) for _ in range(R)],
                 jnp.zeros((R * QB, 2 * HEAD_DIM), F32))
        nxt = sel_scores(0)
        acc_w = window()
        for j in range(n_tiles):
            cur, nxt = nxt, (sel_scores(j + 1) if j + 1 < n_tiles else None)
            carry = sel_tile(j, cur[0], cur[1], carry)
        acc_s = carry[1]
        o_s = acc_s[:, :HEAD_DIM] / acc_s[:, HEAD_DIM:]
        o_w = acc_w[:, :HEAD_DIM] / acc_w[:, HEAD_DIM:]
        gates = _sigmoid(zs_ref[...])
        lane = lax.broadcasted_iota(jnp.int32, (QB, SMALL_WIDTH), 1)
        for r in range(R):
            gi = SMALL_GN + (g * R + r) * 3
            g_s = jnp.sum(jnp.where(lane == gi + 1, gates, 0.0), axis=1, keepdims=True)
            g_w = jnp.sum(jnp.where(lane == gi + 2, gates, 0.0), axis=1, keepdims=True)
            cs = slice(r * HEAD_DIM, (r + 1) * HEAD_DIM)
            o_n = oc_ref[:, cs] + g_s * head_rows(o_s, r) + g_w * head_rows(o_w, r)
            o_ref[:, cs] = (ga_ref[:, cs] + _sigmoid(gmn_ref[:, cs]) * o_n).astype(BF16)

    need = (t0 + QB + tkv - 1) // tkv
    for n_tiles in range(1, T // tkv + 1):
        pl.when(need == n_tiles)(functools.partial(run, n_tiles))


def _sel_win_prompt(z3, zkv3, zs3, selmask, oc, ga, slopes):
    B, T, _ = z3.shape
    ns = T // SEL_BLOCK
    tkv = min(1024, T)
    qb = 128
    wspan = min(WINDOW + 2 * qb, T)
    gw = NSA_GROUP * HEAD_DIM

    def kv(col0):
        return pl.BlockSpec((None, T, HEAD_DIM), lambda b, g, i, sl: (b, 0, (col0 - COL_SEL) // 128 + g))

    return pl.pallas_call(
        functools.partial(_sel_win_kernel, T, qb, ns, tkv, wspan),
        grid_spec=pltpu.PrefetchScalarGridSpec(
            num_scalar_prefetch=1,
            grid=(B, NSA_KV_HEADS, T // qb),
            in_specs=[
                pl.BlockSpec((None, qb, gw), lambda b, g, i, sl: (b, i, COL_QN // gw + g)),
                kv(COL_SEL), kv(COL_SEL + NSA_KV_WIDTH), kv(COL_WIN), kv(COL_WIN + NSA_KV_WIDTH),
                pl.BlockSpec((None, None, qb, ns), lambda b, g, i, sl: (b, g, i, 0)),
                pl.BlockSpec((None, qb, SMALL_WIDTH), lambda b, g, i, sl: (b, i, 0)),
                pl.BlockSpec((None, qb, gw), lambda b, g, i, sl: (b, i, g)),
                pl.BlockSpec((None, qb, gw), lambda b, g, i, sl: (b, i, g)),
                pl.BlockSpec((None, qb, gw), lambda b, g, i, sl: (b, i, COL_GMN // gw + g)),
            ],
            out_specs=pl.BlockSpec((None, qb, gw), lambda b, g, i, sl: (b, i, g)),
        ),
        out_shape=jax.ShapeDtypeStruct((B, T, D_MODEL), BF16),
        compiler_params=_cparams(("parallel", "parallel", "arbitrary")),
        name="sel_win_prompt",
    )(slopes, z3, zkv3, zkv3, zkv3, zkv3, selmask, zs3, oc, ga, z3)


def _merge_kernel(mixed_ref, x_ref, w_ref, g_ref, o_ref):
    y = jnp.dot(mixed_ref[...].astype(BF16), w_ref[...], preferred_element_type=F32)
    o_ref[...] = x_ref[...] + (y * _rms_rows(y)) * g_ref[...]


def _merge_out(mixed2, x2, w_o, g_post1, tm):
    n = x2.shape[0]
    row = pl.BlockSpec((tm, D_MODEL), lambda i: (i, 0))
    return pl.pallas_call(
        _merge_kernel,
        grid=(n // tm,),
        in_specs=[row, row,
                  pl.BlockSpec((D_MODEL, D_MODEL), lambda i: (0, 0)),
                  pl.BlockSpec((1, D_MODEL), lambda i: (0, 0))],
        out_specs=row,
        out_shape=jax.ShapeDtypeStruct((n, D_MODEL), F32),
        compiler_params=_cparams(("parallel",)),
        name="merge_out",
    )(mixed2, x2, w_o, g_post1.reshape(1, D_MODEL))


def _ffn_kernel(x_ref, g2_ref, wg_ref, wu_ref, wo_ref, gp_ref, p_ref, wpg_ref, wp_ref, o_ref, h_scr, acc_scr):
    j = pl.program_id(1)

    @pl.when(j == 0)
    def _():
        x = x_ref[...]
        h_scr[...] = ((x * _rms_rows(x)) * g2_ref[...]).astype(BF16)
        acc_scr[...] = jnp.zeros_like(acc_scr)

    h = h_scr[...]
    gt = jnp.dot(h, wg_ref[...], preferred_element_type=F32)
    up = jnp.dot(h, wu_ref[...], preferred_element_type=F32)
    acc_scr[...] += jnp.dot((_silu(gt) * up).astype(BF16), wo_ref[...], preferred_element_type=F32)

    @pl.when(j == pl.num_programs(1) - 1)
    def _():
        y = acc_scr[...]
        x = x_ref[...] + (y * _rms_rows(y)) * gp_ref[...]
        gate = _sigmoid(jnp.dot(x.astype(BF16), wpg_ref[...], preferred_element_type=F32))
        o_ref[...] = x + gate * jnp.dot(p_ref[...].astype(BF16), wp_ref[...], preferred_element_type=F32)


def _ffn_ple(x2, ple2, g_pre2, w_ffn_in, w_ffn_out, g_post2, w_ple_gate, w_ple, tm, th=512):
    n = x2.shape[0]
    nh = FFN_HIDDEN // th
    once = pl.Buffered(1)
    return pl.pallas_call(
        _ffn_kernel,
        grid=(n // tm, nh),
        in_specs=[
            pl.BlockSpec((tm, D_MODEL), lambda i, j: (i, 0)),
            pl.BlockSpec((1, D_MODEL), lambda i, j: (0, 0)),
            pl.BlockSpec((D_MODEL, th), lambda i, j: (0, j)),
            pl.BlockSpec((D_MODEL, th), lambda i, j: (0, nh + j)),
            pl.BlockSpec((th, D_MODEL), lambda i, j: (j, 0)),
            pl.BlockSpec((1, D_MODEL), lambda i, j: (0, 0)),
            pl.BlockSpec((tm, PLE_DIM), lambda i, j: (i, 0)),
            pl.BlockSpec((D_MODEL, D_MODEL), lambda i, j: (0, 0), pipeline_mode=once),
            pl.BlockSpec((PLE_DIM, D_MODEL), lambda i, j: (0, 0), pipeline_mode=once),
        ],
        out_specs=pl.BlockSpec((tm, D_MODEL), lambda i, j: (i, 0)),
        out_shape=jax.ShapeDtypeStruct((n, D_MODEL), F32),
        scratch_shapes=[pltpu.VMEM((tm, D_MODEL), BF16), pltpu.VMEM((tm, D_MODEL), F32)],
        compiler_params=_cparams(("parallel", "arbitrary")),
        name="ffn_ple",
    )(x2, g_pre2.reshape(1, D_MODEL), w_ffn_in, w_ffn_in, w_ffn_out, g_post2.reshape(1, D_MODEL),
      ple2, w_ple_gate, w_ple)


SAMPLE_ROWS = 8
KV_ROW = 2 * NSA_KV_HEADS


def _compress_pages_kernel(npg, pt_ref, *refs):
    w = refs[npg][...]
    o_ref = refs[npg + 1]
    per = PAGE_SIZE // CMP_BLOCK
    for p in range(npg):
        x = refs[p][...].reshape(per, CMP_BLOCK, KV_ROW, HEAD_DIM)
        o_ref[p * per:(p + 1) * per] = jnp.sum(x * w[None], axis=1)


def _compress_sample(cache_rows, page_table, w_cmp3, npg=16):
    Bs, n_pages = page_table.shape
    per = PAGE_SIZE // CMP_BLOCK

    def page(p):
        return pl.BlockSpec((None, PAGE_SIZE * KV_ROW, HEAD_DIM), lambda b, i, pt: (pt[b, i * npg + p], 0, 0))

    return pl.pallas_call(
        functools.partial(_compress_pages_kernel, npg),
        grid_spec=pltpu.PrefetchScalarGridSpec(
            num_scalar_prefetch=1,
            grid=(Bs, n_pages // npg),
            in_specs=[page(p) for p in range(npg)]
            + [pl.BlockSpec((CMP_BLOCK, KV_ROW, HEAD_DIM), lambda b, i, pt: (0, 0, 0))],
            out_specs=pl.BlockSpec((None, npg * per, KV_ROW, HEAD_DIM), lambda b, i, pt: (b, i, 0, 0)),
        ),
        out_shape=jax.ShapeDtypeStruct((Bs, n_pages * per, KV_ROW, HEAD_DIM), F32),
        compiler_params=_cparams(("parallel", "parallel")),
        name="compress_sample",
    )(page_table, *([cache_rows] * npg), w_cmp3)


def _sel_pages_kernel(npg, past_len, phys_ref, lst_ref, cnt_ref, slopes_ref, q_ref, sel_ref, *refs):
    pages = refs[:npg]
    m_ref, l_ref, acc_ref, q_scr, selrows_scr, slope_scr = refs[npg:]
    b = pl.program_id(0)
    i = pl.program_id(1)
    nrow = NSA_HEADS * SAMPLE_ROWS
    rg = NSA_GROUP * SAMPLE_ROWS
    ns_pad = sel_ref.shape[-1]

    @pl.when(i == 0)
    def _():
        for hd in range(NSA_HEADS):
            rs = slice(hd * SAMPLE_ROWS, (hd + 1) * SAMPLE_ROWS)
            q_scr[rs, :] = q_ref[:, hd * HEAD_DIM:(hd + 1) * HEAD_DIM] * (QK_SCALE * LOG2E)
            selrows_scr[rs, :] = sel_ref[hd // NSA_GROUP]
            slope_scr[rs, :] = jnp.full((SAMPLE_ROWS, HEAD_DIM), slopes_ref[hd] * LOG2E, F32)
        m_ref[...] = jnp.full(m_ref.shape, NEG_INF, F32)
        l_ref[...] = jnp.zeros(l_ref.shape, F32)
        acc_ref[...] = jnp.zeros(acc_ref.shape, F32)

    @pl.when(i * npg < cnt_ref[b])
    def _():
        qb = q_scr[...].astype(BF16)
        selb = selrows_scr[...].astype(BF16)
        slope = slope_scr[...]
        row = lax.broadcasted_iota(jnp.int32, (nrow, PAGE_SIZE), 0)
        pos = lax.broadcasted_iota(jnp.int32, (nrow, PAGE_SIZE), 1)
        d0 = (past_len + row % SAMPLE_ROWS - pos).astype(F32)
        ob = lax.broadcasted_iota(jnp.int32, (ns_pad, PAGE_SIZE), 0)
        ol = lax.broadcasted_iota(jnp.int32, (ns_pad, PAGE_SIZE), 1) // SEL_BLOCK
        scores = []
        for p in range(npg):
            k = i * npg + p
            pg = lst_ref[b, k]
            flags = jnp.dot(selb, (ob == ol + pg * (PAGE_SIZE // SEL_BLOCK)).astype(BF16),
                            preferred_element_type=F32)
            distf = d0 - (pg * PAGE_SIZE).astype(F32)
            base = jnp.where((flags > 0.5) & (distf >= 0.0), distf, MASK_DIST)
            base = jnp.where(k < cnt_ref[b], base, MASK_DIST)
            scores.append(jnp.concatenate(
                [_bdot_nt(qb[g * rg:(g + 1) * rg], pages[p][pl.ds(g, PAGE_SIZE, stride=KV_ROW), :])
                 for g in range(NSA_KV_HEADS)], axis=0) - slope * base)
        s = jnp.concatenate(scores, axis=1)
        m_old = m_ref[...]
        m_new = jnp.maximum(m_old, jnp.max(s, axis=1, keepdims=True))
        pr = jnp.exp2(s - m_new[:, 0:1])
        alpha = jnp.exp2(m_old - m_new)
        l_ref[...] = alpha * l_ref[...] + jnp.sum(pr, axis=1, keepdims=True)
        prb = pr.astype(BF16)
        pv = jnp.zeros((nrow, HEAD_DIM), F32)
        for p in range(npg):
            ps = prb[:, p * PAGE_SIZE:(p + 1) * PAGE_SIZE]
            pv = pv + jnp.concatenate(
                [_bdot(ps[g * rg:(g + 1) * rg], pages[p][pl.ds(NSA_KV_HEADS + g, PAGE_SIZE, stride=KV_ROW), :])
                 for g in range(NSA_KV_HEADS)], axis=0)
        acc_ref[...] = alpha * acc_ref[...] + pv
        m_ref[...] = m_new


def _sel_sample(z3s, selmask, cache_rows, page_table, slopes, past_len, npg=8):
    Bs, n_pages = page_table.shape
    nrow = NSA_HEADS * SAMPLE_ROWS
    ns_pad = selmask.shape[-1]
    per_page = PAGE_SIZE // SEL_BLOCK
    picked = selmask[..., :n_pages * per_page].reshape(Bs, -1, n_pages, per_page).max(axis=(1, 3)) > 0.5
    cnt = picked.sum(axis=1).astype(jnp.int32)
    order = jnp.argsort(jnp.logical_not(picked), axis=1, stable=True).astype(jnp.int32)
    last = jnp.take_along_axis(order, jnp.maximum(cnt - 1, 0)[:, None], axis=1)
    lst = jnp.where(lax.broadcasted_iota(jnp.int32, order.shape, 1) < cnt[:, None], order, last)
    phys = jnp.take_along_axis(page_table, lst, axis=1)

    def page(p):
        return pl.BlockSpec((None, PAGE_SIZE * KV_ROW, HEAD_DIM),
                            lambda b, i, ph, ls, ct, sl: (ph[b, i * npg + p], 0, 0))

    part = pl.BlockSpec((None, nrow, HEAD_DIM), lambda b, i, ph, ls, ct, sl: (b, 0, 0))
    return pl.pallas_call(
        functools.partial(_sel_pages_kernel, npg, past_len),
        grid_spec=pltpu.PrefetchScalarGridSpec(
            num_scalar_prefetch=4,
            grid=(Bs, n_pages // npg),
            in_specs=[
                pl.BlockSpec((None, SAMPLE_ROWS, NSA_HEADS * HEAD_DIM),
                             lambda b, i, ph, ls, ct, sl: (b, 0, COL_QN // 2048)),
                pl.BlockSpec((None, NSA_KV_HEADS, SAMPLE_ROWS, ns_pad), lambda b, i, ph, ls, ct, sl: (b, 0, 0, 0)),
            ] + [page(p) for p in range(npg)],
            out_specs=[part, part, part],
            scratch_shapes=[pltpu.VMEM((nrow, HEAD_DIM), F32), pltpu.VMEM((nrow, ns_pad), F32),
                            pltpu.VMEM((nrow, HEAD_DIM), F32)],
        ),
        out_shape=[jax.ShapeDtypeStruct((Bs, nrow, HEAD_DIM), F32)] * 3,
        compiler_params=_cparams(("parallel", "arbitrary")),
        name="sel_sample",
    )(phys, lst, cnt, slopes, z3s, selmask, *([cache_rows] * npg))


def _finish_sample_kernel(past_len, t_real, nnew, slopes_ref, q_ref, kst_ref, snew_ref, wnew_ref, sel_ref,
                          m_ref, l_ref, acc_ref, oc_ref, zs_ref, ga_ref, gmn_ref, o_ref):
    nst = kst_ref.shape[0]
    cur = past_len // SEL_BLOCK
    gates = _sigmoid(zs_ref[...])
    t_new = lax.broadcasted_iota(jnp.int32, (SAMPLE_ROWS, nnew), 0)
    j_new = lax.broadcasted_iota(jnp.int32, (SAMPLE_ROWS, nnew), 1)
    dist_new = t_new - j_new
    ok_new = (dist_new >= 0) & (j_new < t_real)
    t_st = lax.broadcasted_iota(jnp.int32, (SAMPLE_ROWS, nst), 0)
    i_st = lax.broadcasted_iota(jnp.int32, (SAMPLE_ROWS, nst), 1)
    dist_st = t_st + nst - i_st
    ok_st = dist_st < WINDOW
    for hd in range(NSA_HEADS):
        g = hd // NSA_GROUP
        rs = slice(hd * SAMPLE_ROWS, (hd + 1) * SAMPLE_ROWS)
        kc = slice(g * HEAD_DIM, (g + 1) * HEAD_DIM)
        vc = slice(NSA_KV_WIDTH + g * HEAD_DIM, NSA_KV_WIDTH + (g + 1) * HEAD_DIM)
        sl = slopes_ref[hd] * LOG2E
        qh = (q_ref[:, hd * HEAD_DIM:(hd + 1) * HEAD_DIM] * (QK_SCALE * LOG2E)).astype(BF16)
        valid = ok_new & (sel_ref[g][:, cur:cur + 1] > 0.5)
        s = jnp.where(valid, _bdot_nt(qh, snew_ref[:, kc]) - sl * dist_new.astype(F32), NEG_INF)
        m_old = m_ref[rs, 0:1]
        m_new = jnp.maximum(m_old, jnp.max(s, axis=1, keepdims=True))
        pr = jnp.where(valid, jnp.exp2(s - m_new), 0.0)
        alpha = jnp.exp2(m_old - m_new)
        l = alpha * l_ref[rs, 0:1] + jnp.sum(pr, axis=1, keepdims=True)
        o_s = (alpha * acc_ref[rs, :] + _bdot(pr, snew_ref[:, vc])) / l
        s1 = jnp.where(ok_st, _bdot_nt(qh, kst_ref[:, kc]) - sl * dist_st.astype(F32), NEG_INF)
        s2 = jnp.where(ok_new, _bdot_nt(qh, wnew_ref[:, kc]) - sl * dist_new.astype(F32), NEG_INF)
        mw = jnp.maximum(jnp.max(s1, axis=1, keepdims=True), jnp.max(s2, axis=1, keepdims=True))
        e1 = jnp.where(ok_st, jnp.exp2(s1 - mw), 0.0)
        e2 = jnp.where(ok_new, jnp.exp2(s2 - mw), 0.0)
        den = jnp.sum(e1, axis=1, keepdims=True) + jnp.sum(e2, axis=1, keepdims=True)
        o_w = _bdot(e1 / den, kst_ref[:, vc]) + _bdot(e2 / den, wnew_ref[:, vc])
        gi = SMALL_GN + hd * 3
        cs = slice(hd * HEAD_DIM, (hd + 1) * HEAD_DIM)
        o_n = oc_ref[:, cs] + gates[:, gi + 1:gi + 2] * o_s + gates[:, gi + 2:gi + 3] * o_w
        o_ref[:, cs] = ga_ref[:, cs] + _sigmoid(gmn_ref[:, cs]) * o_n


def _finish_sample(z3s, zs3s, state_win2, selmask, m, l, acc, oc, ga, slopes, past_len, t_real):
    Bs, tz, _ = z3s.shape
    nst = state_win2.shape[1]
    nrow = NSA_HEADS * SAMPLE_ROWS
    ns_pad = selmask.shape[-1]
    kvw = 2 * NSA_KV_WIDTH
    part = pl.BlockSpec((None, nrow, HEAD_DIM), lambda b, sl: (b, 0, 0))
    wide = pl.BlockSpec((None, SAMPLE_ROWS, NSA_HEADS * HEAD_DIM), lambda b, sl: (b, 0, 0))
    return pl.pallas_call(
        functools.partial(_finish_sample_kernel, past_len, t_real, tz),
        grid_spec=pltpu.PrefetchScalarGridSpec(
            num_scalar_prefetch=1,
            grid=(Bs,),
            in_specs=[
                pl.BlockSpec((None, SAMPLE_ROWS, NSA_HEADS * HEAD_DIM), lambda b, sl: (b, 0, COL_QN // 2048)),
                pl.BlockSpec((None, nst, kvw), lambda b, sl: (b, 0, 0)),
                pl.BlockSpec((None, tz, kvw), lambda b, sl: (b, 0, COL_SEL // kvw)),
                pl.BlockSpec((None, tz, kvw), lambda b, sl: (b, 0, COL_WIN // kvw)),
                pl.BlockSpec((None, NSA_KV_HEADS, SAMPLE_ROWS, ns_pad), lambda b, sl: (b, 0, 0, 0)),
                part, part, part, wide,
                pl.BlockSpec((None, SAMPLE_ROWS, SMALL_WIDTH), lambda b, sl: (b, 0, 0)),
                wide,
                pl.BlockSpec((None, SAMPLE_ROWS, D_MODEL), lambda b, sl: (b, 0, COL_GMN // D_MODEL)),
            ],
            out_specs=wide,
        ),
        out_shape=jax.ShapeDtypeStruct((Bs, SAMPLE_ROWS, NSA_HEADS * HEAD_DIM), F32),
        compiler_params=_cparams(("parallel",)),
        name="finish_sample",
    )(slopes, z3s, state_win2, z3s, z3s, selmask, m, l, acc, oc, zs3s, ga, z3s)


def _mix_and_ffn(mixed, x3, ple3, wts, tm):
    B, T, _ = x3.shape
    n = B * T
    x1 = _merge_out(mixed.reshape(n, D_MODEL), x3.reshape(n, D_MODEL), wts["w_o"], wts["g_post1"], min(tm, 256))
    x3o = _ffn_ple(x1, ple3.reshape(n, PLE_DIM), wts["g_pre2"], wts["w_ffn_in"], wts["w_ffn_out"], wts["g_post2"],
                   wts["w_ple_gate"], wts["w_ple"], tm)
    return x3o.reshape(B, T, D_MODEL)


def kernel(x_prompt, x_sample, cache_cmp_kv, cache_sel_kv, page_table, state_win_kv, state_gdn, state_conv, p_prompt, p_sample, g_pre1, w_in, conv_w, A_log, dt_bias, gdn_norm_w, w_cmp, w_o, g_post1, g_pre2, w_ffn_in, w_ffn_out, g_post2, w_ple, w_ple_gate):
    B, T, _ = x_prompt.shape
    Bs, Ts, _ = x_sample.shape
    n_pages = page_table.shape[1]
    past_len = n_pages * PAGE_SIZE
    win_buf = state_win_kv.shape[2]
    kvh = (2, NSA_KV_HEADS, HEAD_DIM)
    qkv_w = 3 * GDN_WIDTH

    wi = w_in[0]
    w_main = (jnp.concatenate([wi[:, 0:8192], wi[:, 13392:17488], wi[:, 8224:13344]], axis=1).astype(BF16),)
    w_small = jnp.concatenate([wi[:, 8192:8224], wi[:, 13344:13392],
                               jnp.zeros((D_MODEL, SMALL_WIDTH - 80), F32)], axis=1).astype(BF16)
    wts = dict(w_o=w_o[0].astype(BF16), g_post1=g_post1[0], g_pre2=g_pre2[0],
               w_ffn_in=w_ffn_in[0].astype(BF16), w_ffn_out=w_ffn_out[0].astype(BF16), g_post2=g_post2[0],
               w_ple=w_ple[0].astype(BF16), w_ple_gate=w_ple_gate[0].astype(BF16))
    hp = jnp.zeros((8, SMALL_WIDTH), F32).at[0, 0:GDN_HEADS].set(A_log[0]).at[1, 0:GDN_HEADS].set(dt_bias[0])
    w_cmp2 = w_cmp[0].reshape(CMP_BLOCK, 2 * NSA_KV_WIDTH)
    heads = jnp.arange(1, NSA_HEADS + 1, dtype=F32)
    slopes = jnp.exp2(-8.0 * heads / NSA_HEADS)

    z2, zs2, zkv2 = _inproj(x_prompt.reshape(B * T, D_MODEL), g_pre1[0], w_main, w_small, 1024)
    z3, zs3 = z2.reshape(B, T, MAIN_WIDTH), zs2.reshape(B, T, SMALL_WIDTH)
    o_a, s_new_p = _gdn(z3, zs3, jnp.zeros((B, 8, qkv_w), F32), conv_w[0], hp, gdn_norm_w[0],
                        jnp.zeros((B, GDN_HEADS, HEAD_DIM, HEAD_DIM), F32), 256, 256)
    kvc, cmp_rows, sel_rows = _compress_prompt(z3, w_cmp2)
    ns = T // SEL_BLOCK
    oc, selmask = _cmp_attn(z3, zs3, kvc, slopes, 512, T, ns, ns, 0, "cmp_attn_prompt")
    mixed = _sel_win_prompt(z3, zkv2.reshape(B, T, KV_COPY_WIDTH), zs3, selmask, oc, o_a, slopes)
    y_prompt = _mix_and_ffn(mixed, x_prompt, p_prompt[0], wts, 512)

    tz = GDN_CHUNK
    xs = jnp.pad(x_sample, ((0, 0), (0, tz - Ts), (0, 0)))
    zs2_, zss2, _ = _inproj(xs.reshape(Bs * tz, D_MODEL), g_pre1[0], w_main, w_small, Bs * tz)
    z3s, zs3s = zs2_.reshape(Bs, tz, MAIN_WIDTH), zss2.reshape(Bs, tz, SMALL_WIDTH)
    conv_prev = jnp.pad(state_conv[0], ((0, 0), (8 - (CONV_WIDTH - 1), 0), (0, 0)))
    o_a_s, s_new_s = _gdn(z3s, zs3s, conv_prev, conv_w[0], hp, gdn_norm_w[0], state_gdn[0], tz, Ts, nh=GDN_HEADS)
    n_pool = cache_cmp_kv.shape[1]
    kvc_s = _compress_sample(cache_cmp_kv[0].reshape(n_pool, PAGE_SIZE * KV_ROW, HEAD_DIM), page_table,
                             w_cmp[0].reshape(CMP_BLOCK, KV_ROW, HEAD_DIM))
    kvc_s = kvc_s.reshape(Bs, kvc_s.shape[1], 2 * NSA_KV_WIDTH)
    ns_real = -(-(past_len + Ts) // SEL_BLOCK)
    ns_pad = -(-ns_real // 128) * 128
    oc_s, selmask_s = _cmp_attn(z3s, zs3s, kvc_s, slopes, SAMPLE_ROWS, SAMPLE_ROWS, ns_pad, ns_real, past_len,
                                "cmp_attn_sample")
    m_s, l_s, acc_s = _sel_sample(z3s, selmask_s, cache_sel_kv[0].reshape(n_pool, PAGE_SIZE * KV_ROW, HEAD_DIM),
                                  page_table, slopes, past_len)
    state_win2 = state_win_kv[0].reshape(Bs, win_buf, 2 * NSA_KV_WIDTH)
    mixed_s = _finish_sample(z3s, zs3s, state_win2, selmask_s, m_s, l_s, acc_s, oc_s, o_a_s, slopes, past_len, Ts)
    mixed_s = jnp.pad(mixed_s, ((0, 0), (0, tz - SAMPLE_ROWS), (0, 0)))
    ps = jnp.pad(p_sample[0], ((0, 0), (0, tz - Ts), (0, 0)))
    y_sample = _mix_and_ffn(mixed_s, xs, ps, wts, Bs * tz)[:, :Ts]

    def kv_rows(z, col, lo, hi):
        return z[:, lo:hi, col:col + 2 * NSA_KV_WIDTH].reshape((z.shape[0], hi - lo) + kvh)

    new_win_s = jnp.concatenate([state_win_kv[0][:, Ts:], kv_rows(z3s, COL_WIN, 0, Ts)], axis=1)
    return (y_prompt, y_sample,
            cmp_rows.reshape((1, B, T) + kvh), sel_rows.reshape((1, B, T) + kvh),
            kv_rows(z3, COL_WIN, T - win_buf, T)[None], s_new_p[None],
            z3[:, T - (CONV_WIDTH - 1):, 0:qkv_w][None],
            kv_rows(z3s, COL_CMP, 0, Ts)[None], kv_rows(z3s, COL_SEL, 0, Ts)[None],
            new_win_s[None], s_new_s[None],
            z3s[:, Ts - (CONV_WIDTH - 1):Ts, 0:qkv_w][None])
```

```python
import functools

import jax
import jax.numpy as jnp
from jax import lax
from jax.experimental import pallas as pl
from jax.experimental.pallas import tpu as pltpu

F32 = jnp.float32
BF16 = jnp.bfloat16
HI = lax.Precision.HIGHEST

D_MODEL = 2048
HEAD_DIM = 128
GDN_HEADS = 16
GDN_WIDTH = 2048
CONV_WIDTH = 4
GDN_CHUNK = 64
NSA_HEADS = 16
NSA_KV_HEADS = 4
NSA_GROUP = 4
NSA_KV_WIDTH = 512
CMP_BLOCK = 32
SEL_BLOCK = 64
SEL_TOPK = 16
WINDOW = 512
PLE_DIM = 256
FFN_HIDDEN = 5632
PAGE_SIZE = 128
RMS_EPS = 1e-6
NEG_INF = -1e30
FORCE_BONUS = float(NSA_GROUP + 1)
QK_SCALE = HEAD_DIM ** -0.5
LOG2E = 1.4426950408889634
MASK_DIST = 1e32

COL_QKV = 0
COL_ZA = 6144
COL_GMA = 8192
COL_GMN = 10240
COL_QN = 12288
COL_CMP = 14336
COL_SEL = 15360
COL_WIN = 16384
MAIN_WIDTH = 17408
KV_COPY_WIDTH = 2048
SMALL_WIDTH = 128
SMALL_A, SMALL_B, SMALL_GN = 0, 16, 32

VMEM_LIMIT = 56 * 1024 * 1024


def _cparams(sem):
    return pltpu.CompilerParams(dimension_semantics=sem, vmem_limit_bytes=VMEM_LIMIT)


def _bdot(a, b):
    return jnp.dot(a.astype(BF16), b.astype(BF16), preferred_element_type=F32)


def _bdot_nt(a, b):
    return lax.dot_general(a.astype(BF16), b.astype(BF16), (((1,), (1,)), ((), ())),
                           preferred_element_type=F32)


def _hdot(a, b):
    return jnp.dot(a, b, precision=HI, preferred_element_type=F32)


def _hdot_nt(a, b):
    return lax.dot_general(a, b, (((1,), (1,)), ((), ())), precision=HI, preferred_element_type=F32)


def _sigmoid(x):
    return 1.0 / (1.0 + jnp.exp(-x))


def _silu(x):
    return x * _sigmoid(x)


def _rms_rows(x):
    return lax.rsqrt(jnp.mean(x * x, axis=-1, keepdims=True) + RMS_EPS)


def _inproj_kernel(bounds, kv_lo, x_ref, g_ref, *refs):
    w_refs = refs[:len(bounds)]
    ws_ref, z_ref, zs_ref, zkv_ref, h_scr = refs[len(bounds):]
    j = pl.program_id(1)

    @pl.when(j == 0)
    def _():
        x = x_ref[...]
        h = ((x * _rms_rows(x)) * g_ref[...]).astype(BF16)
        h_scr[...] = h
        zs_ref[...] = jnp.dot(h, ws_ref[...], preferred_element_type=F32)

    for (lo, hi), w_ref in zip(bounds, w_refs):
        @pl.when((j >= lo) & (j < hi))
        def _(w_ref=w_ref, hi=hi):
            z = jnp.dot(h_scr[...], w_ref[...], preferred_element_type=F32)
            z_ref[...] = z
            if hi > kv_lo:
                @pl.when(j >= kv_lo)
                def _():
                    zkv_ref[...] = z.astype(BF16)


def _inproj(x2, g_pre1, w_slabs, w_small, tm, tn=1024):
    n = x2.shape[0]
    bounds, lo = [], 0
    for w in w_slabs:
        bounds.append((lo, lo + w.shape[1] // tn))
        lo = bounds[-1][1]
    assert lo * tn == MAIN_WIDTH

    def slab_spec(lo, hi):
        return pl.BlockSpec((D_MODEL, tn), lambda i, j: (0, jnp.clip(j - lo, 0, hi - lo - 1)))

    kv_lo, kv_n = COL_SEL // tn, KV_COPY_WIDTH // tn
    assert COL_SEL + KV_COPY_WIDTH == MAIN_WIDTH
    return pl.pallas_call(
        functools.partial(_inproj_kernel, tuple(bounds), kv_lo),
        grid=(n // tm, MAIN_WIDTH // tn),
        in_specs=[
            pl.BlockSpec((tm, D_MODEL), lambda i, j: (i, 0)),
            pl.BlockSpec((1, D_MODEL), lambda i, j: (0, 0)),
        ] + [slab_spec(lo, hi) for lo, hi in bounds] + [
            pl.BlockSpec((D_MODEL, SMALL_WIDTH), lambda i, j: (0, 0)),
        ],
        out_specs=[
            pl.BlockSpec((tm, tn), lambda i, j: (i, j)),
            pl.BlockSpec((tm, SMALL_WIDTH), lambda i, j: (i, 0)),
            pl.BlockSpec((tm, tn), lambda i, j: (i, jnp.clip(j - kv_lo, 0, kv_n - 1))),
        ],
        out_shape=[jax.ShapeDtypeStruct((n, MAIN_WIDTH), F32),
                   jax.ShapeDtypeStruct((n, SMALL_WIDTH), F32),
                   jax.ShapeDtypeStruct((n, KV_COPY_WIDTH), BF16)],
        scratch_shapes=[pltpu.VMEM((tm, D_MODEL), BF16)],
        compiler_params=_cparams(("parallel", "arbitrary")),
        name="inproj",
    )(x2, g_pre1.reshape(1, D_MODEL), *w_slabs, w_small)


def _level_mask(ii, jj, s):
    return ((ii // s) % 2 == 1) & (jj // s == ii // s - 1)


def _gdn_kernel(tb, nh, t_valid,
                q_ref, k_ref, v_ref, qh_ref, kh_ref, vh_ref, cpq_ref, cpk_ref, cpv_ref,
                cwq_ref, cwk_ref, cwv_ref, z_ref, gm_ref, zs_ref, hp_ref, nw_ref, s0_ref,
                o_ref, sn_ref, ext_scr, gt_scr, s_scr):
    C = GDN_CHUNK
    nch = tb // C
    hg = pl.program_id(1)
    t = pl.program_id(2)
    nt = pl.num_programs(2)

    @pl.when(t == 0)
    def _():
        s_scr[...] = s0_ref[...]

    def conv_silu(u_ref, halo_ref, cp_ref, cw_ref):
        prev = jnp.where(t == 0, cp_ref[...], halo_ref[...])
        ext_scr[0:8, :] = prev
        ext_scr[8:8 + tb, :] = u_ref[...]
        w = cw_ref[...]
        acc = ext_scr[5:5 + tb, :] * w[0:1, :]
        for j in range(1, CONV_WIDTH):
            acc = acc + ext_scr[5 + j:5 + j + tb, :] * w[j:j + 1, :]
        return _silu(acc)

    q_all = conv_silu(q_ref, qh_ref, cpq_ref, cwq_ref)
    k_all = conv_silu(k_ref, kh_ref, cpk_ref, cwk_ref)
    v_all = conv_silu(v_ref, vh_ref, cpv_ref, cwv_ref)

    zs = zs_ref[...]
    lane = lax.broadcasted_iota(jnp.int32, (GDN_CHUNK, SMALL_WIDTH), 1)
    xa = zs + hp_ref[1:2, :]
    softplus = jnp.maximum(xa, 0.0) + jnp.log1p(jnp.exp(-jnp.abs(xa)))
    g_all = -jnp.exp(hp_ref[0:1, :]) * softplus
    beta_all = _sigmoid(zs)
    if t_valid < tb:
        live = lax.broadcasted_iota(jnp.int32, (tb, 1), 0) < t_valid
        g_all = jnp.where(live, g_all, 0.0)
        beta_all = jnp.where(live, beta_all, 0.0)

    ii = lax.broadcasted_iota(jnp.int32, (C, C), 0)
    jj = lax.broadcasted_iota(jnp.int32, (C, C), 1)
    tril = (ii >= jj).astype(F32)
    eye = (ii == jj).astype(F32)
    gcum = []
    for c in range(nch):
        gc = _hdot(tril, g_all[c * C:(c + 1) * C])
        gcum.append(gc)
        gt_scr[c] = jnp.transpose(gc)

    Ls, rhss, qkds, kdts, qgs, egls = [], [], [], [], [], []
    for hl in range(nh):
        hglob = hg * nh + hl
        hs = slice(hl * HEAD_DIM, (hl + 1) * HEAD_DIM)
        q = q_all[:, hs]
        k = k_all[:, hs]
        q = (q * lax.rsqrt(jnp.sum(q * q, axis=-1, keepdims=True) + RMS_EPS)) * QK_SCALE
        k = k * lax.rsqrt(jnp.sum(k * k, axis=-1, keepdims=True) + RMS_EPS)
        for c in range(nch):
            sl = slice(c * C, (c + 1) * C)
            qc, kc, vc = q[sl], k[sl], v_all[sl, hs]
            bcol = jnp.sum(jnp.where(lane == SMALL_B + hglob, beta_all[sl], 0.0), axis=1, keepdims=True)
            gcol = jnp.sum(jnp.where(lane == SMALL_A + hglob, gcum[c], 0.0), axis=1, keepdims=True)
            grow = gt_scr[c, pl.ds(SMALL_A + hglob, 1), :]
            decay = jnp.exp(jnp.where(ii >= jj, gcol - grow, NEG_INF))
            qkk = _bdot_nt(jnp.concatenate([qc, kc], axis=0), kc)
            Ls.append(jnp.where(ii > jj, qkk[C:] * decay, 0.0) * bcol)
            eg = jnp.exp(gcol)
            g_last = grow[:, C - 1:C]
            rhss.append(jnp.concatenate([vc * bcol, kc * (bcol * eg)], axis=1))
            qkds.append(qkk[:C] * decay)
            kdts.append(jnp.transpose(kc * jnp.exp(g_last - gcol)))
            qgs.append(qc * eg)
            egls.append(jnp.exp(g_last))

    n_inst = nh * nch
    Xs = [eye - jnp.where(_level_mask(ii, jj, 1), L, 0.0) for L in Ls]
    s = 2
    while s < C:
        m = _level_mask(ii, jj, s)
        Ys = [_bdot(jnp.where(m, Ls[i], 0.0), Xs[i]) for i in range(n_inst)]
        Zs = [_bdot(Xs[i], Ys[i]) for i in range(n_inst)]
        Xs = [Xs[i] - Zs[i] for i in range(n_inst)]
        s *= 2
    sols = [_bdot(Xs[i], rhss[i]) for i in range(n_inst)]
    res = [rhss[i] - sols[i] - _hdot(Ls[i], sols[i]) for i in range(n_inst)]
    sols = [sols[i] + _bdot(Xs[i], res[i]) for i in range(n_inst)]
    NPs = [_bdot(kdts[i], sols[i]) for i in range(n_inst)]
    QOs = [_bdot(qkds[i], sols[i]) for i in range(n_inst)]

    Ss = [s_scr[hl] for hl in range(nh)]
    outs = [None] * n_inst
    for c in range(nch):
        for hl in range(nh):
            i = hl * nch + c
            S = Ss[hl]
            outs[i] = _bdot(qgs[i] - QOs[i][:, HEAD_DIM:], S) + QOs[i][:, :HEAD_DIM]
            Ss[hl] = S * egls[i] - _bdot(NPs[i][:, HEAD_DIM:], S) + NPs[i][:, :HEAD_DIM]
    nw = nw_ref[...]
    for hl in range(nh):
        hs = slice(hl * HEAD_DIM, (hl + 1) * HEAD_DIM)
        s_scr[hl] = Ss[hl]
        for c in range(nch):
            sl = slice(c * C, (c + 1) * C)
            o = outs[hl * nch + c]
            o_ref[sl, hs] = _sigmoid(gm_ref[sl, hs]) * (((o * _rms_rows(o)) * nw) * _silu(z_ref[sl, hs]))

    @pl.when(t == nt - 1)
    def _():
        sn_ref[...] = s_scr[...]


def _gdn(z3, zs3, conv_prev, conv_w, hp, norm_w, s0, tb, t_valid, nh=4):
    B, T, _ = z3.shape
    nt = T // tb
    hb = tb // 8
    wblk = nh * HEAD_DIM
    cq, ck, cv = COL_QKV // wblk, (COL_QKV + GDN_WIDTH) // wblk, (COL_QKV + 2 * GDN_WIDTH) // wblk
    hpg = GDN_HEADS // nh

    def main(col0):
        return pl.BlockSpec((None, tb, wblk), lambda b, h, t: (b, t, col0 + h))

    def halo(col0):
        return pl.BlockSpec((None, 8, wblk), lambda b, h, t: (b, jnp.maximum(t * hb - 1, 0), col0 + h))

    def cprev(col0):
        return pl.BlockSpec((None, 8, wblk), lambda b, h, t: (b, 0, col0 + h))

    def cw(col0):
        return pl.BlockSpec((CONV_WIDTH, wblk), lambda b, h, t: (0, col0 + h))

    state = pl.BlockSpec((None, nh, HEAD_DIM, HEAD_DIM), lambda b, h, t: (b, h, 0, 0))
    return pl.pallas_call(
        functools.partial(_gdn_kernel, tb, nh, t_valid),
        grid=(B, hpg, nt),
        in_specs=[
            main(cq), main(ck), main(cv), halo(cq), halo(ck), halo(cv),
            cprev(0), cprev(hpg), cprev(2 * hpg), cw(0), cw(hpg), cw(2 * hpg),
            main(COL_ZA // wblk), main(COL_GMA // wblk),
            pl.BlockSpec((None, tb, SMALL_WIDTH), lambda b, h, t: (b, t, 0)),
            pl.BlockSpec((8, SMALL_WIDTH), lambda b, h, t: (0, 0)),
            pl.BlockSpec((1, HEAD_DIM), lambda b, h, t: (0, 0)),
            state,
        ],
        out_specs=[pl.BlockSpec((None, tb, wblk), lambda b, h, t: (b, t, h)), state],
        out_shape=[jax.ShapeDtypeStruct((B, T, GDN_WIDTH), F32),
                   jax.ShapeDtypeStruct((B, GDN_HEADS, HEAD_DIM, HEAD_DIM), F32)],
        scratch_shapes=[pltpu.VMEM((tb + 8, wblk), F32),
                        pltpu.VMEM((tb // GDN_CHUNK, SMALL_WIDTH, GDN_CHUNK), F32),
                        pltpu.VMEM((nh, HEAD_DIM, HEAD_DIM), F32)],
        compiler_params=_cparams(("parallel", "parallel", "arbitrary")),
        name="gdn",
    )(z3, z3, z3, z3, z3, z3, conv_prev, conv_prev, conv_prev, conv_w, conv_w, conv_w,
      z3, z3, zs3, hp, norm_w.reshape(1, HEAD_DIM), s0)


def _compress_kernel(x_ref, y_ref, w_ref, o_ref, xr_ref, yr_ref):
    rows = x_ref.shape[0]
    x = x_ref[...]
    o_ref[...] = jnp.sum(x.reshape(rows // CMP_BLOCK, CMP_BLOCK, 2 * NSA_KV_WIDTH) * w_ref[...][None], axis=1)
    for k in range(KV_ROW):
        cols = slice(k * HEAD_DIM, (k + 1) * HEAD_DIM)
        xr_ref[pl.ds(k, rows, stride=KV_ROW), :] = x[:, cols]
        yr_ref[pl.ds(k, rows, stride=KV_ROW), :] = y_ref[:, cols]


def _compress_prompt(z3, w_cmp2, tc=256):
    B, T, _ = z3.shape
    nc = T // CMP_BLOCK
    kvw = 2 * NSA_KV_WIDTH
    rows_spec = pl.BlockSpec((None, tc * KV_ROW, HEAD_DIM), lambda b, i: (b, i, 0))
    return pl.pallas_call(
        _compress_kernel,
        grid=(B, T // tc),
        in_specs=[pl.BlockSpec((None, tc, kvw), lambda b, i: (b, i, COL_CMP // kvw)),
                  pl.BlockSpec((None, tc, kvw), lambda b, i: (b, i, COL_SEL // kvw)),
                  pl.BlockSpec((CMP_BLOCK, kvw), lambda b, i: (0, 0))],
        out_specs=[pl.BlockSpec((None, tc // CMP_BLOCK, kvw), lambda b, i: (b, i, 0)), rows_spec, rows_spec],
        out_shape=[jax.ShapeDtypeStruct((B, nc, kvw), F32),
                   jax.ShapeDtypeStruct((B, T * KV_ROW, HEAD_DIM), F32),
                   jax.ShapeDtypeStruct((B, T * KV_ROW, HEAD_DIM), F32)],
        compiler_params=_cparams(("parallel", "parallel")),
        name="compress_prompt",
    )(z3, z3, w_cmp2)


def _topk_mask(sc, k_sel, axis):
    n = sc.shape[axis]
    idxf = lax.broadcasted_iota(jnp.int32, sc.shape, axis).astype(F32)
    sel = jnp.zeros(sc.shape, F32)
    for _ in range(k_sel):
        m = jnp.max(sc, axis=axis, keepdims=True)
        idx = jnp.min(jnp.where(sc == m, idxf, float(n)), axis=axis, keepdims=True)
        hit = idxf == idx
        sel = jnp.where(hit, 1.0, sel)
        sc = jnp.where(hit, -3.0, sc)
    return sel


def _cmp_attn_kernel(ta, nc, ns, ns_real, t_base, slopes_ref, q_ref, kvc_ref, zs_ref, oc_ref, sel_ref, sc_scr):
    t0 = t_base + pl.program_id(1) * ta
    tpos = t0 + lax.broadcasted_iota(jnp.int32, (ta, nc), 0)
    cend = lax.broadcasted_iota(jnp.int32, (ta, nc), 1) * CMP_BLOCK + (CMP_BLOCK - 1)
    dist = tpos - cend
    valid = dist >= 0
    distf = dist.astype(F32)
    gates = _sigmoid(zs_ref[...])
    blocks_on_rows = ta % 128 == 0
    tok_ax, blk_ax = (1, 0) if blocks_on_rows else (0, 1)
    sshape = (ns, ta) if blocks_on_rows else (ta, ns)
    pshape = (ns, nc) if blocks_on_rows else (nc, ns)
    pool = (lax.broadcasted_iota(jnp.int32, pshape, tok_ax) // (SEL_BLOCK // CMP_BLOCK)
            == lax.broadcasted_iota(jnp.int32, pshape, blk_ax)).astype(F32)
    tq = t0 + lax.broadcasted_iota(jnp.int32, sshape, tok_ax)
    blk = lax.broadcasted_iota(jnp.int32, sshape, blk_ax)
    cur = tq // SEL_BLOCK
    forced = (blk == 0) | (blk == cur) | (blk == cur - 1)
    avail = blk * SEL_BLOCK <= tq
    for g in range(NSA_KV_HEADS):
        kc = kvc_ref[:, g * HEAD_DIM:(g + 1) * HEAD_DIM]
        vc = kvc_ref[:, NSA_KV_WIDTH + g * HEAD_DIM:NSA_KV_WIDTH + (g + 1) * HEAD_DIM]
        imp = jnp.zeros((ta, nc), F32)
        for r in range(NSA_GROUP):
            hd = g * NSA_GROUP + r
            qh = q_ref[:, hd * HEAD_DIM:(hd + 1) * HEAD_DIM] * QK_SCALE
            s = _hdot_nt(qh, kc) - slopes_ref[hd] * distf
            s = jnp.where(valid, s, NEG_INF)
            e = jnp.exp(s - jnp.max(s, axis=1, keepdims=True))
            p = jnp.where(valid, e / jnp.sum(e, axis=1, keepdims=True), 0.0)
            imp = imp + p
            gi = SMALL_GN + hd * 3
            oc_ref[:, hd * HEAD_DIM:(hd + 1) * HEAD_DIM] = gates[:, gi:gi + 1] * _bdot(p, vc)
        imps = _hdot_nt(pool, imp) if blocks_on_rows else _hdot(imp, pool)
        score = jnp.where(forced, imps + FORCE_BONUS, jnp.where(avail, imps, -1.0))
        if ns_real < ns:
            score = jnp.where(blk < ns_real, score, -2.0)
        if blocks_on_rows:
            sc_scr[:, g * ta:(g + 1) * ta] = score
        else:
            sc_scr[g * ta:(g + 1) * ta, :] = score
    sel = _topk_mask(sc_scr[...], min(SEL_TOPK, ns_real), blk_ax)
    for g in range(NSA_KV_HEADS):
        if blocks_on_rows:
            sel_ref[g] = jnp.transpose(sel[:, g * ta:(g + 1) * ta])
        else:
            sel_ref[g] = sel[g * ta:(g + 1) * ta, :]


def _cmp_attn(z3, zs3, kvc, slopes, ta, n_tok, ns, ns_real, t_base, name):
    B = z3.shape[0]
    nc = kvc.shape[1]
    return pl.pallas_call(
        functools.partial(_cmp_attn_kernel, ta, nc, ns, ns_real, t_base),
        grid_spec=pltpu.PrefetchScalarGridSpec(
            num_scalar_prefetch=1,
            grid=(B, n_tok // ta),
            in_specs=[
                pl.BlockSpec((None, ta, NSA_HEADS * HEAD_DIM), lambda b, i, sl: (b, i, COL_QN // 2048)),
                pl.BlockSpec((None, nc, 2 * NSA_KV_WIDTH), lambda b, i, sl: (b, 0, 0)),
                pl.BlockSpec((None, ta, SMALL_WIDTH), lambda b, i, sl: (b, i, 0)),
            ],
            out_specs=[
                pl.BlockSpec((None, ta, NSA_HEADS * HEAD_DIM), lambda b, i, sl: (b, i, 0)),
                pl.BlockSpec((None, NSA_KV_HEADS, ta, ns), lambda b, i, sl: (b, 0, i, 0)),
            ],
            scratch_shapes=[pltpu.VMEM((ns, NSA_KV_HEADS * ta) if ta % 128 == 0 else (NSA_KV_HEADS * ta, ns), F32)],
        ),
        out_shape=[jax.ShapeDtypeStruct((B, n_tok, NSA_HEADS * HEAD_DIM), F32),
                   jax.ShapeDtypeStruct((B, NSA_KV_HEADS, n_tok, ns), F32)],
        compiler_params=_cparams(("parallel", "parallel")),
        name=name,
    )(slopes, z3, kvc, zs3)


def _sel_win_kernel(T, QB, ns, tkv, wspan, slopes_ref, q_ref, ks_ref, vs_ref, kw_ref, vw_ref, sel_ref,
                    zs_ref, oc_ref, ga_ref, gmn_ref, o_ref):
    R = NSA_GROUP
    g = pl.program_id(1)
    qb = pl.program_id(2)
    t0 = qb * QB
    q = jnp.concatenate([(q_ref[:, r * HEAD_DIM:(r + 1) * HEAD_DIM] * (QK_SCALE * LOG2E)).astype(BF16)
                         for r in range(R)], axis=0)
    slope2 = [slopes_ref[g * R + r] * LOG2E for r in range(R)]
    selb = sel_ref[...].astype(BF16)

    def head_rows(x, r):
        return x[r * QB:(r + 1) * QB]

    def spread(cols):
        return jnp.concatenate([jnp.broadcast_to(c, (QB, 2 * HEAD_DIM)) for c in cols], axis=0)

    ti = lax.broadcasted_iota(jnp.int32, (QB, tkv), 0)
    kj = lax.broadcasted_iota(jnp.int32, (QB, tkv), 1)
    d0 = (ti - kj).astype(F32)
    eb = lax.broadcasted_iota(jnp.int32, (ns, tkv), 0)
    ek = lax.broadcasted_iota(jnp.int32, (ns, tkv), 1) // SEL_BLOCK

    def sel_scores(j):
        k0 = j * tkv
        expand = (eb == ek + j * (tkv // SEL_BLOCK)).astype(BF16)
        keymask = jnp.dot(selb, expand, preferred_element_type=F32)
        distf = d0 + (t0 - k0).astype(F32)
        base = jnp.where((distf >= 0.0) & (keymask > 0.5), distf, MASK_DIST)
        return _bdot_nt(q, ks_ref[k0:k0 + tkv, :]), base

    def with_ones(v):
        return jnp.concatenate([v.astype(BF16), jnp.ones(v.shape, BF16)], axis=1)

    def sel_tile(j, s, base, carry):
        ms, acc = carry
        k0 = j * tkv
        ps, ms2, alphas = [], [], []
        for r in range(R):
            sr = head_rows(s, r) - slope2[r] * base
            m_new = jnp.maximum(ms[r], jnp.max(sr, axis=1, keepdims=True))
            ps.append(jnp.exp2((sr - m_new).astype(BF16)))
            alphas.append(jnp.exp2(ms[r] - m_new))
            ms2.append(m_new)
        pv = jnp.dot(jnp.concatenate(ps, axis=0), with_ones(vs_ref[k0:k0 + tkv, :]),
                     preferred_element_type=F32)
        return ms2, spread(alphas) * acc + pv

    def window():
        kstart = pl.multiple_of(jnp.clip(t0 - WINDOW, 0, T - wspan), QB)
        kw = kw_ref[pl.ds(kstart, wspan), :]
        vw = vw_ref[pl.ds(kstart, wspan), :]
        tw = lax.broadcasted_iota(jnp.int32, (QB, wspan), 0)
        kwj = lax.broadcasted_iota(jnp.int32, (QB, wspan), 1)
        dist = (tw - kwj) + (t0 - kstart)
        base_w = jnp.where((dist >= 0) & (dist < WINDOW), dist.astype(F32), MASK_DIST)
        s = _bdot_nt(q, kw)
        pw = []
        for r in range(R):
            sr = head_rows(s, r) - slope2[r] * base_w
            pw.append(jnp.exp2((sr - jnp.max(sr, axis=1, keepdims=True)).astype(BF16)))
        return jnp.dot(jnp.concatenate(pw, axis=0), with_ones(vw), preferred_element_type=F32)

    def run(n_tiles):
        carry = ([jnp.full((QB, 1), NEG_INF, F32) for _ in range(R)],
                 jnp.zeros((R * QB, 2 * HEAD_DIM), F32))
        nxt = sel_scores(0)
        acc_w = window()
        for j in range(n_tiles):
            cur, nxt = nxt, (sel_scores(j + 1) if j + 1 < n_tiles else None)
            carry = sel_tile(j, cur[0], cur[1], carry)
        acc_s = carry[1]
        o_s = acc_s[:, :HEAD_DIM] / acc_s[:, HEAD_DIM:]
        o_w = acc_w[:, :HEAD_DIM] / acc_w[:, HEAD_DIM:]
        gates = _sigmoid(zs_ref[...])
        lane = lax.broadcasted_iota(jnp.int32, (QB, SMALL_WIDTH), 1)
        for r in range(R):
            gi = SMALL_GN + (g * R + r) * 3
            g_s = jnp.sum(jnp.where(lane == gi + 1, gates, 0.0), axis=1, keepdims=True)
            g_w = jnp.sum(jnp.where(lane == gi + 2, gates, 0.0), axis=1, keepdims=True)
            cs = slice(r * HEAD_DIM, (r + 1) * HEAD_DIM)
            o_n = oc_ref[:, cs] + g_s * head_rows(o_s, r) + g_w * head_rows(o_w, r)
            o_ref[:, cs] = (ga_ref[:, cs] + _sigmoid(gmn_ref[:, cs]) * o_n).astype(BF16)

    need = (t0 + QB + tkv - 1) // tkv
    for n_tiles in range(1, T // tkv + 1):
        pl.when(need == n_tiles)(functools.partial(run, n_tiles))


def _sel_win_prompt(z3, zkv3, zs3, selmask, oc, ga, slopes):
    B, T, _ = z3.shape
    ns = T // SEL_BLOCK
    tkv = min(1024, T)
    qb = 128
    wspan = min(WINDOW + 2 * qb, T)
    gw = NSA_GROUP * HEAD_DIM

    def kv(col0):
        return pl.BlockSpec((None, T, HEAD_DIM), lambda b, g, i, sl: (b, 0, (col0 - COL_SEL) // 128 + g))

    return pl.pallas_call(
        functools.partial(_sel_win_kernel, T, qb, ns, tkv, wspan),
        grid_spec=pltpu.PrefetchScalarGridSpec(
            num_scalar_prefetch=1,
            grid=(B, NSA_KV_HEADS, T // qb),
            in_specs=[
                pl.BlockSpec((None, qb, gw), lambda b, g, i, sl: (b, i, COL_QN // gw + g)),
                kv(COL_SEL), kv(COL_SEL + NSA_KV_WIDTH), kv(COL_WIN), kv(COL_WIN + NSA_KV_WIDTH),
                pl.BlockSpec((None, None, qb, ns), lambda b, g, i, sl: (b, g, i, 0)),
                pl.BlockSpec((None, qb, SMALL_WIDTH), lambda b, g, i, sl: (b, i, 0)),
                pl.BlockSpec((None, qb, gw), lambda b, g, i, sl: (b, i, g)),
                pl.BlockSpec((None, qb, gw), lambda b, g, i, sl: (b, i, g)),
                pl.BlockSpec((None, qb, gw), lambda b, g, i, sl: (b, i, COL_GMN // gw + g)),
            ],
            out_specs=pl.BlockSpec((None, qb, gw), lambda b, g, i, sl: (b, i, g)),
        ),
        out_shape=jax.ShapeDtypeStruct((B, T, D_MODEL), BF16),
        compiler_params=_cparams(("parallel", "parallel", "arbitrary")),
        name="sel_win_prompt",
    )(slopes, z3, zkv3, zkv3, zkv3, zkv3, selmask, zs3, oc, ga, z3)


def _merge_kernel(mixed_ref, x_ref, w_ref, g_ref, o_ref):
    y = jnp.dot(mixed_ref[...].astype(BF16), w_ref[...], preferred_element_type=F32)
    o_ref[...] = x_ref[...] + (y * _rms_rows(y)) * g_ref[...]


def _merge_out(mixed2, x2, w_o, g_post1, tm):
    n = x2.shape[0]
    row = pl.BlockSpec((tm, D_MODEL), lambda i: (i, 0))
    return pl.pallas_call(
        _merge_kernel,
        grid=(n // tm,),
        in_specs=[row, row,
                  pl.BlockSpec((D_MODEL, D_MODEL), lambda i: (0, 0)),
                  pl.BlockSpec((1, D_MODEL), lambda i: (0, 0))],
        out_specs=row,
        out_shape=jax.ShapeDtypeStruct((n, D_MODEL), F32),
        compiler_params=_cparams(("parallel",)),
        name="merge_out",
    )(mixed2, x2, w_o, g_post1.reshape(1, D_MODEL))


def _ffn_kernel(x_ref, g2_ref, wg_ref, wu_ref, wo_ref, gp_ref, p_ref, wpg_ref, wp_ref, o_ref, h_scr, acc_scr):
    j = pl.program_id(1)

    @pl.when(j == 0)
    def _():
        x = x_ref[...]
        h_scr[...] = ((x * _rms_rows(x)) * g2_ref[...]).astype(BF16)
        acc_scr[...] = jnp.zeros_like(acc_scr)

    h = h_scr[...]
    gt = jnp.dot(h, wg_ref[...], preferred_element_type=F32)
    up = jnp.dot(h, wu_ref[...], preferred_element_type=F32)
    acc_scr[...] += jnp.dot((_silu(gt) * up).astype(BF16), wo_ref[...], preferred_element_type=F32)

    @pl.when(j == pl.num_programs(1) - 1)
    def _():
        y = acc_scr[...]
        x = x_ref[...] + (y * _rms_rows(y)) * gp_ref[...]
        gate = _sigmoid(jnp.dot(x.astype(BF16), wpg_ref[...], preferred_element_type=F32))
        o_ref[...] = x + gate * jnp.dot(p_ref[...].astype(BF16), wp_ref[...], preferred_element_type=F32)


def _ffn_ple(x2, ple2, g_pre2, w_ffn_in, w_ffn_out, g_post2, w_ple_gate, w_ple, tm, th=512):
    n = x2.shape[0]
    nh = FFN_HIDDEN // th
    once = pl.Buffered(1)
    return pl.pallas_call(
        _ffn_kernel,
        grid=(n // tm, nh),
        in_specs=[
            pl.BlockSpec((tm, D_MODEL), lambda i, j: (i, 0)),
            pl.BlockSpec((1, D_MODEL), lambda i, j: (0, 0)),
            pl.BlockSpec((D_MODEL, th), lambda i, j: (0, j)),
            pl.BlockSpec((D_MODEL, th), lambda i, j: (0, nh + j)),
            pl.BlockSpec((th, D_MODEL), lambda i, j: (j, 0)),
            pl.BlockSpec((1, D_MODEL), lambda i, j: (0, 0)),
            pl.BlockSpec((tm, PLE_DIM), lambda i, j: (i, 0)),
            pl.BlockSpec((D_MODEL, D_MODEL), lambda i, j: (0, 0), pipeline_mode=once),
            pl.BlockSpec((PLE_DIM, D_MODEL), lambda i, j: (0, 0), pipeline_mode=once),
        ],
        out_specs=pl.BlockSpec((tm, D_MODEL), lambda i, j: (i, 0)),
        out_shape=jax.ShapeDtypeStruct((n, D_MODEL), F32),
        scratch_shapes=[pltpu.VMEM((tm, D_MODEL), BF16), pltpu.VMEM((tm, D_MODEL), F32)],
        compiler_params=_cparams(("parallel", "arbitrary")),
        name="ffn_ple",
    )(x2, g_pre2.reshape(1, D_MODEL), w_ffn_in, w_ffn_in, w_ffn_out, g_post2.reshape(1, D_MODEL),
      ple2, w_ple_gate, w_ple)


SAMPLE_ROWS = 8
KV_ROW = 2 * NSA_KV_HEADS


def _compress_pages_kernel(npg, pt_ref, *refs):
    w = refs[npg][...]
    o_ref = refs[npg + 1]
    per = PAGE_SIZE // CMP_BLOCK
    for p in range(npg):
        x = refs[p][...].reshape(per, CMP_BLOCK, KV_ROW, HEAD_DIM)
        o_ref[p * per:(p + 1) * per] = jnp.sum(x * w[None], axis=1)


def _compress_sample(cache_rows, page_table, w_cmp3, npg=16):
    Bs, n_pages = page_table.shape
    per = PAGE_SIZE // CMP_BLOCK

    def page(p):
        return pl.BlockSpec((None, PAGE_SIZE * KV_ROW, HEAD_DIM), lambda b, i, pt: (pt[b, i * npg + p], 0, 0))

    return pl.pallas_call(
        functools.partial(_compress_pages_kernel, npg),
        grid_spec=pltpu.PrefetchScalarGridSpec(
            num_scalar_prefetch=1,
            grid=(Bs, n_pages // npg),
            in_specs=[page(p) for p in range(npg)]
            + [pl.BlockSpec((CMP_BLOCK, KV_ROW, HEAD_DIM), lambda b, i, pt: (0, 0, 0))],
            out_specs=pl.BlockSpec((None, npg * per, KV_ROW, HEAD_DIM), lambda b, i, pt: (b, i, 0, 0)),
        ),
        out_shape=jax.ShapeDtypeStruct((Bs, n_pages * per, KV_ROW, HEAD_DIM), F32),
        compiler_params=_cparams(("parallel", "parallel")),
        name="compress_sample",
    )(page_table, *([cache_rows] * npg), w_cmp3)


def _sel_pages_kernel(npg, past_len, phys_ref, lst_ref, cnt_ref, slopes_ref, q_ref, sel_ref, *refs):
    pages = refs[:npg]
    m_ref, l_ref, acc_ref, q_scr, selrows_scr, slope_scr = refs[npg:]
    b = pl.program_id(0)
    i = pl.program_id(1)
    nrow = NSA_HEADS * SAMPLE_ROWS
    rg = NSA_GROUP * SAMPLE_ROWS
    ns_pad = sel_ref.shape[-1]

    @pl.when(i == 0)
    def _():
        for hd in range(NSA_HEADS):
            rs = slice(hd * SAMPLE_ROWS, (hd + 1) * SAMPLE_ROWS)
            q_scr[rs, :] = q_ref[:, hd * HEAD_DIM:(hd + 1) * HEAD_DIM] * (QK_SCALE * LOG2E)
            selrows_scr[rs, :] = sel_ref[hd // NSA_GROUP]
            slope_scr[rs, :] = jnp.full((SAMPLE_ROWS, HEAD_DIM), slopes_ref[hd] * LOG2E, F32)
        m_ref[...] = jnp.full(m_ref.shape, NEG_INF, F32)
        l_ref[...] = jnp.zeros(l_ref.shape, F32)
        acc_ref[...] = jnp.zeros(acc_ref.shape, F32)

    @pl.when(i * npg < cnt_ref[b])
    def _():
        qb = q_scr[...].astype(BF16)
        selb = selrows_scr[...].astype(BF16)
        slope = slope_scr[...]
        row = lax.broadcasted_iota(jnp.int32, (nrow, PAGE_SIZE), 0)
        pos = lax.broadcasted_iota(jnp.int32, (nrow, PAGE_SIZE), 1)
        d0 = (past_len + row % SAMPLE_ROWS - pos).astype(F32)
        ob = lax.broadcasted_iota(jnp.int32, (ns_pad, PAGE_SIZE), 0)
        ol = lax.broadcasted_iota(jnp.int32, (ns_pad, PAGE_SIZE), 1) // SEL_BLOCK
        scores = []
        for p in range(npg):
            k = i * npg + p
            pg = lst_ref[b, k]
            flags = jnp.dot(selb, (ob == ol + pg * (PAGE_SIZE // SEL_BLOCK)).astype(BF16),
                            preferred_element_type=F32)
            distf = d0 - (pg * PAGE_SIZE).astype(F32)
            base = jnp.where((flags > 0.5) & (distf >= 0.0), distf, MASK_DIST)
            base = jnp.where(k < cnt_ref[b], base, MASK_DIST)
            scores.append(jnp.concatenate(
                [_bdot_nt(qb[g * rg:(g + 1) * rg], pages[p][pl.ds(g, PAGE_SIZE, stride=KV_ROW), :])
                 for g in range(NSA_KV_HEADS)], axis=0) - slope * base)
        s = jnp.concatenate(scores, axis=1)
        m_old = m_ref[...]
        m_new = jnp.maximum(m_old, jnp.max(s, axis=1, keepdims=True))
        pr = jnp.exp2(s - m_new[:, 0:1])
        alpha = jnp.exp2(m_old - m_new)
        l_ref[...] = alpha * l_ref[...] + jnp.sum(pr, axis=1, keepdims=True)
        prb = pr.astype(BF16)
        pv = jnp.zeros((nrow, HEAD_DIM), F32)
        for p in range(npg):
            ps = prb[:, p * PAGE_SIZE:(p + 1) * PAGE_SIZE]
            pv = pv + jnp.concatenate(
                [_bdot(ps[g * rg:(g + 1) * rg], pages[p][pl.ds(NSA_KV_HEADS + g, PAGE_SIZE, stride=KV_ROW), :])
                 for g in range(NSA_KV_HEADS)], axis=0)
        acc_ref[...] = alpha * acc_ref[...] + pv
        m_ref[...] = m_new


def _sel_sample(z3s, selmask, cache_rows, page_table, slopes, past_len, npg=8):
    Bs, n_pages = page_table.shape
    nrow = NSA_HEADS * SAMPLE_ROWS
    ns_pad = selmask.shape[-1]
    per_page = PAGE_SIZE // SEL_BLOCK
    picked = selmask[..., :n_pages * per_page].reshape(Bs, -1, n_pages, per_page).max(axis=(1, 3)) > 0.5
    cnt = picked.sum(axis=1).astype(jnp.int32)
    order = jnp.argsort(jnp.logical_not(picked), axis=1, stable=True).astype(jnp.int32)
    last = jnp.take_along_axis(order, jnp.maximum(cnt - 1, 0)[:, None], axis=1)
    lst = jnp.where(lax.broadcasted_iota(jnp.int32, order.shape, 1) < cnt[:, None], order, last)
    phys = jnp.take_along_axis(page_table, lst, axis=1)

    def page(p):
        return pl.BlockSpec((None, PAGE_SIZE * KV_ROW, HEAD_DIM),
                            lambda b, i, ph, ls, ct, sl: (ph[b, i * npg + p], 0, 0))

    part = pl.BlockSpec((None, nrow, HEAD_DIM), lambda b, i, ph, ls, ct, sl: (b, 0, 0))
    return pl.pallas_call(
        functools.partial(_sel_pages_kernel, npg, past_len),
        grid_spec=pltpu.PrefetchScalarGridSpec(
            num_scalar_prefetch=4,
            grid=(Bs, n_pages // npg),
            in_specs=[
                pl.BlockSpec((None, SAMPLE_ROWS, NSA_HEADS * HEAD_DIM),
                             lambda b, i, ph, ls, ct, sl: (b, 0, COL_QN // 2048)),
                pl.BlockSpec((None, NSA_KV_HEADS, SAMPLE_ROWS, ns_pad), lambda b, i, ph, ls, ct, sl: (b, 0, 0, 0)),
            ] + [page(p) for p in range(npg)],
            out_specs=[part, part, part],
            scratch_shapes=[pltpu.VMEM((nrow, HEAD_DIM), F32), pltpu.VMEM((nrow, ns_pad), F32),
                            pltpu.VMEM((nrow, HEAD_DIM), F32)],
        ),
        out_shape=[jax.ShapeDtypeStruct((Bs, nrow, HEAD_DIM), F32)] * 3,
        compiler_params=_cparams(("parallel", "arbitrary")),
        name="sel_sample",
    )(phys, lst, cnt, slopes, z3s, selmask, *([cache_rows] * npg))


def _finish_sample_kernel(past_len, t_real, nnew, slopes_ref, q_ref, kst_ref, snew_ref, wnew_ref, sel_ref,
                          m_ref, l_ref, acc_ref, oc_ref, zs_ref, ga_ref, gmn_ref, o_ref):
    nst = kst_ref.shape[0]
    cur = past_len // SEL_BLOCK
    gates = _sigmoid(zs_ref[...])
    t_new = lax.broadcasted_iota(jnp.int32, (SAMPLE_ROWS, nnew), 0)
    j_new = lax.broadcasted_iota(jnp.int32, (SAMPLE_ROWS, nnew), 1)
    dist_new = t_new - j_new
    ok_new = (dist_new >= 0) & (j_new < t_real)
    t_st = lax.broadcasted_iota(jnp.int32, (SAMPLE_ROWS, nst), 0)
    i_st = lax.broadcasted_iota(jnp.int32, (SAMPLE_ROWS, nst), 1)
    dist_st = t_st + nst - i_st
    ok_st = dist_st < WINDOW
    for hd in range(NSA_HEADS):
        g = hd // NSA_GROUP
        rs = slice(hd * SAMPLE_ROWS, (hd + 1) * SAMPLE_ROWS)
        kc = slice(g * HEAD_DIM, (g + 1) * HEAD_DIM)
        vc = slice(NSA_KV_WIDTH + g * HEAD_DIM, NSA_KV_WIDTH + (g + 1) * HEAD_DIM)
        sl = slopes_ref[hd] * LOG2E
        qh = (q_ref[:, hd * HEAD_DIM:(hd + 1) * HEAD_DIM] * (QK_SCALE * LOG2E)).astype(BF16)
        valid = ok_new & (sel_ref[g][:, cur:cur + 1] > 0.5)
        s = jnp.where(valid, _bdot_nt(qh, snew_ref[:, kc]) - sl * dist_new.astype(F32), NEG_INF)
        m_old = m_ref[rs, 0:1]
        m_new = jnp.maximum(m_old, jnp.max(s, axis=1, keepdims=True))
        pr = jnp.where(valid, jnp.exp2(s - m_new), 0.0)
        alpha = jnp.exp2(m_old - m_new)
        l = alpha * l_ref[rs, 0:1] + jnp.sum(pr, axis=1, keepdims=True)
        o_s = (alpha * acc_ref[rs, :] + _bdot(pr, snew_ref[:, vc])) / l
        s1 = jnp.where(ok_st, _bdot_nt(qh, kst_ref[:, kc]) - sl * dist_st.astype(F32), NEG_INF)
        s2 = jnp.where(ok_new, _bdot_nt(qh, wnew_ref[:, kc]) - sl * dist_new.astype(F32), NEG_INF)
        mw = jnp.maximum(jnp.max(s1, axis=1, keepdims=True), jnp.max(s2, axis=1, keepdims=True))
        e1 = jnp.where(ok_st, jnp.exp2(s1 - mw), 0.0)
        e2 = jnp.where(ok_new, jnp.exp2(s2 - mw), 0.0)
        den = jnp.sum(e1, axis=1, keepdims=True) + jnp.sum(e2, axis=1, keepdims=True)
        o_w = _bdot(e1 / den, kst_ref[:, vc]) + _bdot(e2 / den, wnew_ref[:, vc])
        gi = SMALL_GN + hd * 3
        cs = slice(hd * HEAD_DIM, (hd + 1) * HEAD_DIM)
        o_n = oc_ref[:, cs] + gates[:, gi + 1:gi + 2] * o_s + gates[:, gi + 2:gi + 3] * o_w
        o_ref[:, cs] = ga_ref[:, cs] + _sigmoid(gmn_ref[:, cs]) * o_n


def _finish_sample(z3s, zs3s, state_win2, selmask, m, l, acc, oc, ga, slopes, past_len, t_real):
    Bs, tz, _ = z3s.shape
    nst = state_win2.shape[1]
    nrow = NSA_HEADS * SAMPLE_ROWS
    ns_pad = selmask.shape[-1]
    kvw = 2 * NSA_KV_WIDTH
    part = pl.BlockSpec((None, nrow, HEAD_DIM), lambda b, sl: (b, 0, 0))
    wide = pl.BlockSpec((None, SAMPLE_ROWS, NSA_HEADS * HEAD_DIM), lambda b, sl: (b, 0, 0))
    return pl.pallas_call(
        functools.partial(_finish_sample_kernel, past_len, t_real, tz),
        grid_spec=pltpu.PrefetchScalarGridSpec(
            num_scalar_prefetch=1,
            grid=(Bs,),
            in_specs=[
                pl.BlockSpec((None, SAMPLE_ROWS, NSA_HEADS * HEAD_DIM), lambda b, sl: (b, 0, COL_QN // 2048)),
                pl.BlockSpec((None, nst, kvw), lambda b, sl: (b, 0, 0)),
                pl.BlockSpec((None, tz, kvw), lambda b, sl: (b, 0, COL_SEL // kvw)),
                pl.BlockSpec((None, tz, kvw), lambda b, sl: (b, 0, COL_WIN // kvw)),
                pl.BlockSpec((None, NSA_KV_HEADS, SAMPLE_ROWS, ns_pad), lambda b, sl: (b, 0, 0, 0)),
                part, part, part, wide,
                pl.BlockSpec((None, SAMPLE_ROWS, SMALL_WIDTH), lambda b, sl: (b, 0, 0)),
                wide,
                pl.BlockSpec((None, SAMPLE_ROWS, D_MODEL), lambda b, sl: (b, 0, COL_GMN // D_MODEL)),
            ],
            out_specs=wide,
        ),
        out_shape=jax.ShapeDtypeStruct((Bs, SAMPLE_ROWS, NSA_HEADS * HEAD_DIM), F32),
        compiler_params=_cparams(("parallel",)),
        name="finish_sample",
    )(slopes, z3s, state_win2, z3s, z3s, selmask, m, l, acc, oc, zs3s, ga, z3s)


def _mix_and_ffn(mixed, x3, ple3, wts, tm):
    B, T, _ = x3.shape
    n = B * T
    x1 = _merge_out(mixed.reshape(n, D_MODEL), x3.reshape(n, D_MODEL), wts["w_o"], wts["g_post1"], tm)
    x3o = _ffn_ple(x1, ple3.reshape(n, PLE_DIM), wts["g_pre2"], wts["w_ffn_in"], wts["w_ffn_out"], wts["g_post2"],
                   wts["w_ple_gate"], wts["w_ple"], tm)
    return x3o.reshape(B, T, D_MODEL)


def kernel(x_prompt, x_sample, cache_cmp_kv, cache_sel_kv, page_table, state_win_kv, state_gdn, state_conv, p_prompt, p_sample, g_pre1, w_in, conv_w, A_log, dt_bias, gdn_norm_w, w_cmp, w_o, g_post1, g_pre2, w_ffn_in, w_ffn_out, g_post2, w_ple, w_ple_gate):
    B, T, _ = x_prompt.shape
    Bs, Ts, _ = x_sample.shape
    n_pages = page_table.shape[1]
    past_len = n_pages * PAGE_SIZE
    win_buf = state_win_kv.shape[2]
    kvh = (2, NSA_KV_HEADS, HEAD_DIM)
    qkv_w = 3 * GDN_WIDTH

    wi = w_in[0]
    w_main = (jnp.concatenate([wi[:, 0:8192], wi[:, 13392:17488], wi[:, 8224:13344]], axis=1).astype(BF16),)
    w_small = jnp.concatenate([wi[:, 8192:8224], wi[:, 13344:13392],
                               jnp.zeros((D_MODEL, SMALL_WIDTH - 80), F32)], axis=1).astype(BF16)
    wts = dict(w_o=w_o[0].astype(BF16), g_post1=g_post1[0], g_pre2=g_pre2[0],
               w_ffn_in=w_ffn_in[0].astype(BF16), w_ffn_out=w_ffn_out[0].astype(BF16), g_post2=g_post2[0],
               w_ple=w_ple[0].astype(BF16), w_ple_gate=w_ple_gate[0].astype(BF16))
    hp = jnp.zeros((8, SMALL_WIDTH), F32).at[0, 0:GDN_HEADS].set(A_log[0]).at[1, 0:GDN_HEADS].set(dt_bias[0])
    w_cmp2 = w_cmp[0].reshape(CMP_BLOCK, 2 * NSA_KV_WIDTH)
    heads = jnp.arange(1, NSA_HEADS + 1, dtype=F32)
    slopes = jnp.exp2(-8.0 * heads / NSA_HEADS)

    z2, zs2, zkv2 = _inproj(x_prompt.reshape(B * T, D_MODEL), g_pre1[0], w_main, w_small, 1024)
    z3, zs3 = z2.reshape(B, T, MAIN_WIDTH), zs2.reshape(B, T, SMALL_WIDTH)
    o_a, s_new_p = _gdn(z3, zs3, jnp.zeros((B, 8, qkv_w), F32), conv_w[0], hp, gdn_norm_w[0],
                        jnp.zeros((B, GDN_HEADS, HEAD_DIM, HEAD_DIM), F32), 256, 256)
    kvc, cmp_rows, sel_rows = _compress_prompt(z3, w_cmp2)
    ns = T // SEL_BLOCK
    oc, selmask = _cmp_attn(z3, zs3, kvc, slopes, 1024, T, ns, ns, 0, "cmp_attn_prompt")
    mixed = _sel_win_prompt(z3, zkv2.reshape(B, T, KV_COPY_WIDTH), zs3, selmask, oc, o_a, slopes)
    y_prompt = _mix_and_ffn(mixed, x_prompt, p_prompt[0], wts, 512)

    tz = GDN_CHUNK
    xs = jnp.pad(x_sample, ((0, 0), (0, tz - Ts), (0, 0)))
    zs2_, zss2, _ = _inproj(xs.reshape(Bs * tz, D_MODEL), g_pre1[0], w_main, w_small, Bs * tz)
    z3s, zs3s = zs2_.reshape(Bs, tz, MAIN_WIDTH), zss2.reshape(Bs, tz, SMALL_WIDTH)
    conv_prev = jnp.pad(state_conv[0], ((0, 0), (8 - (CONV_WIDTH - 1), 0), (0, 0)))
    o_a_s, s_new_s = _gdn(z3s, zs3s, conv_prev, conv_w[0], hp, gdn_norm_w[0], state_gdn[0], tz, Ts, nh=GDN_HEADS)
    n_pool = cache_cmp_kv.shape[1]
    kvc_s = _compress_sample(cache_cmp_kv[0].reshape(n_pool, PAGE_SIZE * KV_ROW, HEAD_DIM), page_table,
                             w_cmp[0].reshape(CMP_BLOCK, KV_ROW, HEAD_DIM))
    kvc_s = kvc_s.reshape(Bs, kvc_s.shape[1], 2 * NSA_KV_WIDTH)
    ns_real = -(-(past_len + Ts) // SEL_BLOCK)
    ns_pad = -(-ns_real // 128) * 128
    oc_s, selmask_s = _cmp_attn(z3s, zs3s, kvc_s, slopes, SAMPLE_ROWS, SAMPLE_ROWS, ns_pad, ns_real, past_len,
                                "cmp_attn_sample")
    m_s, l_s, acc_s = _sel_sample(z3s, selmask_s, cache_sel_kv[0].reshape(n_pool, PAGE_SIZE * KV_ROW, HEAD_DIM),
                                  page_table, slopes, past_len)
    state_win2 = state_win_kv[0].reshape(Bs, win_buf, 2 * NSA_KV_WIDTH)
    mixed_s = _finish_sample(z3s, zs3s, state_win2, selmask_s, m_s, l_s, acc_s, oc_s, o_a_s, slopes, past_len, Ts)
    mixed_s = jnp.pad(mixed_s, ((0, 0), (0, tz - SAMPLE_ROWS), (0, 0)))
    ps = jnp.pad(p_sample[0], ((0, 0), (0, tz - Ts), (0, 0)))
    y_sample = _mix_and_ffn(mixed_s, xs, ps, wts, Bs * tz)[:, :Ts]

    def kv_rows(z, col, lo, hi):
        return z[:, lo:hi, col:col + 2 * NSA_KV_WIDTH].reshape((z.shape[0], hi - lo) + kvh)

    new_win_s = jnp.concatenate([state_win_kv[0][:, Ts:], kv_rows(z3s, COL_WIN, 0, Ts)], axis=1)
    return (y_prompt, y_sample,
            cmp_rows.reshape((1, B, T) + kvh), sel_rows.reshape((1, B, T) + kvh),
            kv_rows(z3, COL_WIN, T - win_buf, T)[None], s_new_p[None],
            z3[:, T - (CONV_WIDTH - 1):, 0:qkv_w][None],
            kv_rows(z3s, COL_CMP, 0, Ts)[None], kv_rows(z3s, COL_SEL, 0, Ts)[None],
            new_win_s[None], s_new_s[None],
            z3s[:, Ts - (CONV_WIDTH - 1):Ts, 0:qkv_w][None])
```

```python
import functools

import jax
import jax.numpy as jnp
from jax import lax
from jax.experimental import pallas as pl
from jax.experimental.pallas import tpu as pltpu

F32 = jnp.float32
BF16 = jnp.bfloat16
HI = lax.Precision.HIGHEST

D_MODEL = 2048
HEAD_DIM = 128
GDN_HEADS = 16
GDN_WIDTH = 2048
CONV_WIDTH = 4
GDN_CHUNK = 64
NSA_HEADS = 16
NSA_KV_HEADS = 4
NSA_GROUP = 4
NSA_KV_WIDTH = 512
CMP_BLOCK = 32
SEL_BLOCK = 64
SEL_TOPK = 16
WINDOW = 512
PLE_DIM = 256
FFN_HIDDEN = 5632
PAGE_SIZE = 128
RMS_EPS = 1e-6
NEG_INF = -1e30
FORCE_BONUS = float(NSA_GROUP + 1)
QK_SCALE = HEAD_DIM ** -0.5
LOG2E = 1.4426950408889634
MASK_DIST = 1e32

COL_QKV = 0
COL_ZA = 6144
COL_GMA = 8192
COL_GMN = 10240
COL_QN = 12288
COL_CMP = 14336
COL_SEL = 15360
COL_WIN = 16384
MAIN_WIDTH = 17408
KV_COPY_WIDTH = 2048
SMALL_WIDTH = 128
SMALL_A, SMALL_B, SMALL_GN = 0, 16, 32

VMEM_LIMIT = 56 * 1024 * 1024


def _cparams(sem):
    return pltpu.CompilerParams(dimension_semantics=sem, vmem_limit_bytes=VMEM_LIMIT)


def _bdot(a, b):
    return jnp.dot(a.astype(BF16), b.astype(BF16), preferred_element_type=F32)


def _bdot_nt(a, b):
    return lax.dot_general(a.astype(BF16), b.astype(BF16), (((1,), (1,)), ((), ())),
                           preferred_element_type=F32)


def _hdot(a, b):
    return jnp.dot(a, b, precision=HI, preferred_element_type=F32)


def _hdot_nt(a, b):
    return lax.dot_general(a, b, (((1,), (1,)), ((), ())), precision=HI, preferred_element_type=F32)


def _sigmoid(x):
    return 1.0 / (1.0 + jnp.exp(-x))


def _silu(x):
    return x * _sigmoid(x)


def _rms_rows(x):
    return lax.rsqrt(jnp.mean(x * x, axis=-1, keepdims=True) + RMS_EPS)


def _inproj_kernel(bounds, kv_lo, x_ref, g_ref, *refs):
    w_refs = refs[:len(bounds)]
    ws_ref, z_ref, zs_ref, zkv_ref, h_scr = refs[len(bounds):]
    j = pl.program_id(1)

    @pl.when(j == 0)
    def _():
        x = x_ref[...]
        h = ((x * _rms_rows(x)) * g_ref[...]).astype(BF16)
        h_scr[...] = h
        zs_ref[...] = jnp.dot(h, ws_ref[...], preferred_element_type=F32)

    for (lo, hi), w_ref in zip(bounds, w_refs):
        @pl.when((j >= lo) & (j < hi))
        def _(w_ref=w_ref, hi=hi):
            z = jnp.dot(h_scr[...], w_ref[...], preferred_element_type=F32)
            z_ref[...] = z
            if hi > kv_lo:
                @pl.when(j >= kv_lo)
                def _():
                    zkv_ref[...] = z.astype(BF16)


def _inproj(x2, g_pre1, w_slabs, w_small, tm, tn=1024):
    n = x2.shape[0]
    bounds, lo = [], 0
    for w in w_slabs:
        bounds.append((lo, lo + w.shape[1] // tn))
        lo = bounds[-1][1]
    assert lo * tn == MAIN_WIDTH

    def slab_spec(lo, hi):
        return pl.BlockSpec((D_MODEL, tn), lambda i, j: (0, jnp.clip(j - lo, 0, hi - lo - 1)))

    kv_lo, kv_n = COL_SEL // tn, KV_COPY_WIDTH // tn
    assert COL_SEL + KV_COPY_WIDTH == MAIN_WIDTH
    return pl.pallas_call(
        functools.partial(_inproj_kernel, tuple(bounds), kv_lo),
        grid=(n // tm, MAIN_WIDTH // tn),
        in_specs=[
            pl.BlockSpec((tm, D_MODEL), lambda i, j: (i, 0)),
            pl.BlockSpec((1, D_MODEL), lambda i, j: (0, 0)),
        ] + [slab_spec(lo, hi) for lo, hi in bounds] + [
            pl.BlockSpec((D_MODEL, SMALL_WIDTH), lambda i, j: (0, 0)),
        ],
        out_specs=[
            pl.BlockSpec((tm, tn), lambda i, j: (i, j)),
            pl.BlockSpec((tm, SMALL_WIDTH), lambda i, j: (i, 0)),
            pl.BlockSpec((tm, tn), lambda i, j: (i, jnp.clip(j - kv_lo, 0, kv_n - 1))),
        ],
        out_shape=[jax.ShapeDtypeStruct((n, MAIN_WIDTH), F32),
                   jax.ShapeDtypeStruct((n, SMALL_WIDTH), F32),
                   jax.ShapeDtypeStruct((n, KV_COPY_WIDTH), BF16)],
        scratch_shapes=[pltpu.VMEM((tm, D_MODEL), BF16)],
        compiler_params=_cparams(("parallel", "arbitrary")),
        name="inproj",
    )(x2, g_pre1.reshape(1, D_MODEL), *w_slabs, w_small)


def _level_mask(ii, jj, s):
    return ((ii // s) % 2 == 1) & (jj // s == ii // s - 1)


def _gdn_kernel(tb, nh, t_valid,
                q_ref, k_ref, v_ref, qh_ref, kh_ref, vh_ref, cpq_ref, cpk_ref, cpv_ref,
                cwq_ref, cwk_ref, cwv_ref, z_ref, gm_ref, zs_ref, hp_ref, nw_ref, s0_ref,
                o_ref, sn_ref, ext_scr, gt_scr, s_scr):
    C = GDN_CHUNK
    nch = tb // C
    hg = pl.program_id(1)
    t = pl.program_id(2)
    nt = pl.num_programs(2)

    @pl.when(t == 0)
    def _():
        s_scr[...] = s0_ref[...]

    def conv_silu(u_ref, halo_ref, cp_ref, cw_ref):
        prev = jnp.where(t == 0, cp_ref[...], halo_ref[...])
        ext_scr[0:8, :] = prev
        ext_scr[8:8 + tb, :] = u_ref[...]
        w = cw_ref[...]
        acc = ext_scr[5:5 + tb, :] * w[0:1, :]
        for j in range(1, CONV_WIDTH):
            acc = acc + ext_scr[5 + j:5 + j + tb, :] * w[j:j + 1, :]
        return _silu(acc)

    q_all = conv_silu(q_ref, qh_ref, cpq_ref, cwq_ref)
    k_all = conv_silu(k_ref, kh_ref, cpk_ref, cwk_ref)
    v_all = conv_silu(v_ref, vh_ref, cpv_ref, cwv_ref)

    zs = zs_ref[...]
    lane = lax.broadcasted_iota(jnp.int32, (GDN_CHUNK, SMALL_WIDTH), 1)
    xa = zs + hp_ref[1:2, :]
    softplus = jnp.maximum(xa, 0.0) + jnp.log1p(jnp.exp(-jnp.abs(xa)))
    g_all = -jnp.exp(hp_ref[0:1, :]) * softplus
    beta_all = _sigmoid(zs)
    if t_valid < tb:
        live = lax.broadcasted_iota(jnp.int32, (tb, 1), 0) < t_valid
        g_all = jnp.where(live, g_all, 0.0)
        beta_all = jnp.where(live, beta_all, 0.0)

    ii = lax.broadcasted_iota(jnp.int32, (C, C), 0)
    jj = lax.broadcasted_iota(jnp.int32, (C, C), 1)
    tril = (ii >= jj).astype(F32)
    eye = (ii == jj).astype(F32)
    gcum = []
    for c in range(nch):
        gc = _hdot(tril, g_all[c * C:(c + 1) * C])
        gcum.append(gc)
        gt_scr[c] = jnp.transpose(gc)

    Ls, rhss, qkds, kdts, qgs, egls = [], [], [], [], [], []
    for hl in range(nh):
        hglob = hg * nh + hl
        hs = slice(hl * HEAD_DIM, (hl + 1) * HEAD_DIM)
        q = q_all[:, hs]
        k = k_all[:, hs]
        q = (q * lax.rsqrt(jnp.sum(q * q, axis=-1, keepdims=True) + RMS_EPS)) * QK_SCALE
        k = k * lax.rsqrt(jnp.sum(k * k, axis=-1, keepdims=True) + RMS_EPS)
        for c in range(nch):
            sl = slice(c * C, (c + 1) * C)
            qc, kc, vc = q[sl], k[sl], v_all[sl, hs]
            bcol = jnp.sum(jnp.where(lane == SMALL_B + hglob, beta_all[sl], 0.0), axis=1, keepdims=True)
            gcol = jnp.sum(jnp.where(lane == SMALL_A + hglob, gcum[c], 0.0), axis=1, keepdims=True)
            grow = gt_scr[c, pl.ds(SMALL_A + hglob, 1), :]
            decay = jnp.exp(jnp.where(ii >= jj, gcol - grow, NEG_INF))
            qkk = _bdot_nt(jnp.concatenate([qc, kc], axis=0), kc)
            Ls.append(jnp.where(ii > jj, qkk[C:] * decay, 0.0) * bcol)
            eg = jnp.exp(gcol)
            g_last = grow[:, C - 1:C]
            rhss.append(jnp.concatenate([vc * bcol, kc * (bcol * eg)], axis=1))
            qkds.append(qkk[:C] * decay)
            kdts.append(jnp.transpose(kc * jnp.exp(g_last - gcol)))
            qgs.append(qc * eg)
            egls.append(jnp.exp(g_last))

    n_inst = nh * nch
    Xs = [eye - jnp.where(_level_mask(ii, jj, 1), L, 0.0) for L in Ls]
    s = 2
    while s < C:
        m = _level_mask(ii, jj, s)
        Ys = [_bdot(jnp.where(m, Ls[i], 0.0), Xs[i]) for i in range(n_inst)]
        Zs = [_bdot(Xs[i], Ys[i]) for i in range(n_inst)]
        Xs = [Xs[i] - Zs[i] for i in range(n_inst)]
        s *= 2
    sols = [_bdot(Xs[i], rhss[i]) for i in range(n_inst)]
    res = [rhss[i] - sols[i] - _hdot(Ls[i], sols[i]) for i in range(n_inst)]
    sols = [sols[i] + _bdot(Xs[i], res[i]) for i in range(n_inst)]
    NPs = [_bdot(kdts[i], sols[i]) for i in range(n_inst)]
    QOs = [_bdot(qkds[i], sols[i]) for i in range(n_inst)]

    Ss = [s_scr[hl] for hl in range(nh)]
    outs = [None] * n_inst
    for c in range(nch):
        for hl in range(nh):
            i = hl * nch + c
            S = Ss[hl]
            outs[i] = _bdot(qgs[i] - QOs[i][:, HEAD_DIM:], S) + QOs[i][:, :HEAD_DIM]
            Ss[hl] = S * egls[i] - _bdot(NPs[i][:, HEAD_DIM:], S) + NPs[i][:, :HEAD_DIM]
    nw = nw_ref[...]
    for hl in range(nh):
        hs = slice(hl * HEAD_DIM, (hl + 1) * HEAD_DIM)
        s_scr[hl] = Ss[hl]
        for c in range(nch):
            sl = slice(c * C, (c + 1) * C)
            o = outs[hl * nch + c]
            o_ref[sl, hs] = _sigmoid(gm_ref[sl, hs]) * (((o * _rms_rows(o)) * nw) * _silu(z_ref[sl, hs]))

    @pl.when(t == nt - 1)
    def _():
        sn_ref[...] = s_scr[...]


def _gdn(z3, zs3, conv_prev, conv_w, hp, norm_w, s0, tb, t_valid, nh=4):
    B, T, _ = z3.shape
    nt = T // tb
    hb = tb // 8
    wblk = nh * HEAD_DIM
    cq, ck, cv = COL_QKV // wblk, (COL_QKV + GDN_WIDTH) // wblk, (COL_QKV + 2 * GDN_WIDTH) // wblk
    hpg = GDN_HEADS // nh

    def main(col0):
        return pl.BlockSpec((None, tb, wblk), lambda b, h, t: (b, t, col0 + h))

    def halo(col0):
        return pl.BlockSpec((None, 8, wblk), lambda b, h, t: (b, jnp.maximum(t * hb - 1, 0), col0 + h))

    def cprev(col0):
        return pl.BlockSpec((None, 8, wblk), lambda b, h, t: (b, 0, col0 + h))

    def cw(col0):
        return pl.BlockSpec((CONV_WIDTH, wblk), lambda b, h, t: (0, col0 + h))

    state = pl.BlockSpec((None, nh, HEAD_DIM, HEAD_DIM), lambda b, h, t: (b, h, 0, 0))
    return pl.pallas_call(
        functools.partial(_gdn_kernel, tb, nh, t_valid),
        grid=(B, hpg, nt),
        in_specs=[
            main(cq), main(ck), main(cv), halo(cq), halo(ck), halo(cv),
            cprev(0), cprev(hpg), cprev(2 * hpg), cw(0), cw(hpg), cw(2 * hpg),
            main(COL_ZA // wblk), main(COL_GMA // wblk),
            pl.BlockSpec((None, tb, SMALL_WIDTH), lambda b, h, t: (b, t, 0)),
            pl.BlockSpec((8, SMALL_WIDTH), lambda b, h, t: (0, 0)),
            pl.BlockSpec((1, HEAD_DIM), lambda b, h, t: (0, 0)),
            state,
        ],
        out_specs=[pl.BlockSpec((None, tb, wblk), lambda b, h, t: (b, t, h)), state],
        out_shape=[jax.ShapeDtypeStruct((B, T, GDN_WIDTH), F32),
                   jax.ShapeDtypeStruct((B, GDN_HEADS, HEAD_DIM, HEAD_DIM), F32)],
        scratch_shapes=[pltpu.VMEM((tb + 8, wblk), F32),
                        pltpu.VMEM((tb // GDN_CHUNK, SMALL_WIDTH, GDN_CHUNK), F32),
                        pltpu.VMEM((nh, HEAD_DIM, HEAD_DIM), F32)],
        compiler_params=_cparams(("parallel", "parallel", "arbitrary")),
        name="gdn",
    )(z3, z3, z3, z3, z3, z3, conv_prev, conv_prev, conv_prev, conv_w, conv_w, conv_w,
      z3, z3, zs3, hp, norm_w.reshape(1, HEAD_DIM), s0)


def _compress_kernel(x_ref, y_ref, w_ref, o_ref, xr_ref, yr_ref):
    rows = x_ref.shape[0]
    x = x_ref[...]
    o_ref[...] = jnp.sum(x.reshape(rows // CMP_BLOCK, CMP_BLOCK, 2 * NSA_KV_WIDTH) * w_ref[...][None], axis=1)
    for k in range(KV_ROW):
        cols = slice(k * HEAD_DIM, (k + 1) * HEAD_DIM)
        xr_ref[pl.ds(k, rows, stride=KV_ROW), :] = x[:, cols]
        yr_ref[pl.ds(k, rows, stride=KV_ROW), :] = y_ref[:, cols]


def _compress_prompt(z3, w_cmp2, tc=256):
    B, T, _ = z3.shape
    nc = T // CMP_BLOCK
    kvw = 2 * NSA_KV_WIDTH
    rows_spec = pl.BlockSpec((None, tc * KV_ROW, HEAD_DIM), lambda b, i: (b, i, 0))
    return pl.pallas_call(
        _compress_kernel,
        grid=(B, T // tc),
        in_specs=[pl.BlockSpec((None, tc, kvw), lambda b, i: (b, i, COL_CMP // kvw)),
                  pl.BlockSpec((None, tc, kvw), lambda b, i: (b, i, COL_SEL // kvw)),
                  pl.BlockSpec((CMP_BLOCK, kvw), lambda b, i: (0, 0))],
        out_specs=[pl.BlockSpec((None, tc // CMP_BLOCK, kvw), lambda b, i: (b, i, 0)), rows_spec, rows_spec],
        out_shape=[jax.ShapeDtypeStruct((B, nc, kvw), F32),
                   jax.ShapeDtypeStruct((B, T * KV_ROW, HEAD_DIM), F32),
                   jax.ShapeDtypeStruct((B, T * KV_ROW, HEAD_DIM), F32)],
        compiler_params=_cparams(("parallel", "parallel")),
        name="compress_prompt",
    )(z3, z3, w_cmp2)


def _topk_mask(sc, k_sel, axis):
    n = sc.shape[axis]
    idxf = lax.broadcasted_iota(jnp.int32, sc.shape, axis).astype(F32)
    sel = jnp.zeros(sc.shape, F32)
    for _ in range(k_sel):
        m = jnp.max(sc, axis=axis, keepdims=True)
        idx = jnp.min(jnp.where(sc == m, idxf, float(n)), axis=axis, keepdims=True)
        hit = idxf == idx
        sel = jnp.where(hit, 1.0, sel)
        sc = jnp.where(hit, -3.0, sc)
    return sel


def _cmp_attn_kernel(ta, nc, ns, ns_real, t_base, slopes_ref, q_ref, kvc_ref, zs_ref, oc_ref, sel_ref, sc_scr):
    t0 = t_base + pl.program_id(1) * ta
    tpos = t0 + lax.broadcasted_iota(jnp.int32, (ta, nc), 0)
    cend = lax.broadcasted_iota(jnp.int32, (ta, nc), 1) * CMP_BLOCK + (CMP_BLOCK - 1)
    dist = tpos - cend
    valid = dist >= 0
    distf = dist.astype(F32)
    gates = _sigmoid(zs_ref[...])
    blocks_on_rows = ta % 128 == 0
    tok_ax, blk_ax = (1, 0) if blocks_on_rows else (0, 1)
    sshape = (ns, ta) if blocks_on_rows else (ta, ns)
    pshape = (ns, nc) if blocks_on_rows else (nc, ns)
    pool = (lax.broadcasted_iota(jnp.int32, pshape, tok_ax) // (SEL_BLOCK // CMP_BLOCK)
            == lax.broadcasted_iota(jnp.int32, pshape, blk_ax)).astype(F32)
    tq = t0 + lax.broadcasted_iota(jnp.int32, sshape, tok_ax)
    blk = lax.broadcasted_iota(jnp.int32, sshape, blk_ax)
    cur = tq // SEL_BLOCK
    forced = (blk == 0) | (blk == cur) | (blk == cur - 1)
    avail = blk * SEL_BLOCK <= tq
    for g in range(NSA_KV_HEADS):
        kc = kvc_ref[:, g * HEAD_DIM:(g + 1) * HEAD_DIM]
        vc = kvc_ref[:, NSA_KV_WIDTH + g * HEAD_DIM:NSA_KV_WIDTH + (g + 1) * HEAD_DIM]
        imp = jnp.zeros((ta, nc), F32)
        for r in range(NSA_GROUP):
            hd = g * NSA_GROUP + r
            qh = q_ref[:, hd * HEAD_DIM:(hd + 1) * HEAD_DIM] * QK_SCALE
            s = _hdot_nt(qh, kc) - slopes_ref[hd] * distf
            s = jnp.where(valid, s, NEG_INF)
            e = jnp.exp(s - jnp.max(s, axis=1, keepdims=True))
            p = jnp.where(valid, e / jnp.sum(e, axis=1, keepdims=True), 0.0)
            imp = imp + p
            gi = SMALL_GN + hd * 3
            oc_ref[:, hd * HEAD_DIM:(hd + 1) * HEAD_DIM] = gates[:, gi:gi + 1] * _bdot(p, vc)
        imps = _hdot_nt(pool, imp) if blocks_on_rows else _hdot(imp, pool)
        score = jnp.where(forced, imps + FORCE_BONUS, jnp.where(avail, imps, -1.0))
        if ns_real < ns:
            score = jnp.where(blk < ns_real, score, -2.0)
        if blocks_on_rows:
            sc_scr[:, g * ta:(g + 1) * ta] = score
        else:
            sc_scr[g * ta:(g + 1) * ta, :] = score
    sel = _topk_mask(sc_scr[...], min(SEL_TOPK, ns_real), blk_ax)
    for g in range(NSA_KV_HEADS):
        if blocks_on_rows:
            sel_ref[g] = jnp.transpose(sel[:, g * ta:(g + 1) * ta])
        else:
            sel_ref[g] = sel[g * ta:(g + 1) * ta, :]


def _cmp_attn(z3, zs3, kvc, slopes, ta, n_tok, ns, ns_real, t_base, name):
    B = z3.shape[0]
    nc = kvc.shape[1]
    return pl.pallas_call(
        functools.partial(_cmp_attn_kernel, ta, nc, ns, ns_real, t_base),
        grid_spec=pltpu.PrefetchScalarGridSpec(
            num_scalar_prefetch=1,
            grid=(B, n_tok // ta),
            in_specs=[
                pl.BlockSpec((None, ta, NSA_HEADS * HEAD_DIM), lambda b, i, sl: (b, i, COL_QN // 2048)),
                pl.BlockSpec((None, nc, 2 * NSA_KV_WIDTH), lambda b, i, sl: (b, 0, 0)),
                pl.BlockSpec((None, ta, SMALL_WIDTH), lambda b, i, sl: (b, i, 0)),
            ],
            out_specs=[
                pl.BlockSpec((None, ta, NSA_HEADS * HEAD_DIM), lambda b, i, sl: (b, i, 0)),
                pl.BlockSpec((None, NSA_KV_HEADS, ta, ns), lambda b, i, sl: (b, 0, i, 0)),
            ],
            scratch_shapes=[pltpu.VMEM((ns, NSA_KV_HEADS * ta) if ta % 128 == 0 else (NSA_KV_HEADS * ta, ns), F32)],
        ),
        out_shape=[jax.ShapeDtypeStruct((B, n_tok, NSA_HEADS * HEAD_DIM), F32),
                   jax.ShapeDtypeStruct((B, NSA_KV_HEADS, n_tok, ns), F32)],
        compiler_params=_cparams(("parallel", "parallel")),
        name=name,
    )(slopes, z3, kvc, zs3)


def _sel_win_kernel(T, QB, ns, tkv, wspan, slopes_ref, q_ref, ks_ref, vs_ref, kw_ref, vw_ref, sel_ref,
                    zs_ref, oc_ref, ga_ref, gmn_ref, o_ref):
    R = NSA_GROUP
    g = pl.program_id(1)
    qb = pl.program_id(2)
    t0 = qb * QB
    q = jnp.concatenate([(q_ref[:, r * HEAD_DIM:(r + 1) * HEAD_DIM] * (QK_SCALE * LOG2E)).astype(BF16)
                         for r in range(R)], axis=0)
    slope2 = [slopes_ref[g * R + r] * LOG2E for r in range(R)]
    selb = sel_ref[...].astype(BF16)

    def head_rows(x, r):
        return x[r * QB:(r + 1) * QB]

    def spread(cols):
        return jnp.concatenate([jnp.broadcast_to(c, (QB, 2 * HEAD_DIM)) for c in cols], axis=0)

    ti = lax.broadcasted_iota(jnp.int32, (QB, tkv), 0)
    kj = lax.broadcasted_iota(jnp.int32, (QB, tkv), 1)
    d0 = (ti - kj).astype(F32)
    eb = lax.broadcasted_iota(jnp.int32, (ns, tkv), 0)
    ek = lax.broadcasted_iota(jnp.int32, (ns, tkv), 1) // SEL_BLOCK

    def sel_scores(j):
        k0 = j * tkv
        expand = (eb == ek + j * (tkv // SEL_BLOCK)).astype(BF16)
        keymask = jnp.dot(selb, expand, preferred_element_type=F32)
        distf = d0 + (t0 - k0).astype(F32)
        base = jnp.where((distf >= 0.0) & (keymask > 0.5), distf, MASK_DIST)
        return _bdot_nt(q, ks_ref[k0:k0 + tkv, :]), base

    def with_ones(v):
        return jnp.concatenate([v.astype(BF16), jnp.ones(v.shape, BF16)], axis=1)

    def sel_tile(j, s, base, carry):
        ms, acc = carry
        k0 = j * tkv
        ps, ms2, alphas = [], [], []
        for r in range(R):
            sr = head_rows(s, r) - slope2[r] * base
            m_new = jnp.maximum(ms[r], jnp.max(sr, axis=1, keepdims=True))
            ps.append(jnp.exp2((sr - m_new).astype(BF16)))
            alphas.append(jnp.exp2(ms[r] - m_new))
            ms2.append(m_new)
        pv = jnp.dot(jnp.concatenate(ps, axis=0), with_ones(vs_ref[k0:k0 + tkv, :]),
                     preferred_element_type=F32)
        return ms2, spread(alphas) * acc + pv

    def window():
        kstart = pl.multiple_of(jnp.clip(t0 - WINDOW, 0, T - wspan), QB)
        kw = kw_ref[pl.ds(kstart, wspan), :]
        vw = vw_ref[pl.ds(kstart, wspan), :]
        tw = lax.broadcasted_iota(jnp.int32, (QB, wspan), 0)
        kwj = lax.broadcasted_iota(jnp.int32, (QB, wspan), 1)
        dist = (tw - kwj) + (t0 - kstart)
        base_w = jnp.where((dist >= 0) & (dist < WINDOW), dist.astype(F32), MASK_DIST)
        s = _bdot_nt(q, kw)
        pw = []
        for r in range(R):
            sr = head_rows(s, r) - slope2[r] * base_w
            pw.append(jnp.exp2((sr - jnp.max(sr, axis=1, keepdims=True)).astype(BF16)))
        return jnp.dot(jnp.concatenate(pw, axis=0), with_ones(vw), preferred_element_type=F32)

    def run(n_tiles):
        carry = ([jnp.full((QB, 1), NEG_INF, F32) for _ in range(R)],
                 jnp.zeros((R * QB, 2 * HEAD_DIM), F32))
        nxt = sel_scores(0)
        acc_w = window()
        for j in range(n_tiles):
            cur, nxt = nxt, (sel_scores(j + 1) if j + 1 < n_tiles else None)
            carry = sel_tile(j, cur[0], cur[1], carry)
        acc_s = carry[1]
        o_s = acc_s[:, :HEAD_DIM] / acc_s[:, HEAD_DIM:]
        o_w = acc_w[:, :HEAD_DIM] / acc_w[:, HEAD_DIM:]
        gates = _sigmoid(zs_ref[...])
        lane = lax.broadcasted_iota(jnp.int32, (QB, SMALL_WIDTH), 1)
        for r in range(R):
            gi = SMALL_GN + (g * R + r) * 3
            g_s = jnp.sum(jnp.where(lane == gi + 1, gates, 0.0), axis=1, keepdims=True)
            g_w = jnp.sum(jnp.where(lane == gi + 2, gates, 0.0), axis=1, keepdims=True)
            cs = slice(r * HEAD_DIM, (r + 1) * HEAD_DIM)
            o_n = oc_ref[:, cs] + g_s * head_rows(o_s, r) + g_w * head_rows(o_w, r)
            o_ref[:, cs] = (ga_ref[:, cs] + _sigmoid(gmn_ref[:, cs]) * o_n).astype(BF16)

    need = (t0 + QB + tkv - 1) // tkv
    for n_tiles in range(1, T // tkv + 1):
        pl.when(need == n_tiles)(functools.partial(run, n_tiles))


def _sel_win_prompt(z3, zkv3, zs3, selmask, oc, ga, slopes):
    B, T, _ = z3.shape
    ns = T // SEL_BLOCK
    tkv = min(1024, T)
    qb = 128
    wspan = min(WINDOW + 2 * qb, T)
    gw = NSA_GROUP * HEAD_DIM

    def kv(col0):
        return pl.BlockSpec((None, T, HEAD_DIM), lambda b, g, i, sl: (b, 0, (col0 - COL_SEL) // 128 + g))

    return pl.pallas_call(
        functools.partial(_sel_win_kernel, T, qb, ns, tkv, wspan),
        grid_spec=pltpu.PrefetchScalarGridSpec(
            num_scalar_prefetch=1,
            grid=(B, NSA_KV_HEADS, T // qb),
            in_specs=[
                pl.BlockSpec((None, qb, gw), lambda b, g, i, sl: (b, i, COL_QN // gw + g)),
                kv(COL_SEL), kv(COL_SEL + NSA_KV_WIDTH), kv(COL_WIN), kv(COL_WIN + NSA_KV_WIDTH),
                pl.BlockSpec((None, None, qb, ns), lambda b, g, i, sl: (b, g, i, 0)),
                pl.BlockSpec((None, qb, SMALL_WIDTH), lambda b, g, i, sl: (b, i, 0)),
                pl.BlockSpec((None, qb, gw), lambda b, g, i, sl: (b, i, g)),
                pl.BlockSpec((None, qb, gw), lambda b, g, i, sl: (b, i, g)),
                pl.BlockSpec((None, qb, gw), lambda b, g, i, sl: (b, i, COL_GMN // gw + g)),
            ],
            out_specs=pl.BlockSpec((None, qb, gw), lambda b, g, i, sl: (b, i, g)),
        ),
        out_shape=jax.ShapeDtypeStruct((B, T, D_MODEL), BF16),
        compiler_params=_cparams(("parallel", "parallel", "arbitrary")),
        name="sel_win_prompt",
    )(slopes, z3, zkv3, zkv3, zkv3, zkv3, selmask, zs3, oc, ga, z3)


def _merge_kernel(mixed_ref, x_ref, w_ref, g_ref, o_ref):
    y = jnp.dot(mixed_ref[...].astype(BF16), w_ref[...], preferred_element_type=F32)
    o_ref[...] = x_ref[...] + (y * _rms_rows(y)) * g_ref[...]


def _merge_out(mixed2, x2, w_o, g_post1, tm):
    n = x2.shape[0]
    row = pl.BlockSpec((tm, D_MODEL), lambda i: (i, 0))
    return pl.pallas_call(
        _merge_kernel,
        grid=(n // tm,),
        in_specs=[row, row,
                  pl.BlockSpec((D_MODEL, D_MODEL), lambda i: (0, 0)),
                  pl.BlockSpec((1, D_MODEL), lambda i: (0, 0))],
        out_specs=row,
        out_shape=jax.ShapeDtypeStruct((n, D_MODEL), F32),
        compiler_params=_cparams(("parallel",)),
        name="merge_out",
    )(mixed2, x2, w_o, g_post1.reshape(1, D_MODEL))


def _ffn_kernel(x_ref, g2_ref, wg_ref, wu_ref, wo_ref, gp_ref, p_ref, wpg_ref, wp_ref, o_ref, h_scr, acc_scr):
    j = pl.program_id(1)

    @pl.when(j == 0)
    def _():
        x = x_ref[...]
        h_scr[...] = ((x * _rms_rows(x)) * g2_ref[...]).astype(BF16)
        acc_scr[...] = jnp.zeros_like(acc_scr)

    h = h_scr[...]
    gt = jnp.dot(h, wg_ref[...], preferred_element_type=F32)
    up = jnp.dot(h, wu_ref[...], preferred_element_type=F32)
    acc_scr[...] += jnp.dot((_silu(gt) * up).astype(BF16), wo_ref[...], preferred_element_type=F32)

    @pl.when(j == pl.num_programs(1) - 1)
    def _():
        y = acc_scr[...]
        x = x_ref[...] + (y * _rms_rows(y)) * gp_ref[...]
        gate = _sigmoid(jnp.dot(x.astype(BF16), wpg_ref[...], preferred_element_type=F32))
        o_ref[...] = x + gate * jnp.dot(p_ref[...].astype(BF16), wp_ref[...], preferred_element_type=F32)


def _ffn_ple(x2, ple2, g_pre2, w_ffn_in, w_ffn_out, g_post2, w_ple_gate, w_ple, tm, th=512):
    n = x2.shape[0]
    nh = FFN_HIDDEN // th
    once = pl.Buffered(1)
    return pl.pallas_call(
        _ffn_kernel,
        grid=(n // tm, nh),
        in_specs=[
            pl.BlockSpec((tm, D_MODEL), lambda i, j: (i, 0)),
            pl.BlockSpec((1, D_MODEL), lambda i, j: (0, 0)),
            pl.BlockSpec((D_MODEL, th), lambda i, j: (0, j)),
            pl.BlockSpec((D_MODEL, th), lambda i, j: (0, nh + j)),
            pl.BlockSpec((th, D_MODEL), lambda i, j: (j, 0)),
            pl.BlockSpec((1, D_MODEL), lambda i, j: (0, 0)),
            pl.BlockSpec((tm, PLE_DIM), lambda i, j: (i, 0)),
            pl.BlockSpec((D_MODEL, D_MODEL), lambda i, j: (0, 0), pipeline_mode=once),
            pl.BlockSpec((PLE_DIM, D_MODEL), lambda i, j: (0, 0), pipeline_mode=once),
        ],
        out_specs=pl.BlockSpec((tm, D_MODEL), lambda i, j: (i, 0)),
        out_shape=jax.ShapeDtypeStruct((n, D_MODEL), F32),
        scratch_shapes=[pltpu.VMEM((tm, D_MODEL), BF16), pltpu.VMEM((tm, D_MODEL), F32)],
        compiler_params=_cparams(("parallel", "arbitrary")),
        name="ffn_ple",
    )(x2, g_pre2.reshape(1, D_MODEL), w_ffn_in, w_ffn_in, w_ffn_out, g_post2.reshape(1, D_MODEL),
      ple2, w_ple_gate, w_ple)


SAMPLE_ROWS = 8
KV_ROW = 2 * NSA_KV_HEADS


def _compress_pages_kernel(npg, pt_ref, *refs):
    w = refs[npg][...]
    o_ref = refs[npg + 1]
    per = PAGE_SIZE // CMP_BLOCK
    for p in range(npg):
        x = refs[p][...].reshape(per, CMP_BLOCK, KV_ROW, HEAD_DIM)
        o_ref[p * per:(p + 1) * per] = jnp.sum(x * w[None], axis=1)


def _compress_sample(cache_rows, page_table, w_cmp3, npg=16):
    Bs, n_pages = page_table.shape
    per = PAGE_SIZE // CMP_BLOCK

    def page(p):
        return pl.BlockSpec((None, PAGE_SIZE * KV_ROW, HEAD_DIM), lambda b, i, pt: (pt[b, i * npg + p], 0, 0))

    return pl.pallas_call(
        functools.partial(_compress_pages_kernel, npg),
        grid_spec=pltpu.PrefetchScalarGridSpec(
            num_scalar_prefetch=1,
            grid=(Bs, n_pages // npg),
            in_specs=[page(p) for p in range(npg)]
            + [pl.BlockSpec((CMP_BLOCK, KV_ROW, HEAD_DIM), lambda b, i, pt: (0, 0, 0))],
            out_specs=pl.BlockSpec((None, npg * per, KV_ROW, HEAD_DIM), lambda b, i, pt: (b, i, 0, 0)),
        ),
        out_shape=jax.ShapeDtypeStruct((Bs, n_pages * per, KV_ROW, HEAD_DIM), F32),
        compiler_params=_cparams(("parallel", "parallel")),
        name="compress_sample",
    )(page_table, *([cache_rows] * npg), w_cmp3)


def _sel_pages_kernel(npg, past_len, phys_ref, lst_ref, cnt_ref, slopes_ref, q_ref, sel_ref, *refs):
    pages = refs[:npg]
    m_ref, l_ref, acc_ref, q_scr, selrows_scr, slope_scr = refs[npg:]
    b = pl.program_id(0)
    i = pl.program_id(1)
    nrow = NSA_HEADS * SAMPLE_ROWS
    rg = NSA_GROUP * SAMPLE_ROWS
    ns_pad = sel_ref.shape[-1]

    @pl.when(i == 0)
    def _():
        for hd in range(NSA_HEADS):
            rs = slice(hd * SAMPLE_ROWS, (hd + 1) * SAMPLE_ROWS)
            q_scr[rs, :] = q_ref[:, hd * HEAD_DIM:(hd + 1) * HEAD_DIM] * (QK_SCALE * LOG2E)
            selrows_scr[rs, :] = sel_ref[hd // NSA_GROUP]
            slope_scr[rs, :] = jnp.full((SAMPLE_ROWS, HEAD_DIM), slopes_ref[hd] * LOG2E, F32)
        m_ref[...] = jnp.full(m_ref.shape, NEG_INF, F32)
        l_ref[...] = jnp.zeros(l_ref.shape, F32)
        acc_ref[...] = jnp.zeros(acc_ref.shape, F32)

    @pl.when(i * npg < cnt_ref[b])
    def _():
        qb = q_scr[...].astype(BF16)
        selb = selrows_scr[...].astype(BF16)
        slope = slope_scr[...]
        row = lax.broadcasted_iota(jnp.int32, (nrow, PAGE_SIZE), 0)
        pos = lax.broadcasted_iota(jnp.int32, (nrow, PAGE_SIZE), 1)
        d0 = (past_len + row % SAMPLE_ROWS - pos).astype(F32)
        ob = lax.broadcasted_iota(jnp.int32, (ns_pad, PAGE_SIZE), 0)
        ol = lax.broadcasted_iota(jnp.int32, (ns_pad, PAGE_SIZE), 1) // SEL_BLOCK
        scores = []
        for p in range(npg):
            k = i * npg + p
            pg = lst_ref[b, k]
            flags = jnp.dot(selb, (ob == ol + pg * (PAGE_SIZE // SEL_BLOCK)).astype(BF16),
                            preferred_element_type=F32)
            distf = d0 - (pg * PAGE_SIZE).astype(F32)
            base = jnp.where((flags > 0.5) & (distf >= 0.0), distf, MASK_DIST)
            base = jnp.where(k < cnt_ref[b], base, MASK_DIST)
            scores.append(jnp.concatenate(
                [_bdot_nt(qb[g * rg:(g + 1) * rg], pages[p][pl.ds(g, PAGE_SIZE, stride=KV_ROW), :])
                 for g in range(NSA_KV_HEADS)], axis=0) - slope * base)
        s = jnp.concatenate(scores, axis=1)
        m_old = m_ref[...]
        m_new = jnp.maximum(m_old, jnp.max(s, axis=1, keepdims=True))
        pr = jnp.exp2(s - m_new[:, 0:1])
        alpha = jnp.exp2(m_old - m_new)
        l_ref[...] = alpha * l_ref[...] + jnp.sum(pr, axis=1, keepdims=True)
        prb = pr.astype(BF16)
        pv = jnp.zeros((nrow, HEAD_DIM), F32)
        for p in range(npg):
            ps = prb[:, p * PAGE_SIZE:(p + 1) * PAGE_SIZE]
            pv = pv + jnp.concatenate(
                [_bdot(ps[g * rg:(g + 1) * rg], pages[p][pl.ds(NSA_KV_HEADS + g, PAGE_SIZE, stride=KV_ROW), :])
                 for g in range(NSA_KV_HEADS)], axis=0)
        acc_ref[...] = alpha * acc_ref[...] + pv
        m_ref[...] = m_new


def _sel_sample(z3s, selmask, cache_rows, page_table, slopes, past_len, npg=8):
    Bs, n_pages = page_table.shape
    nrow = NSA_HEADS * SAMPLE_ROWS
    ns_pad = selmask.shape[-1]
    per_page = PAGE_SIZE // SEL_BLOCK
    picked = selmask[..., :n_pages * per_page].reshape(Bs, -1, n_pages, per_page).max(axis=(1, 3)) > 0.5
    cnt = picked.sum(axis=1).astype(jnp.int32)
    order = jnp.argsort(jnp.logical_not(picked), axis=1, stable=True).astype(jnp.int32)
    last = jnp.take_along_axis(order, jnp.maximum(cnt - 1, 0)[:, None], axis=1)
    lst = jnp.where(lax.broadcasted_iota(jnp.int32, order.shape, 1) < cnt[:, None], order, last)
    phys = jnp.take_along_axis(page_table, lst, axis=1)

    def page(p):
        return pl.BlockSpec((None, PAGE_SIZE * KV_ROW, HEAD_DIM),
                            lambda b, i, ph, ls, ct, sl: (ph[b, i * npg + p], 0, 0))

    part = pl.BlockSpec((None, nrow, HEAD_DIM), lambda b, i, ph, ls, ct, sl: (b, 0, 0))
    return pl.pallas_call(
        functools.partial(_sel_pages_kernel, npg, past_len),
        grid_spec=pltpu.PrefetchScalarGridSpec(
            num_scalar_prefetch=4,
            grid=(Bs, n_pages // npg),
            in_specs=[
                pl.BlockSpec((None, SAMPLE_ROWS, NSA_HEADS * HEAD_DIM),
                             lambda b, i, ph, ls, ct, sl: (b, 0, COL_QN // 2048)),
                pl.BlockSpec((None, NSA_KV_HEADS, SAMPLE_ROWS, ns_pad), lambda b, i, ph, ls, ct, sl: (b, 0, 0, 0)),
            ] + [page(p) for p in range(npg)],
            out_specs=[part, part, part],
            scratch_shapes=[pltpu.VMEM((nrow, HEAD_DIM), F32), pltpu.VMEM((nrow, ns_pad), F32),
                            pltpu.VMEM((nrow, HEAD_DIM), F32)],
        ),
        out_shape=[jax.ShapeDtypeStruct((Bs, nrow, HEAD_DIM), F32)] * 3,
        compiler_params=_cparams(("parallel", "arbitrary")),
        name="sel_sample",
    )(phys, lst, cnt, slopes, z3s, selmask, *([cache_rows] * npg))


def _finish_sample_kernel(past_len, t_real, nnew, slopes_ref, q_ref, kst_ref, snew_ref, wnew_ref, sel_ref,
                          m_ref, l_ref, acc_ref, oc_ref, zs_ref, ga_ref, gmn_ref, o_ref):
    nst = kst_ref.shape[0]
    cur = past_len // SEL_BLOCK
    gates = _sigmoid(zs_ref[...])
    t_new = lax.broadcasted_iota(jnp.int32, (SAMPLE_ROWS, nnew), 0)
    j_new = lax.broadcasted_iota(jnp.int32, (SAMPLE_ROWS, nnew), 1)
    dist_new = t_new - j_new
    ok_new = (dist_new >= 0) & (j_new < t_real)
    t_st = lax.broadcasted_iota(jnp.int32, (SAMPLE_ROWS, nst), 0)
    i_st = lax.broadcasted_iota(jnp.int32, (SAMPLE_ROWS, nst), 1)
    dist_st = t_st + nst - i_st
    ok_st = dist_st < WINDOW
    for hd in range(NSA_HEADS):
        g = hd // NSA_GROUP
        rs = slice(hd * SAMPLE_ROWS, (hd + 1) * SAMPLE_ROWS)
        kc = slice(g * HEAD_DIM, (g + 1) * HEAD_DIM)
        vc = slice(NSA_KV_WIDTH + g * HEAD_DIM, NSA_KV_WIDTH + (g + 1) * HEAD_DIM)
        sl = slopes_ref[hd] * LOG2E
        qh = (q_ref[:, hd * HEAD_DIM:(hd + 1) * HEAD_DIM] * (QK_SCALE * LOG2E)).astype(BF16)
        valid = ok_new & (sel_ref[g][:, cur:cur + 1] > 0.5)
        s = jnp.where(valid, _bdot_nt(qh, snew_ref[:, kc]) - sl * dist_new.astype(F32), NEG_INF)
        m_old = m_ref[rs, 0:1]
        m_new = jnp.maximum(m_old, jnp.max(s, axis=1, keepdims=True))
        pr = jnp.where(valid, jnp.exp2(s - m_new), 0.0)
        alpha = jnp.exp2(m_old - m_new)
        l = alpha * l_ref[rs, 0:1] + jnp.sum(pr, axis=1, keepdims=True)
        o_s = (alpha * acc_ref[rs, :] + _bdot(pr, snew_ref[:, vc])) / l
        s1 = jnp.where(ok_st, _bdot_nt(qh, kst_ref[:, kc]) - sl * dist_st.astype(F32), NEG_INF)
        s2 = jnp.where(ok_new, _bdot_nt(qh, wnew_ref[:, kc]) - sl * dist_new.astype(F32), NEG_INF)
        mw = jnp.maximum(jnp.max(s1, axis=1, keepdims=True), jnp.max(s2, axis=1, keepdims=True))
        e1 = jnp.where(ok_st, jnp.exp2(s1 - mw), 0.0)
        e2 = jnp.where(ok_new, jnp.exp2(s2 - mw), 0.0)
        den = jnp.sum(e1, axis=1, keepdims=True) + jnp.sum(e2, axis=1, keepdims=True)
        o_w = _bdot(e1 / den, kst_ref[:, vc]) + _bdot(e2 / den, wnew_ref[:, vc])
        gi = SMALL_GN + hd * 3
        cs = slice(hd * HEAD_DIM, (hd + 1) * HEAD_DIM)
        o_n = oc_ref[:, cs] + gates[:, gi + 1:gi + 2] * o_s + gates[:, gi + 2:gi + 3] * o_w
        o_ref[:, cs] = ga_ref[:, cs] + _sigmoid(gmn_ref[:, cs]) * o_n


def _finish_sample(z3s, zs3s, state_win2, selmask, m, l, acc, oc, ga, slopes, past_len, t_real):
    Bs, tz, _ = z3s.shape
    nst = state_win2.shape[1]
    nrow = NSA_HEADS * SAMPLE_ROWS
    ns_pad = selmask.shape[-1]
    kvw = 2 * NSA_KV_WIDTH
    part = pl.BlockSpec((None, nrow, HEAD_DIM), lambda b, sl: (b, 0, 0))
    wide = pl.BlockSpec((None, SAMPLE_ROWS, NSA_HEADS * HEAD_DIM), lambda b, sl: (b, 0, 0))
    return pl.pallas_call(
        functools.partial(_finish_sample_kernel, past_len, t_real, tz),
        grid_spec=pltpu.PrefetchScalarGridSpec(
            num_scalar_prefetch=1,
            grid=(Bs,),
            in_specs=[
                pl.BlockSpec((None, SAMPLE_ROWS, NSA_HEADS * HEAD_DIM), lambda b, sl: (b, 0, COL_QN // 2048)),
                pl.BlockSpec((None, nst, kvw), lambda b, sl: (b, 0, 0)),
                pl.BlockSpec((None, tz, kvw), lambda b, sl: (b, 0, COL_SEL // kvw)),
                pl.BlockSpec((None, tz, kvw), lambda b, sl: (b, 0, COL_WIN // kvw)),
                pl.BlockSpec((None, NSA_KV_HEADS, SAMPLE_ROWS, ns_pad), lambda b, sl: (b, 0, 0, 0)),
                part, part, part, wide,
                pl.BlockSpec((None, SAMPLE_ROWS, SMALL_WIDTH), lambda b, sl: (b, 0, 0)),
                wide,
                pl.BlockSpec((None, SAMPLE_ROWS, D_MODEL), lambda b, sl: (b, 0, COL_GMN // D_MODEL)),
            ],
            out_specs=wide,
        ),
        out_shape=jax.ShapeDtypeStruct((Bs, SAMPLE_ROWS, NSA_HEADS * HEAD_DIM), F32),
        compiler_params=_cparams(("parallel",)),
        name="finish_sample",
    )(slopes, z3s, state_win2, z3s, z3s, selmask, m, l, acc, oc, zs3s, ga, z3s)


def _mix_and_ffn(mixed, x3, ple3, wts, tm):
    B, T, _ = x3.shape
    n = B * T
    x1 = _merge_out(mixed.reshape(n, D_MODEL), x3.reshape(n, D_MODEL), wts["w_o"], wts["g_post1"], tm)
    x3o = _ffn_ple(x1, ple3.reshape(n, PLE_DIM), wts["g_pre2"], wts["w_ffn_in"], wts["w_ffn_out"], wts["g_post2"],
                   wts["w_ple_gate"], wts["w_ple"], tm)
    return x3o.reshape(B, T, D_MODEL)


def kernel(x_prompt, x_sample, cache_cmp_kv, cache_sel_kv, page_table, state_win_kv, state_gdn, state_conv, p_prompt, p_sample, g_pre1, w_in, conv_w, A_log, dt_bias, gdn_norm_w, w_cmp, w_o, g_post1, g_pre2, w_ffn_in, w_ffn_out, g_post2, w_ple, w_ple_gate):
    B, T, _ = x_prompt.shape
    Bs, Ts, _ = x_sample.shape
    n_pages = page_table.shape[1]
    past_len = n_pages * PAGE_SIZE
    win_buf = state_win_kv.shape[2]
    kvh = (2, NSA_KV_HEADS, HEAD_DIM)
    qkv_w = 3 * GDN_WIDTH

    wi = w_in[0]
    w_main = (jnp.concatenate([wi[:, 0:8192], wi[:, 13392:17488], wi[:, 8224:13344]], axis=1).astype(BF16),)
    w_small = jnp.concatenate([wi[:, 8192:8224], wi[:, 13344:13392],
                               jnp.zeros((D_MODEL, SMALL_WIDTH - 80), F32)], axis=1).astype(BF16)
    wts = dict(w_o=w_o[0].astype(BF16), g_post1=g_post1[0], g_pre2=g_pre2[0],
               w_ffn_in=w_ffn_in[0].astype(BF16), w_ffn_out=w_ffn_out[0].astype(BF16), g_post2=g_post2[0],
               w_ple=w_ple[0].astype(BF16), w_ple_gate=w_ple_gate[0].astype(BF16))
    hp = jnp.zeros((8, SMALL_WIDTH), F32).at[0, 0:GDN_HEADS].set(A_log[0]).at[1, 0:GDN_HEADS].set(dt_bias[0])
    w_cmp2 = w_cmp[0].reshape(CMP_BLOCK, 2 * NSA_KV_WIDTH)
    heads = jnp.arange(1, NSA_HEADS + 1, dtype=F32)
    slopes = jnp.exp2(-8.0 * heads / NSA_HEADS)

    z2, zs2, zkv2 = _inproj(x_prompt.reshape(B * T, D_MODEL), g_pre1[0], w_main, w_small, 1024)
    z3, zs3 = z2.reshape(B, T, MAIN_WIDTH), zs2.reshape(B, T, SMALL_WIDTH)
    o_a, s_new_p = _gdn(z3, zs3, jnp.zeros((B, 8, qkv_w), F32), conv_w[0], hp, gdn_norm_w[0],
                        jnp.zeros((B, GDN_HEADS, HEAD_DIM, HEAD_DIM), F32), 512, 512)
    kvc, cmp_rows, sel_rows = _compress_prompt(z3, w_cmp2)
    ns = T // SEL_BLOCK
    oc, selmask = _cmp_attn(z3, zs3, kvc, slopes, 1024, T, ns, ns, 0, "cmp_attn_prompt")
    mixed = _sel_win_prompt(z3, zkv2.reshape(B, T, KV_COPY_WIDTH), zs3, selmask, oc, o_a, slopes)
    y_prompt = _mix_and_ffn(mixed, x_prompt, p_prompt[0], wts, 512)

    tz = GDN_CHUNK
    xs = jnp.pad(x_sample, ((0, 0), (0, tz - Ts), (0, 0)))
    zs2_, zss2, _ = _inproj(xs.reshape(Bs * tz, D_MODEL), g_pre1[0], w_main, w_small, Bs * tz)
    z3s, zs3s = zs2_.reshape(Bs, tz, MAIN_WIDTH), zss2.reshape(Bs, tz, SMALL_WIDTH)
    conv_prev = jnp.pad(state_conv[0], ((0, 0), (8 - (CONV_WIDTH - 1), 0), (0, 0)))
    o_a_s, s_new_s = _gdn(z3s, zs3s, conv_prev, conv_w[0], hp, gdn_norm_w[0], state_gdn[0], tz, Ts, nh=GDN_HEADS)
    n_pool = cache_cmp_kv.shape[1]
    kvc_s = _compress_sample(cache_cmp_kv[0].reshape(n_pool, PAGE_SIZE * KV_ROW, HEAD_DIM), page_table,
                             w_cmp[0].reshape(CMP_BLOCK, KV_ROW, HEAD_DIM))
    kvc_s = kvc_s.reshape(Bs, kvc_s.shape[1], 2 * NSA_KV_WIDTH)
    ns_real = -(-(past_len + Ts) // SEL_BLOCK)
    ns_pad = -(-ns_real // 128) * 128
    oc_s, selmask_s = _cmp_attn(z3s, zs3s, kvc_s, slopes, SAMPLE_ROWS, SAMPLE_ROWS, ns_pad, ns_real, past_len,
                                "cmp_attn_sample")
    m_s, l_s, acc_s = _sel_sample(z3s, selmask_s, cache_sel_kv[0].reshape(n_pool, PAGE_SIZE * KV_ROW, HEAD_DIM),
                                  page_table, slopes, past_len)
    state_win2 = state_win_kv[0].reshape(Bs, win_buf, 2 * NSA_KV_WIDTH)
    mixed_s = _finish_sample(z3s, zs3s, state_win2, selmask_s, m_s, l_s, acc_s, oc_s, o_a_s, slopes, past_len, Ts)
    mixed_s = jnp.pad(mixed_s, ((0, 0), (0, tz - SAMPLE_ROWS), (0, 0)))
    ps = jnp.pad(p_sample[0], ((0, 0), (0, tz - Ts), (0, 0)))
    y_sample = _mix_and_ffn(mixed_s, xs, ps, wts, Bs * tz)[:, :Ts]

    def kv_rows(z, col, lo, hi):
        return z[:, lo:hi, col:col + 2 * NSA_KV_WIDTH].reshape((z.shape[0], hi - lo) + kvh)

    new_win_s = jnp.concatenate([state_win_kv[0][:, Ts:], kv_rows(z3s, COL_WIN, 0, Ts)], axis=1)
    return (y_prompt, y_sample,
            cmp_rows.reshape((1, B, T) + kvh), sel_rows.reshape((1, B, T) + kvh),
            kv_rows(z3, COL_WIN, T - win_buf, T)[None], s_new_p[None],
            z3[:, T - (CONV_WIDTH - 1):, 0:qkv_w][None],
            kv_rows(z3s, COL_CMP, 0, Ts)[None], kv_rows(z3s, COL_SEL, 0, Ts)[None],
            new_win_s[None], s_new_s[None],
            z3s[:, Ts - (CONV_WIDTH - 1):Ts, 0:qkv_w][None])
```
